```python
import math
import jax
import jax.numpy as jnp
from jax import lax
import numpy as np

D_MODEL = 2048
BATCH = 2
SEQ = 16384
DEPTH = 4

N_MEM = 256
N_BRANCH = 4
MIX_W = D_MODEL // 4
HEAD_DIM = 128
N_MIX_HEADS = MIX_W // HEAD_DIM

HG_HEADS = N_MIX_HEADS
HG_DK = HEAD_DIM
HG_DV = MIX_W // HG_HEADS
HG_CHUNK = 64

DA_HEADS = N_MIX_HEADS
DA_DV = MIX_W // DA_HEADS
DA_DC = DA_DV // 2

NSA_HEADS = N_MIX_HEADS
NSA_DK = HEAD_DIM
NSA_DV = MIX_W // NSA_HEADS
CMP_LEN = 32
CMP_STRIDE = 16
CMP_HIDDEN = 2 * NSA_DK
SLC_BLOCK = 64
SLC_TOPK = 16
N_LOCAL_BLOCKS = 2
SWA_WINDOW = 512
CMP_R = CMP_LEN // CMP_STRIDE
SLC_R = SLC_BLOCK // CMP_STRIDE
SLC_OVERLAP_W = tuple(float(min(w + 1, CMP_R, SLC_R + CMP_R - 1 - w)) for w in range(SLC_R + CMP_R - 1))

GM_GROUPS = N_MIX_HEADS
GM_DG = MIX_W // GM_GROUPS
GM_CHUNK = 128

XA_HEADS = 4
XA_DH = D_MODEL // XA_HEADS
D_FF = -(-(8 * D_MODEL) // (3 * 256)) * 256

Q_BLOCK = 128
NEG = -1e30
TINY = 1e-30
FORCE_SCORE = 1e4

IN_SIZES = (
    HG_HEADS * HG_DK, HG_HEADS * HG_DK, HG_HEADS * HG_DV, HG_HEADS * HG_DV,
    DA_HEADS * 2 * DA_DC, DA_HEADS * 2 * DA_DC, DA_HEADS * DA_DV,
    NSA_HEADS * NSA_DK, NSA_DK, NSA_DV, NSA_DK, NSA_DV, NSA_DK, NSA_DV,
    3 * NSA_HEADS,
    2 * MIX_W,
    N_BRANCH * D_MODEL,
)
IN_SPLITS = tuple(int(c) for c in np.cumsum(IN_SIZES)[:-1])
N_IN = int(sum(IN_SIZES))

kernel_name = 'hybrid_gated_parallel_mixer_trunk'


def rmsnorm(x, w, eps=1e-6):
    xf = x.astype(jnp.float32)
    y = xf * lax.rsqrt(jnp.mean(xf * xf, axis=-1, keepdims=True) + eps)
    return (y * w.astype(jnp.float32)).astype(x.dtype)


def layernorm(x, w, b, eps=1e-5):
    xf = x.astype(jnp.float32)
    mu = jnp.mean(xf, axis=-1, keepdims=True)
    var = jnp.mean(jnp.square(xf - mu), axis=-1, keepdims=True)
    y = (xf - mu) * lax.rsqrt(var + eps)
    return (y * w.astype(jnp.float32) + b.astype(jnp.float32)).astype(x.dtype)


def masked_softmax(s, mask):
    s = jnp.where(mask, s.astype(jnp.float32), NEG)
    m = jnp.max(s, axis=-1, keepdims=True)
    p = jnp.where(mask, jnp.exp(s - m), 0.0)
    return p / jnp.maximum(jnp.sum(p, axis=-1, keepdims=True), TINY)


def hgrn2_mixer(q, f_logit, i, g, lower_bound, norm_w):
    B, S, _ = q.shape
    H, DK, DV, C = HG_HEADS, HG_DK, HG_DV, HG_CHUNK
    NC = S // C
    qf = jax.nn.silu(q.astype(jnp.float32))
    f = lower_bound + (1.0 - lower_bound) * jax.nn.sigmoid(f_logit.astype(jnp.float32))
    k = 1.0 - f
    log_f = jnp.log(f)

    def to_chunks(t):
        return t.reshape(B, NC, C, H, t.shape[-1] // H).transpose(1, 0, 3, 2, 4)

    qc, kc, vc, lc = (to_chunks(t) for t in (qf, k, i.astype(jnp.float32), log_f))
    causal = jnp.tril(jnp.ones((C, C), dtype=bool))[:, :, None]

    def step(state, inp):
        qb, kb, vb, lb = inp
        b = jnp.cumsum(lb, axis=2)
        o_inter = jnp.einsum('bhtk,bhkv->bhtv', qb * jnp.exp(b), state)
        rel = jnp.where(causal, b[:, :, :, None, :] - b[:, :, None, :, :], -jnp.inf)
        scores = jnp.einsum('bhtk,bhsk,bhtsk->bhts', qb, kb, jnp.exp(rel))
        o_intra = jnp.einsum('bhts,bhsv->bhtv', scores, vb)
        b_end = b[:, :, -1:, :]
        new_state = (jnp.exp(b_end[:, :, 0, :])[..., None] * state
                     + jnp.einsum('bhsk,bhsv->bhkv', kb * jnp.exp(b_end - b), vb))
        return new_state, o_inter + o_intra

    state0 = jnp.zeros((B, H, DK, DV), jnp.float32)
    _, o = lax.scan(step, state0, (qc, kc, vc, lc))
    o = o.transpose(1, 0, 3, 2, 4).reshape(B, S, H, DV)
    o = rmsnorm(o, norm_w).reshape(B, S, H * DV) * jax.nn.silu(g.astype(jnp.float32))
    return o.astype(q.dtype)


def diff_attention_mixer(q, k, v, lq1, lk1, lq2, lk2, norm_w, lambda_init):
    B, S, _ = q.shape
    H, DC, DV, QB = DA_HEADS, DA_DC, DA_DV, Q_BLOCK
    nb = S // QB
    out_dtype = v.dtype
    qb_all = q.reshape(B, nb, QB, H, 2, DC).swapaxes(0, 1)
    k = k.reshape(B, S, H, 2, DC)
    v = v.reshape(B, S, H, DV).astype(jnp.float32)
    lam = (jnp.exp(jnp.sum(lq1.astype(jnp.float32) * lk1.astype(jnp.float32)))
           - jnp.exp(jnp.sum(lq2.astype(jnp.float32) * lk2.astype(jnp.float32))) + lambda_init)
    scale = DC ** -0.5
    kpos = jnp.arange(S)

    def block(args):
        idx, qb = args
        qpos = idx * QB + jnp.arange(QB)
        s = jnp.einsum('bqhcd,bkhcd->bhcqk', qb, k).astype(jnp.float32) * scale
        s = jnp.where(kpos[None, :] <= qpos[:, None], s, NEG)
        p = jax.nn.softmax(s, axis=-1)
        w = p[:, :, 0] - lam * p[:, :, 1]
        return jnp.einsum('bhqk,bkhv->bqhv', w, v)

    o = lax.map(block, (jnp.arange(nb), qb_all))
    o = o.swapaxes(0, 1).reshape(B, S, H, DV)
    o = rmsnorm(o, norm_w) * (1.0 - lambda_init)
    return o.reshape(B, S, H * DV).astype(out_dtype)


def compress_tokens(t, pos, w1, w2):
    B, S, d = t.shape
    n_chunks = S // CMP_STRIDE
    n_cmp = n_chunks - CMP_R + 1
    tr = t.reshape(B, n_chunks, CMP_STRIDE, d)
    blocks = jnp.concatenate([tr[:, m:m + n_cmp] for m in range(CMP_R)], axis=2) + pos
    flat = blocks.reshape(B, n_cmp, CMP_LEN * d)
    return jax.nn.silu(flat @ w1) @ w2


def nsa_mixer(q, k_cmp, v_cmp, k_slc, v_slc, k_swa, v_swa, gate_logits,
              pos_k, w1_k, w2_k, pos_v, w1_v, w2_v):
    B, S, _ = q.shape
    H, DK, DV, QB = NSA_HEADS, NSA_DK, NSA_DV, Q_BLOCK
    nb = S // QB
    out_dtype = q.dtype
    scale = DK ** -0.5
    gates = jax.nn.sigmoid(gate_logits.astype(jnp.float32)).reshape(B, S, H, 3)

    kc = compress_tokens(k_cmp, pos_k, w1_k, w2_k)
    vc = compress_tokens(v_cmp, pos_v, w1_v, w2_v)
    n_cmp = kc.shape[1]
    cmp_end = jnp.arange(n_cmp) * CMP_STRIDE + CMP_LEN - 1

    n_slc = S // SLC_BLOCK
    topk = min(SLC_TOPK, n_slc)
    ks_blocks = k_slc.reshape(B, n_slc, SLC_BLOCK, DK)
    vs_blocks = v_slc.reshape(B, n_slc, SLC_BLOCK, DV)
    blk = jnp.arange(n_slc)

    k_swa_pad = jnp.pad(k_swa, ((0, 0), (SWA_WINDOW, 0), (0, 0)))
    v_swa_pad = jnp.pad(v_swa, ((0, 0), (SWA_WINDOW, 0), (0, 0)))

    q_all = q.reshape(B, nb, QB, H, DK).swapaxes(0, 1)
    g_all = gates.reshape(B, nb, QB, H, 3).swapaxes(0, 1)

    def block(args):
        idx, qb, gb = args
        qpos = idx * QB + jnp.arange(QB)
        s = jnp.einsum('bqhd,bnd->bhqn', qb, kc) * scale
        p_cmp = masked_softmax(s, cmp_end[None, :] <= qpos[:, None])
        o_cmp = jnp.einsum('bhqn,bnd->bqhd', p_cmp, vc.astype(jnp.float32))
        imp = jnp.pad(jnp.sum(p_cmp, axis=1), ((0, 0), (0, 0), (CMP_R - 1, CMP_R - 1)))
        span = SLC_R * (n_slc - 1) + 1
        imp = sum(w * imp[..., o:o + span:SLC_R] for o, w in enumerate(SLC_OVERLAP_W))
        cur = (qpos // SLC_BLOCK)[:, None]
        valid = blk[None, :] <= cur
        forced = (blk[None, :] == 0) | (blk[None, :] > cur - N_LOCAL_BLOCKS)
        score = jnp.where(valid, jnp.where(forced, FORCE_SCORE, imp), -1.0)
        _, sel = lax.top_k(score, topk)
        ks_sel = jax.vmap(lambda kb, ib: kb[ib])(ks_blocks, sel)
        vs_sel = jax.vmap(lambda vb, ib: vb[ib])(vs_blocks, sel)
        tok_pos = sel[..., None] * SLC_BLOCK + jnp.arange(SLC_BLOCK)
        smask = (tok_pos <= qpos[None, :, None, None]).reshape(B, 1, QB, topk * SLC_BLOCK)
        s = jnp.einsum('bqhd,bqnjd->bhqnj', qb, ks_sel) * scale
        p = masked_softmax(s.reshape(B, H, QB, topk * SLC_BLOCK), smask)
        o_slc = jnp.einsum('bhqm,bqmd->bqhd', p,
                           vs_sel.reshape(B, QB, topk * SLC_BLOCK, DV).astype(jnp.float32))
        start = idx * QB
        kw = lax.dynamic_slice_in_dim(k_swa_pad, start, SWA_WINDOW + QB, axis=1)
        vw = lax.dynamic_slice_in_dim(v_swa_pad, start, SWA_WINDOW + QB, axis=1)
        kpos = start - SWA_WINDOW + jnp.arange(SWA_WINDOW + QB)
        dist = qpos[:, None] - kpos[None, :]
        wmask = (dist >= 0) & (dist < SWA_WINDOW) & (kpos[None, :] >= 0)
        s = jnp.einsum('bqhd,bkd->bhqk', qb, kw) * scale
        p = masked_softmax(s, wmask)
        o_swa = jnp.einsum('bhqk,bkd->bqhd', p, vw.astype(jnp.float32))
        return gb[..., 0:1] * o_cmp + gb[..., 1:2] * o_slc + gb[..., 2:3] * o_swa

    o = lax.map(block, (jnp.arange(nb), q_all, g_all))
    return o.swapaxes(0, 1).reshape(B, S, H * DV).astype(out_dtype)


def chunk_gmlp_mixer(z, ln_w, ln_b, w_s, b_s):
    B, S, _ = z.shape
    G, DG, C = GM_GROUPS, GM_DG, GM_CHUNK
    NC = S // C
    z = jax.nn.gelu(z)
    u, v = jnp.split(z, 2, axis=-1)
    v = layernorm(v, ln_w, ln_b).reshape(B, NC, C, G, DG)
    w_causal = w_s * jnp.tril(jnp.ones((C, C), dtype=w_s.dtype))
    vm = jnp.einsum('gts,bnsgd->bntgd', w_causal, v) + b_s.T[None, None, :, :, None]
    return u * vm.reshape(B, S, MIX_W)


def cross_attention(h, mem_n, w_q, w_kv, w_o):
    B, S, _ = h.shape
    M = mem_n.shape[1]
    q = (h @ w_q).reshape(B, S, XA_HEADS, XA_DH)
    k, v = jnp.split(mem_n @ w_kv, 2, axis=-1)
    k = k.reshape(B, M, XA_HEADS, XA_DH)
    v = v.reshape(B, M, XA_HEADS, XA_DH)
    s = jnp.einsum('bshd,bmhd->bhsm', q, k).astype(jnp.float32) * (XA_DH ** -0.5)
    p = jax.nn.softmax(s, axis=-1).astype(h.dtype)
    o = jnp.einsum('bhsm,bmhd->bshd', p, v).reshape(B, S, D_MODEL)
    return o @ w_o


def swiglu_ffn(h, w_in, w_out):
    gate, up = jnp.split(h @ w_in, 2, axis=-1)
    return (jax.nn.silu(gate) * up) @ w_out


def setup_inputs(seed: int = 0) -> dict:
    key = jax.random.key(seed)
    keys = jax.random.split(key, 48)
    ctr = [0]

    def nrm(shape, scale):
        k = keys[ctr[0]]
        ctr[0] += 1
        return jax.random.normal(k, shape, jnp.float32) * scale

    def gain(shape):
        return 1.0 + nrm(shape, 0.05)

    L, D = DEPTH, D_MODEL
    return {
        'x': nrm((BATCH, SEQ, D), 1.0),
        'mem': nrm((BATCH, N_MEM, D), 1.0),
        'mem_norm_w': gain((D,)),
        'mix_norm_w': gain((L, D)),
        'w_in': nrm((L, D, N_IN), D ** -0.5),
        'hg_lower_bounds': nrm((L, HG_HEADS * HG_DK), 0.1),
        'hg_norm_w': gain((L, HG_DV)),
        'da_lambda_q1': nrm((L, DA_DC), 0.1),
        'da_lambda_k1': nrm((L, DA_DC), 0.1),
        'da_lambda_q2': nrm((L, DA_DC), 0.1),
        'da_lambda_k2': nrm((L, DA_DC), 0.1),
        'da_norm_w': gain((L, DA_DV)),
        'nsa_pos_k': nrm((L, CMP_LEN, NSA_DK), 0.1),
        'nsa_cmp_w1_k': nrm((L, CMP_LEN * NSA_DK, CMP_HIDDEN), (CMP_LEN * NSA_DK) ** -0.5),
        'nsa_cmp_w2_k': nrm((L, CMP_HIDDEN, NSA_DK), CMP_HIDDEN ** -0.5),
        'nsa_pos_v': nrm((L, CMP_LEN, NSA_DV), 0.1),
        'nsa_cmp_w1_v': nrm((L, CMP_LEN * NSA_DV, CMP_HIDDEN), (CMP_LEN * NSA_DV) ** -0.5),
        'nsa_cmp_w2_v': nrm((L, CMP_HIDDEN, NSA_DV), CMP_HIDDEN ** -0.5),
        'gm_ln_w': gain((L, MIX_W)),
        'gm_ln_b': nrm((L, MIX_W), 0.02),
        'gm_w_s': nrm((L, GM_GROUPS, GM_CHUNK, GM_CHUNK), GM_CHUNK ** -0.5),
        'gm_b_s': 1.0 + nrm((L, GM_GROUPS, GM_CHUNK), 0.1),
        'w_branch': nrm((L, N_BRANCH, MIX_W, D), MIX_W ** -0.5),
        'w_out': nrm((L, D, D), D ** -0.5),
        'xa_norm_w': gain((L, D)),
        'xa_w_q': nrm((L, D, D), D ** -0.5),
        'xa_w_kv': nrm((L, D, 2 * D), D ** -0.5),
        'xa_w_o': nrm((L, D, D), D ** -0.5),
        'ffn_norm_w': gain((L, D)),
        'ffn_w_in': nrm((L, D, 2 * D_FF), D ** -0.5),
        'ffn_w_out': nrm((L, D_FF, D), D_FF ** -0.5),
        'final_norm_w': gain((D,)),
    }


def reference(x, mem, mem_norm_w, mix_norm_w, w_in, hg_lower_bounds, hg_norm_w,
              da_lambda_q1, da_lambda_k1, da_lambda_q2, da_lambda_k2, da_norm_w,
              nsa_pos_k, nsa_cmp_w1_k, nsa_cmp_w2_k, nsa_pos_v, nsa_cmp_w1_v, nsa_cmp_w2_v,
              gm_ln_w, gm_ln_b, gm_w_s, gm_b_s, w_branch, w_out,
              xa_norm_w, xa_w_q, xa_w_kv, xa_w_o, ffn_norm_w, ffn_w_in, ffn_w_out,
              final_norm_w):
    B, S, _ = x.shape
    mem_n = rmsnorm(mem, mem_norm_w)
    lb_soft = jax.nn.softmax(hg_lower_bounds.astype(jnp.float32), axis=0)
    lower_bounds = jnp.cumsum(lb_soft, axis=0) - lb_soft[0]
    for l in range(DEPTH):
        h = rmsnorm(x, mix_norm_w[l])
        (hq, hf, hi, hg, dq, dk, dv, nq, nkc, nvc, nks, nvs, nkw, nvw, ngate, gz,
         gate_logits) = jnp.split(h @ w_in[l], IN_SPLITS, axis=-1)
        lambda_init = 0.8 - 0.6 * math.exp(-0.3 * l)
        o_a = hgrn2_mixer(hq, hf, hi, hg, lower_bounds[l], hg_norm_w[l])
        o_b = diff_attention_mixer(dq, dk, dv, da_lambda_q1[l], da_lambda_k1[l],
                                   da_lambda_q2[l], da_lambda_k2[l], da_norm_w[l], lambda_init)
        o_c = nsa_mixer(nq, nkc, nvc, nks, nvs, nkw, nvw, ngate,
                        nsa_pos_k[l], nsa_cmp_w1_k[l], nsa_cmp_w2_k[l],
                        nsa_pos_v[l], nsa_cmp_w1_v[l], nsa_cmp_w2_v[l])
        o_d = chunk_gmlp_mixer(gz, gm_ln_w[l], gm_ln_b[l], gm_w_s[l], gm_b_s[l])
        gates = jax.nn.sigmoid(gate_logits).reshape(B, S, N_BRANCH, D_MODEL)
        merged = gates[:, :, 0] * (o_a @ w_branch[l, 0])
        merged = merged + gates[:, :, 1] * (o_b @ w_branch[l, 1])
        merged = merged + gates[:, :, 2] * (o_c @ w_branch[l, 2])
        merged = merged + gates[:, :, 3] * (o_d @ w_branch[l, 3])
        x = x + merged @ w_out[l]
        x = x + cross_attention(rmsnorm(x, xa_norm_w[l]), mem_n, xa_w_q[l], xa_w_kv[l], xa_w_o[l])
        x = x + swiglu_ffn(rmsnorm(x, ffn_norm_w[l]), ffn_w_in[l], ffn_w_out[l])
    return rmsnorm(x, final_norm_w)
```

```python
import functools
import math

import jax
import jax.numpy as jnp
from jax import lax
from jax.experimental import pallas as pl
from jax.experimental.pallas import tpu as pltpu

BF = jnp.bfloat16
F32 = jnp.float32

LANE = 128
HEAD_DIM = 128
N_HEADS = 4
MIX_W = N_HEADS * HEAD_DIM
N_BRANCH = 4
HG_CHUNK = 64
HG_SUB = 16
HG_EXP_CLAMP = 80.0
CMP_LEN = 32
CMP_STRIDE = 16
SLC_BLOCK = 64
SLC_TOPK = 16
N_LOCAL_BLOCKS = 2
SWA_WINDOW = 512
XA_HEADS = 4
NEG = -1e30
TINY = 1e-30
FORCE_SCORE = 1e4
VMEM_LIMIT = 56 * 1024 * 1024

BLK_HG = 0
BLK_DA = 16
BLK_NQ = 28
BLK_NKV = 32
BLK_NG = 38
BLK_GM = 40
BLK_MG = 48
N_PROJ = 112 * LANE


def _cparams(sem):
    return pltpu.CompilerParams(dimension_semantics=sem, vmem_limit_bytes=VMEM_LIMIT)


def _dot(a, b):
    return jnp.dot(a.astype(BF), b.astype(BF), preferred_element_type=F32)


def _dot_nt(a, b):
    return lax.dot_general(a.astype(BF), b.astype(BF), (((1,), (1,)), ((), ())),
                           preferred_element_type=F32)


def _dot_tn(a, b):
    return lax.dot_general(a.astype(BF), b.astype(BF), (((0,), (0,)), ((), ())),
                           preferred_element_type=F32)


def _split3(a):
    hi = a.astype(BF)
    r = a - hi.astype(F32)
    mid = r.astype(BF)
    lo = (r - mid.astype(F32)).astype(BF)
    return hi, mid, lo


def _dot_exact_rhs(a_bf, b):
    hi, mid, lo = _split3(b)
    f = lambda t: jnp.dot(a_bf, t, preferred_element_type=F32)
    return f(hi) + f(mid) + f(lo)


def _dot_exact_lhs(a, b_bf):
    hi, mid, lo = _split3(a)
    f = lambda t: jnp.dot(t, b_bf, preferred_element_type=F32)
    return f(hi) + f(mid) + f(lo)


def _rms(x, w, eps):
    return x * lax.rsqrt(jnp.mean(x * x, axis=-1, keepdims=True) + eps) * w


def _sigmoid(x):
    return 1.0 / (1.0 + jnp.exp(-x))


def _silu(x):
    return x * _sigmoid(x)


def _norm_matmul_kernel(x_ref, nw_ref, w_ref, o_ref, h_ref):
    @pl.when(pl.program_id(1) == 0)
    def _():
        h_ref[...] = _rms(x_ref[...], nw_ref[...], 1e-6).astype(BF)

    o_ref[...] = jnp.dot(h_ref[...], w_ref[...], preferred_element_type=F32).astype(o_ref.dtype)


def norm_matmul(x, norm_w, w, out_dtype, tm=512, tn=1024):
    T, D = x.shape
    N = w.shape[1]
    tm, tn = min(tm, T), min(tn, N)
    return pl.pallas_call(
        _norm_matmul_kernel,
        grid=(T // tm, N // tn),
        in_specs=[pl.BlockSpec((tm, D), lambda i, j: (i, 0)),
                  pl.BlockSpec((1, D), lambda i, j: (0, 0)),
                  pl.BlockSpec((D, tn), lambda i, j: (0, j))],
        out_specs=pl.BlockSpec((tm, tn), lambda i, j: (i, j)),
        out_shape=jax.ShapeDtypeStruct((T, N), out_dtype),
        scratch_shapes=[pltpu.VMEM((tm, D), BF)],
        compiler_params=_cparams(("parallel", "arbitrary")),
        name="norm_matmul",
    )(x, norm_w.reshape(1, D), w)


def _matmul_residual_kernel(a_ref, w_ref, r_ref, o_ref):
    o_ref[...] = r_ref[...] + jnp.dot(a_ref[...], w_ref[...], preferred_element_type=F32)


def matmul_residual(a, w, r, tm=512, tn=1024):
    T, K = a.shape
    N = w.shape[1]
    tm, tn = min(tm, T), min(tn, N)
    return pl.pallas_call(
        _matmul_residual_kernel,
        grid=(T // tm, N // tn),
        in_specs=[pl.BlockSpec((tm, K), lambda i, j: (i, 0)),
                  pl.BlockSpec((K, tn), lambda i, j: (0, j)),
                  pl.BlockSpec((tm, tn), lambda i, j: (i, j))],
        out_specs=pl.BlockSpec((tm, tn), lambda i, j: (i, j)),
        out_shape=jax.ShapeDtypeStruct((T, N), F32),
        compiler_params=_cparams(("parallel", "arbitrary")),
        name="matmul_residual",
    )(a, w, r)


def _rmsnorm_kernel(x_ref, w_ref, o_ref):
    o_ref[...] = _rms(x_ref[...], w_ref[...], 1e-6)


def rmsnorm(x, w, tm=512):
    T, D = x.shape
    tm = min(tm, T)
    return pl.pallas_call(
        _rmsnorm_kernel,
        grid=(T // tm,),
        in_specs=[pl.BlockSpec((tm, D), lambda i: (i, 0)),
                  pl.BlockSpec((1, D), lambda i: (0, 0))],
        out_specs=pl.BlockSpec((tm, D), lambda i: (i, 0)),
        out_shape=jax.ShapeDtypeStruct((T, D), F32),
        compiler_params=_cparams(("parallel",)),
        name="final_rmsnorm",
    )(x, w.reshape(1, D))


def _hgrn2_kernel(layer, q_ref, f_ref, i_ref, g_ref, lb_ref, nw_ref, o_ref, state_ref):
    @pl.when(pl.program_id(2) == 0)
    def _():
        state_ref[...] = jnp.zeros_like(state_ref)

    lb_all = lb_ref[...]
    e = jnp.exp(lb_all - jnp.max(lb_all, axis=0, keepdims=True))
    soft = e / jnp.sum(e, axis=0, keepdims=True)
    lrow = lax.broadcasted_iota(jnp.int32, soft.shape, 0)
    lower = jnp.sum(jnp.where((lrow >= 1) & (lrow <= layer), soft, 0.0), axis=0, keepdims=True)

    C, SUB = HG_CHUNK, HG_SUB
    tt = q_ref.shape[0]
    row = lax.broadcasted_iota(jnp.int32, (C, C), 0)
    col = lax.broadcasted_iota(jnp.int32, (C, C), 1)
    tri = jnp.where(col <= row, 1.0, 0.0).astype(BF)
    tri_sub = jnp.where(col < (row // SUB) * SUB, 1.0, 0.0).astype(BF)
    srow = lax.broadcasted_iota(jnp.int32, (SUB, C), 0)
    scol = lax.broadcasted_iota(jnp.int32, (SUB, C), 1)

    state = state_ref[...]
    for c in range(tt // C):
        rows = slice(c * C, (c + 1) * C)
        q = _silu(q_ref[rows, :])
        f = lower + (1.0 - lower) * _sigmoid(f_ref[rows, :])
        k = 1.0 - f
        v = i_ref[rows, :]
        log_f = jnp.log(f)
        b = _dot_exact_rhs(tri, log_f)
        b_sub = _dot_exact_rhs(tri_sub, log_f)
        b_end = jnp.sum(log_f, axis=0, keepdims=True)
        o_inter = _dot_nt(q * jnp.exp(b), state)
        qt_all = q * jnp.exp(b - b_sub)
        outs = []
        for s in range(C // SUB):
            r0, r1 = s * SUB, (s + 1) * SUB
            kt = k * jnp.exp(jnp.minimum(b_sub[r0:r0 + 1] - b, HG_EXP_CLAMP))
            sc = _dot_nt(qt_all[r0:r1], kt)
            sc = jnp.where(scol <= srow + r0, sc, 0.0)
            outs.append(_dot(sc, v))
        o = o_inter + jnp.concatenate(outs, axis=0)
        state = state * jnp.exp(b_end) + _dot_tn(v, k * jnp.exp(b_end - b))
        o = _rms(o, nw_ref[...], 1e-6) * _silu(g_ref[rows, :])
        o_ref[rows, :] = o.astype(o_ref.dtype)
    state_ref[...] = state


def hgrn2_mixer(proj, lower_bounds, norm_w, layer, tt=256):
    B, S, _ = proj.shape
    L = lower_bounds.shape[0]
    tt = min(tt, S)
    seg = lambda off: pl.BlockSpec((None, tt, LANE), lambda b, h, t: (b, t, BLK_HG + off + h))
    return pl.pallas_call(
        functools.partial(_hgrn2_kernel, layer),
        grid=(B, N_HEADS, S // tt),
        in_specs=[seg(0), seg(4), seg(8), seg(12),
                  pl.BlockSpec((L, LANE), lambda b, h, t: (0, h)),
                  pl.BlockSpec((1, LANE), lambda b, h, t: (0, 0))],
        out_specs=pl.BlockSpec((None, tt, LANE), lambda b, h, t: (b, t, h)),
        out_shape=jax.ShapeDtypeStruct((B, S, MIX_W), BF),
        scratch_shapes=[pltpu.VMEM((HEAD_DIM, HEAD_DIM), F32)],
        compiler_params=_cparams(("parallel", "parallel", "arbitrary")),
        name="hgrn2",
    )(proj, proj, proj, proj, lower_bounds, norm_w.reshape(1, LANE))


def _diff_attn_kernel(lambda_init, tq, tk, q_ref, k_ref, v_ref, lam_ref, nw_ref, o_ref,
                      qs_ref, m_ref, l_ref, acc_ref):
    i, j = pl.program_id(2), pl.program_id(3)
    dc = HEAD_DIM // 2

    @pl.when(j == 0)
    def _():
        q = q_ref[...] * (dc ** -0.5)
        first = lax.broadcasted_iota(jnp.int32, q.shape, 1) < dc
        qs_ref[0:tq, :] = jnp.where(first, q, 0.0).astype(BF)
        qs_ref[tq:2 * tq, :] = jnp.where(first, 0.0, q).astype(BF)
        m_ref[...] = jnp.full_like(m_ref, NEG)
        l_ref[...] = jnp.zeros_like(l_ref)
        acc_ref[...] = jnp.zeros_like(acc_ref)

    @pl.when(j * tk <= i * tq + tq - 1)
    def _():
        s = _dot_nt(qs_ref[...], k_ref[...])
        r = lax.broadcasted_iota(jnp.int32, (2 * tq, tk), 0)
        qpos = i * tq + jnp.where(r >= tq, r - tq, r)
        kpos = j * tk + lax.broadcasted_iota(jnp.int32, (2 * tq, tk), 1)
        s = jnp.where(kpos <= qpos, s, NEG)
        m_prev = m_ref[...]
        m_new = jnp.maximum(m_prev, jnp.max(s, axis=-1, keepdims=True))
        alpha = jnp.exp(m_prev - m_new)
        p = jnp.exp(s - m_new)
        l_ref[...] = alpha * l_ref[...] + jnp.sum(p, axis=-1, keepdims=True)
        acc_ref[...] = alpha * acc_ref[...] + _dot(p, v_ref[...])
        m_ref[...] = m_new

    @pl.when(j == pl.num_programs(3) - 1)
    def _():
        lam = (jnp.exp(jnp.sum(lam_ref[0:1] * lam_ref[1:2], axis=-1, keepdims=True))
               - jnp.exp(jnp.sum(lam_ref[2:3] * lam_ref[3:4], axis=-1, keepdims=True)) + lambda_init)
        o = acc_ref[...] / l_ref[...]
        o = o[0:tq] - lam * o[tq:2 * tq]
        o = _rms(o, nw_ref[...], 1e-6) * (1.0 - lambda_init)
        o_ref[...] = o.astype(o_ref.dtype)


def diff_attention_mixer(proj, lam_params, norm_w, lambda_init, tq=512, tk=512):
    B, S, _ = proj.shape
    tq, tk = min(tq, S), min(tk, S)
    last = lambda i: (i * tq + tq - 1) // tk
    return pl.pallas_call(
        functools.partial(_diff_attn_kernel, lambda_init, tq, tk),
        grid=(B, N_HEADS, S // tq, S // tk),
        in_specs=[pl.BlockSpec((None, tq, LANE), lambda b, h, i, j: (b, i, BLK_DA + h)),
                  pl.BlockSpec((None, tk, LANE),
                               lambda b, h, i, j: (b, jnp.minimum(j, last(i)), BLK_DA + 4 + h)),
                  pl.BlockSpec((None, tk, LANE),
                               lambda b, h, i, j: (b, jnp.minimum(j, last(i)), BLK_DA + 8 + h)),
                  pl.BlockSpec((4, HEAD_DIM // 2), lambda b, h, i, j: (0, 0)),
                  pl.BlockSpec((1, LANE), lambda b, h, i, j: (0, 0))],
        out_specs=pl.BlockSpec((None, tq, LANE), lambda b, h, i, j: (b, i, h)),
        out_shape=jax.ShapeDtypeStruct((B, S, MIX_W), BF),
        scratch_shapes=[pltpu.VMEM((2 * tq, LANE), BF),
                        pltpu.VMEM((2 * tq, 1), F32),
                        pltpu.VMEM((2 * tq, 1), F32),
                        pltpu.VMEM((2 * tq, LANE), F32)],
        compiler_params=_cparams(("parallel", "parallel", "parallel", "arbitrary")),
        name="diff_attention",
    )(proj, proj, proj, lam_params, norm_w.reshape(1, LANE))


def _compress_kernel(t_ref, pos_ref, w1_ref, w2_ref, o_ref):
    half = t_ref.shape[1]
    t = t_ref[...]
    a = _dot(t, w1_ref[0:half, :])
    bnext = _dot(t, w1_ref[half:2 * half, :])
    c0 = _dot(pos_ref[...], w1_ref[...])
    n = a.shape[0]
    hidden = a + pltpu.roll(bnext, n - 1, 0) + c0
    o_ref[...] = _dot(_silu(hidden), w2_ref[...])


def compress_tokens(t, pos, w1, w2):
    B, S, d = t.shape
    n_chunks = S // CMP_STRIDE
    hid = w1.shape[1]
    t = t.reshape(B, n_chunks, CMP_STRIDE * d)
    return pl.pallas_call(
        _compress_kernel,
        grid=(B,),
        in_specs=[pl.BlockSpec((None, n_chunks, CMP_STRIDE * d), lambda b: (b, 0, 0)),
                  pl.BlockSpec((1, CMP_LEN * d), lambda b: (0, 0)),
                  pl.BlockSpec((CMP_LEN * d, hid), lambda b: (0, 0)),
                  pl.BlockSpec((hid, d), lambda b: (0, 0))],
        out_specs=pl.BlockSpec((None, n_chunks, d), lambda b: (b, 0, 0)),
        out_shape=jax.ShapeDtypeStruct((B, n_chunks, d), F32),
        compiler_params=_cparams(("parallel",)),
        name="nsa_compress",
    )(t, pos.reshape(1, CMP_LEN * d), w1, w2)


def _cmp_select_kernel(tq, q_ref, kc_ref, vc_ref, pool_ref, o_ref, sel_ref):
    i = pl.program_id(1)
    ncp = kc_ref.shape[0]
    n_slc = sel_ref.shape[1]
    scale = HEAD_DIM ** -0.5
    qpos = i * tq + lax.broadcasted_iota(jnp.int32, (tq, 1), 0)
    cmp_end = lax.broadcasted_iota(jnp.int32, (1, ncp), 1) * CMP_STRIDE + (CMP_LEN - 1)
    mask = cmp_end <= qpos
    kc = kc_ref[...].astype(BF)
    vc = vc_ref[...].astype(BF)
    imp = jnp.zeros((tq, ncp), F32)
    for h in range(N_HEADS):
        cols = slice(h * HEAD_DIM, (h + 1) * HEAD_DIM)
        s = _dot_nt(q_ref[:, cols], kc) * scale
        s = jnp.where(mask, s, NEG)
        m = jnp.max(s, axis=-1, keepdims=True)
        p = jnp.where(mask, jnp.exp(s - m), 0.0)
        p = p / jnp.maximum(jnp.sum(p, axis=-1, keepdims=True), TINY)
        o_ref[:, cols] = _dot(p, vc)
        imp = imp + p
    imp_slc = _dot_exact_lhs(imp, pool_ref[...])
    blk = lax.broadcasted_iota(jnp.int32, (1, n_slc), 1)
    cur = qpos // SLC_BLOCK
    valid = blk <= cur
    forced = (blk == 0) | (blk > cur - N_LOCAL_BLOCKS)
    score = jnp.where(valid, jnp.where(forced, FORCE_SCORE, imp_slc), -1.0)
    sel = jnp.zeros((tq, n_slc), F32)
    blk_f = blk.astype(F32)
    for _ in range(min(SLC_TOPK, n_slc)):
        m = jnp.max(score, axis=-1, keepdims=True)
        idx = jnp.min(jnp.where(score == m, blk_f, float(n_slc)), axis=-1, keepdims=True)
        pick = blk_f == idx
        sel = jnp.where(pick, 1.0, sel)
        score = jnp.where(pick, -2.0, score)
    sel_ref[...] = sel.astype(sel_ref.dtype)


def nsa_compressed_and_select(proj, kc, vc, tq=128):
    B, S, _ = proj.shape
    ncp = kc.shape[1]
    n_slc = S // SLC_BLOCK
    n = jnp.arange(ncp)[:, None]
    jj = jnp.arange(n_slc)[None, :]
    w = n - (SLC_BLOCK // CMP_STRIDE) * jj + 1
    pool = jnp.where((w >= 0) & (w <= 4), jnp.where((w == 0) | (w == 4), 1.0, 2.0), 0.0).astype(BF)
    return pl.pallas_call(
        functools.partial(_cmp_select_kernel, tq),
        grid=(B, S // tq),
        in_specs=[pl.BlockSpec((None, tq, MIX_W), lambda b, i: (b, i, BLK_NQ // 4)),
                  pl.BlockSpec((None, ncp, HEAD_DIM), lambda b, i: (b, 0, 0)),
                  pl.BlockSpec((None, ncp, HEAD_DIM), lambda b, i: (b, 0, 0)),
                  pl.BlockSpec((ncp, n_slc), lambda b, i: (0, 0))],
        out_specs=[pl.BlockSpec((None, tq, MIX_W), lambda b, i: (b, i, 0)),
                   pl.BlockSpec((None, tq, n_slc), lambda b, i: (b, i, 0))],
        out_shape=[jax.ShapeDtypeStruct((B, S, MIX_W), F32),
                   jax.ShapeDtypeStruct((B, S, n_slc), BF)],
        compiler_params=_cparams(("parallel", "parallel")),
        name="nsa_cmp_select",
    )(proj, kc, vc, pool)


def _slc_attn_kernel(tq, tk, q_ref, k_ref, v_ref, sel_ref, exp_ref, o_ref, qs_ref, m_ref, l_ref, acc_ref):
    i, j = pl.program_id(1), pl.program_id(2)

    @pl.when(j == 0)
    def _():
        qs_ref[...] = (q_ref[...] * (HEAD_DIM ** -0.5)).astype(BF)
        m_ref[...] = jnp.full_like(m_ref, NEG)
        l_ref[...] = jnp.zeros_like(l_ref)
        acc_ref[...] = jnp.zeros_like(acc_ref)

    @pl.when(j * tk <= i * tq + tq - 1)
    def _():
        picked = jnp.dot(sel_ref[...], exp_ref[...], preferred_element_type=F32)
        qpos = i * tq + lax.broadcasted_iota(jnp.int32, (tq, tk), 0)
        kpos = j * tk + lax.broadcasted_iota(jnp.int32, (tq, tk), 1)
        mask = jnp.where(kpos <= qpos, picked, 0.0) > 0.5
        k = k_ref[...].astype(BF)
        v = v_ref[...].astype(BF)
        for h in range(N_HEADS):
            cols = slice(h * HEAD_DIM, (h + 1) * HEAD_DIM)
            s = jnp.where(mask, _dot_nt(qs_ref[:, cols], k), NEG)
            m_prev = m_ref[h]
            m_new = jnp.maximum(m_prev, jnp.max(s, axis=-1, keepdims=True))
            alpha = jnp.exp(m_prev - m_new)
            p = jnp.where(mask, jnp.exp(s - m_new), 0.0)
            l_ref[h] = alpha * l_ref[h] + jnp.sum(p, axis=-1, keepdims=True)
            acc_ref[:, cols] = alpha * acc_ref[:, cols] + _dot(p, v)
            m_ref[h] = m_new

    @pl.when(j == pl.num_programs(2) - 1)
    def _():
        for h in range(N_HEADS):
            cols = slice(h * HEAD_DIM, (h + 1) * HEAD_DIM)
            o_ref[:, cols] = acc_ref[:, cols] / jnp.maximum(l_ref[h], TINY)


def nsa_selected_attention(proj, sel, tq=256, tk=512):
    B, S, _ = proj.shape
    tq, tk = min(tq, S), min(tk, S)
    n_slc = S // SLC_BLOCK
    expand = (jnp.arange(n_slc)[:, None] == (jnp.arange(S)[None, :] // SLC_BLOCK)).astype(BF)
    last = lambda i: (i * tq + tq - 1) // tk
    kv = lambda off: pl.BlockSpec((None, tk, LANE),
                                  lambda b, i, j: (b, jnp.minimum(j, last(i)), BLK_NKV + off))
    return pl.pallas_call(
        functools.partial(_slc_attn_kernel, tq, tk),
        grid=(B, S // tq, S // tk),
        in_specs=[pl.BlockSpec((None, tq, MIX_W), lambda b, i, j: (b, i, BLK_NQ // 4)),
                  kv(2), kv(3),
                  pl.BlockSpec((None, tq, n_slc), lambda b, i, j: (b, i, 0)),
                  pl.BlockSpec((n_slc, tk), lambda b, i, j: (0, jnp.minimum(j, last(i))))],
        out_specs=pl.BlockSpec((None, tq, MIX_W), lambda b, i, j: (b, i, 0)),
        out_shape=jax.ShapeDtypeStruct((B, S, MIX_W), F32),
        scratch_shapes=[pltpu.VMEM((tq, MIX_W), BF),
                        pltpu.VMEM((N_HEADS, tq, 1), F32),
                        pltpu.VMEM((N_HEADS, tq, 1), F32),
                        pltpu.VMEM((tq, MIX_W), F32)],
        compiler_params=_cparams(("parallel", "parallel", "arbitrary")),
        name="nsa_selected",
    )(proj, proj, proj, sel, expand)


def _swa_combine_kernel(tq, q_ref, kp_ref, kc_ref, vp_ref, vc_ref, g_ref, ocmp_ref, oslc_ref, o_ref):
    i = pl.program_id(1)
    scale = HEAD_DIM ** -0.5
    k = jnp.concatenate([kp_ref[...], kc_ref[...]], axis=0).astype(BF)
    v = jnp.concatenate([vp_ref[...], vc_ref[...]], axis=0).astype(BF)
    nk = k.shape[0]
    qpos = i * tq + lax.broadcasted_iota(jnp.int32, (tq, nk), 0)
    kpos = i * tq - SWA_WINDOW + lax.broadcasted_iota(jnp.int32, (tq, nk), 1)
    dist = qpos - kpos
    mask = (dist >= 0) & (dist < SWA_WINDOW) & (kpos >= 0)
    gates = _sigmoid(g_ref[...])
    for h in range(N_HEADS):
        cols = slice(h * HEAD_DIM, (h + 1) * HEAD_DIM)
        s = jnp.where(mask, _dot_nt(q_ref[:, cols] * scale, k), NEG)
        m = jnp.max(s, axis=-1, keepdims=True)
        p = jnp.where(mask, jnp.exp(s - m), 0.0)
        p = p / jnp.maximum(jnp.sum(p, axis=-1, keepdims=True), TINY)
        o_swa = _dot(p, v)
        o = (gates[:, 3 * h:3 * h + 1] * ocmp_ref[:, cols]
             + gates[:, 3 * h + 1:3 * h + 2] * oslc_ref[:, cols]
             + gates[:, 3 * h + 2:3 * h + 3] * o_swa)
        o_ref[:, cols] = o.astype(o_ref.dtype)


def nsa_window_and_combine(proj, o_cmp, o_slc):
    B, S, _ = proj.shape
    tq = SWA_WINDOW
    kvp = lambda off: pl.BlockSpec((None, tq, LANE), lambda b, i: (b, jnp.maximum(i - 1, 0), BLK_NKV + off))
    kvc = lambda off: pl.BlockSpec((None, tq, LANE), lambda b, i: (b, i, BLK_NKV + off))
    full = lambda: pl.BlockSpec((None, tq, MIX_W), lambda b, i: (b, i, 0))
    return pl.pallas_call(
        functools.partial(_swa_combine_kernel, tq),
        grid=(B, S // tq),
        in_specs=[pl.BlockSpec((None, tq, MIX_W), lambda b, i: (b, i, BLK_NQ // 4)),
                  kvp(4), kvc(4), kvp(5), kvc(5),
                  pl.BlockSpec((None, tq, LANE), lambda b, i: (b, i, BLK_NG)),
                  full(), full()],
        out_specs=full(),
        out_shape=jax.ShapeDtypeStruct((B, S, MIX_W), BF),
        compiler_params=_cparams(("parallel", "parallel")),
        name="nsa_window_combine",
    )(proj, proj, proj, proj, proj, proj, o_cmp, o_slc)


def _gelu_tanh(x):
    return 0.5 * x * (1.0 + jnp.tanh(math.sqrt(2.0 / math.pi) * (x + 0.044715 * (x * x * x))))


def _gmlp_kernel(u_ref, v_ref, lnw_ref, lnb_ref, ws_ref, bs_ref, o_ref):
    C = ws_ref.shape[1]
    row = lax.broadcasted_iota(jnp.int32, (C, C), 0)
    col = lax.broadcasted_iota(jnp.int32, (C, C), 1)
    causal = col <= row
    for c in range(u_ref.shape[0] // C):
        rows = slice(c * C, (c + 1) * C)
        u = _gelu_tanh(u_ref[rows, :])
        v = _gelu_tanh(v_ref[rows, :])
        mu = jnp.mean(v, axis=-1, keepdims=True)
        var = jnp.mean(jnp.square(v - mu), axis=-1, keepdims=True)
        v = (v - mu) * lax.rsqrt(var + 1e-5) * lnw_ref[...] + lnb_ref[...]
        for g in range(N_HEADS):
            cols = slice(g * HEAD_DIM, (g + 1) * HEAD_DIM)
            w = jnp.where(causal, ws_ref[g], 0.0)
            vm = _dot(w, v[:, cols]) + bs_ref[:, cols]
            o_ref[rows, cols] = (u[:, cols] * vm).astype(o_ref.dtype)


def chunk_gmlp_mixer(proj, ln_w, ln_b, w_s, b_s, tt=512):
    B, S, _ = proj.shape
    G, C, _ = w_s.shape
    tt = min(tt, S)
    bias = jnp.repeat(b_s.T, HEAD_DIM, axis=1)
    return pl.pallas_call(
        _gmlp_kernel,
        grid=(B, S // tt),
        in_specs=[pl.BlockSpec((None, tt, MIX_W), lambda b, t: (b, t, BLK_GM // 4)),
                  pl.BlockSpec((None, tt, MIX_W), lambda b, t: (b, t, BLK_GM // 4 + 1)),
                  pl.BlockSpec((1, MIX_W), lambda b, t: (0, 0)),
                  pl.BlockSpec((1, MIX_W), lambda b, t: (0, 0)),
                  pl.BlockSpec((G, C, C), lambda b, t: (0, 0, 0)),
                  pl.BlockSpec((C, MIX_W), lambda b, t: (0, 0))],
        out_specs=pl.BlockSpec((None, tt, MIX_W), lambda b, t: (b, t, 0)),
        out_shape=jax.ShapeDtypeStruct((B, S, MIX_W), BF),
        compiler_params=_cparams(("parallel", "parallel")),
        name="chunk_gmlp",
    )(proj, proj, ln_w.reshape(1, MIX_W), ln_b.reshape(1, MIX_W), w_s, bias)


def _merge_kernel(oa_ref, ob_ref, oc_ref, od_ref, ga_ref, gb_ref, gc_ref, gd_ref, wb_ref, wo_ref, x_ref,
                  o_ref, acc_ref):
    j = pl.program_id(1)

    @pl.when(j == 0)
    def _():
        acc_ref[...] = jnp.zeros_like(acc_ref)

    merged = None
    for b, (o_b, g_b) in enumerate(((oa_ref, ga_ref), (ob_ref, gb_ref), (oc_ref, gc_ref), (od_ref, gd_ref))):
        term = _sigmoid(g_b[...]) * jnp.dot(o_b[...], wb_ref[b], preferred_element_type=F32)
        merged = term if merged is None else merged + term
    acc_ref[...] += jnp.dot(merged.astype(BF), wo_ref[...], preferred_element_type=F32)

    @pl.when(j == pl.num_programs(1) - 1)
    def _():
        o_ref[...] = x_ref[...] + acc_ref[...]


def merge_project(x, proj, branches, w_branch, w_out, tm=512, tn=512):
    T, D = x.shape
    tm = min(tm, T)
    per = D // tn
    o_spec = pl.BlockSpec((tm, MIX_W), lambda i, j: (i, 0))
    g_spec = lambda b: pl.BlockSpec((tm, tn), lambda i, j: (i, (BLK_MG * LANE) // tn + b * per + j))
    return pl.pallas_call(
        _merge_kernel,
        grid=(T // tm, per),
        in_specs=[o_spec, o_spec, o_spec, o_spec, g_spec(0), g_spec(1), g_spec(2), g_spec(3),
                  pl.BlockSpec((N_BRANCH, MIX_W, tn), lambda i, j: (0, 0, j)),
                  pl.BlockSpec((tn, D), lambda i, j: (j, 0)),
                  pl.BlockSpec((tm, D), lambda i, j: (i, 0))],
        out_specs=pl.BlockSpec((tm, D), lambda i, j: (i, 0)),
        out_shape=jax.ShapeDtypeStruct((T, D), F32),
        scratch_shapes=[pltpu.VMEM((tm, D), F32)],
        compiler_params=_cparams(("parallel", "arbitrary")),
        name="merge_project",
    )(*branches, proj, proj, proj, proj, w_branch, w_out, x)


def _cross_attn_kernel(q_ref, k_ref, v_ref, o_ref):
    dh = q_ref.shape[1] // XA_HEADS
    for h in range(XA_HEADS):
        cols = slice(h * dh, (h + 1) * dh)
        s = _dot_nt(q_ref[:, cols], k_ref[:, cols]) * (dh ** -0.5)
        m = jnp.max(s, axis=-1, keepdims=True)
        p = jnp.exp(s - m)
        p = p / jnp.sum(p, axis=-1, keepdims=True)
        o_ref[:, cols] = _dot(p, v_ref[:, cols]).astype(o_ref.dtype)


def cross_attention_core(q, kv, tq=512):
    B, S, D = q.shape
    M = kv.shape[1]
    tq = min(tq, S)
    return pl.pallas_call(
        _cross_attn_kernel,
        grid=(B, S // tq),
        in_specs=[pl.BlockSpec((None, tq, D), lambda b, i: (b, i, 0)),
                  pl.BlockSpec((None, M, D), lambda b, i: (b, 0, 0)),
                  pl.BlockSpec((None, M, D), lambda b, i: (b, 0, 1))],
        out_specs=pl.BlockSpec((None, tq, D), lambda b, i: (b, i, 0)),
        out_shape=jax.ShapeDtypeStruct((B, S, D), BF),
        compiler_params=_cparams(("parallel", "parallel")),
        name="cross_attention",
    )(q, kv, kv)


def _ffn_kernel(x_ref, nw_ref, wg_ref, wu_ref, wo_ref, o_ref, h_ref, acc_ref):
    j = pl.program_id(1)

    @pl.when(j == 0)
    def _():
        h_ref[...] = _rms(x_ref[...], nw_ref[...], 1e-6).astype(BF)
        acc_ref[...] = jnp.zeros_like(acc_ref)

    h = h_ref[...]
    gate = jnp.dot(h, wg_ref[...], preferred_element_type=F32)
    up = jnp.dot(h, wu_ref[...], preferred_element_type=F32)
    acc_ref[...] += jnp.dot((_silu(gate) * up).astype(BF), wo_ref[...], preferred_element_type=F32)

    @pl.when(j == pl.num_programs(1) - 1)
    def _():
        o_ref[...] = x_ref[...] + acc_ref[...]


def swiglu_ffn(x, norm_w, w_in, w_out, tm=512, tf=512):
    T, D = x.shape
    F = w_out.shape[0]
    tm = min(tm, T)
    nf = F // tf
    return pl.pallas_call(
        _ffn_kernel,
        grid=(T // tm, nf),
        in_specs=[pl.BlockSpec((tm, D), lambda i, j: (i, 0)),
                  pl.BlockSpec((1, D), lambda i, j: (0, 0)),
                  pl.BlockSpec((D, tf), lambda i, j: (0, j)),
                  pl.BlockSpec((D, tf), lambda i, j: (0, nf + j)),
                  pl.BlockSpec((tf, D), lambda i, j: (j, 0))],
        out_specs=pl.BlockSpec((tm, D), lambda i, j: (i, 0)),
        out_shape=jax.ShapeDtypeStruct((T, D), F32),
        scratch_shapes=[pltpu.VMEM((tm, D), BF), pltpu.VMEM((tm, D), F32)],
        compiler_params=_cparams(("parallel", "arbitrary")),
        name="swiglu_ffn",
    )(x, norm_w.reshape(1, D), w_in, w_in, w_out)


def _pack_w_in(w):
    d = w.shape[0]
    n_gate = 3 * N_HEADS
    gate_off = BLK_NG * LANE
    z = lambda n: jnp.zeros((d, n), w.dtype)
    rest = w[:, gate_off + n_gate:]
    packed = jnp.concatenate([w[:, :gate_off + n_gate], z(BLK_GM * LANE - gate_off - n_gate), rest], axis=1)
    assert packed.shape[1] == N_PROJ, packed.shape
    return packed.astype(BF)


def kernel(x, mem, mem_norm_w, mix_norm_w, w_in, hg_lower_bounds, hg_norm_w, da_lambda_q1, da_lambda_k1, da_lambda_q2, da_lambda_k2, da_norm_w, nsa_pos_k, nsa_cmp_w1_k, nsa_cmp_w2_k, nsa_pos_v, nsa_cmp_w1_v, nsa_cmp_w2_v, gm_ln_w, gm_ln_b, gm_w_s, gm_b_s, w_branch, w_out, xa_norm_w, xa_w_q, xa_w_kv, xa_w_o, ffn_norm_w, ffn_w_in, ffn_w_out, final_norm_w):
    B, S, D = x.shape
    M = mem.shape[1]
    depth = w_in.shape[0]
    assert D == BLK_MG * LANE // 3 and S % SWA_WINDOW == 0, (S, D)
    T = B * S
    x = x.reshape(T, D)
    mem2 = mem.reshape(B * M, D)
    for l in range(depth):
        lambda_init = 0.8 - 0.6 * math.exp(-0.3 * l)
        proj = norm_matmul(x, mix_norm_w[l], _pack_w_in(w_in[l]), F32).reshape(B, S, N_PROJ)
        o_a = hgrn2_mixer(proj, hg_lower_bounds, hg_norm_w[l], l)
        lam_params = jnp.stack([da_lambda_q1[l], da_lambda_k1[l], da_lambda_q2[l], da_lambda_k2[l]])
        o_b = diff_attention_mixer(proj, lam_params, da_norm_w[l], lambda_init)
        nkv = lambda off: proj[:, :, (BLK_NKV + off) * LANE:(BLK_NKV + off + 1) * LANE]
        kc = compress_tokens(nkv(0), nsa_pos_k[l], nsa_cmp_w1_k[l].astype(BF), nsa_cmp_w2_k[l].astype(BF))
        vc = compress_tokens(nkv(1), nsa_pos_v[l], nsa_cmp_w1_v[l].astype(BF), nsa_cmp_w2_v[l].astype(BF))
        o_cmp, sel = nsa_compressed_and_select(proj, kc, vc)
        o_slc = nsa_selected_attention(proj, sel)
        o_c = nsa_window_and_combine(proj, o_cmp, o_slc)
        o_d = chunk_gmlp_mixer(proj, gm_ln_w[l], gm_ln_b[l], gm_w_s[l], gm_b_s[l])
        branches = [o.reshape(T, MIX_W) for o in (o_a, o_b, o_c, o_d)]
        x = merge_project(x, proj.reshape(T, N_PROJ), branches, w_branch[l].astype(BF), w_out[l].astype(BF))
        q = norm_matmul(x, xa_norm_w[l], xa_w_q[l].astype(BF), BF).reshape(B, S, D)
        kv = norm_matmul(mem2, mem_norm_w, xa_w_kv[l].astype(BF), F32).reshape(B, M, 2 * D)
        att = cross_attention_core(q, kv).reshape(T, D)
        x = matmul_residual(att, xa_w_o[l].astype(BF), x)
        x = swiglu_ffn(x, ffn_norm_w[l], ffn_w_in[l].astype(BF), ffn_w_out[l].astype(BF))
    return rmsnorm(x, final_norm_w).reshape(B, S, D)
```

```python
import functools
import math

import jax
import jax.numpy as jnp
from jax import lax
from jax.experimental import pallas as pl
from jax.experimental.pallas import tpu as pltpu

BF = jnp.bfloat16
F32 = jnp.float32

LANE = 128
HEAD_DIM = 128
N_HEADS = 4
MIX_W = N_HEADS * HEAD_DIM
N_BRANCH = 4
HG_CHUNK = 64
HG_SUB = 16
HG_EXP_CLAMP = 80.0
CMP_LEN = 32
CMP_STRIDE = 16
SLC_BLOCK = 64
SLC_TOPK = 16
N_LOCAL_BLOCKS = 2
SWA_WINDOW = 512
XA_HEADS = 4
NEG = -1e30
TINY = 1e-30
FORCE_SCORE = 1e4
LOG2E = math.log2(math.e)
FLASH_MM_ROWS = 256
VMEM_LIMIT = 56 * 1024 * 1024

BLK_HG = 0
BLK_DA = 16
BLK_NQ = 28
BLK_NKV = 32
BLK_NG = 38
BLK_GM = 40
BLK_MG = 48
N_PROJ = 112 * LANE


def _cparams(sem):
    return pltpu.CompilerParams(dimension_semantics=sem, vmem_limit_bytes=VMEM_LIMIT)


def _dot(a, b):
    return jnp.dot(a.astype(BF), b.astype(BF), preferred_element_type=F32)


def _dot_nt(a, b):
    return lax.dot_general(a.astype(BF), b.astype(BF), (((1,), (1,)), ((), ())),
                           preferred_element_type=F32)


def _dot_tn(a, b):
    return lax.dot_general(a.astype(BF), b.astype(BF), (((0,), (0,)), ((), ())),
                           preferred_element_type=F32)


def _split3(a):
    hi = a.astype(BF)
    r = a - hi.astype(F32)
    mid = r.astype(BF)
    lo = (r - mid.astype(F32)).astype(BF)
    return hi, mid, lo


def _dot_exact_rhs(a_bf, b):
    hi, mid, lo = _split3(b)
    f = lambda t: jnp.dot(a_bf, t, preferred_element_type=F32)
    return f(hi) + f(mid) + f(lo)


def _dot_exact_lhs(a, b_bf):
    hi, mid, lo = _split3(a)
    f = lambda t: jnp.dot(t, b_bf, preferred_element_type=F32)
    return f(hi) + f(mid) + f(lo)


def _rms(x, w, eps):
    return x * lax.rsqrt(jnp.mean(x * x, axis=-1, keepdims=True) + eps) * w


def _sigmoid(x):
    return 1.0 / (1.0 + jnp.exp(-x))


def _silu(x):
    return x * _sigmoid(x)


def _norm_matmul_kernel(x_ref, nw_ref, w_ref, o_ref, h_ref):
    @pl.when(pl.program_id(1) == 0)
    def _():
        h_ref[...] = _rms(x_ref[...], nw_ref[...], 1e-6).astype(BF)

    o_ref[...] = jnp.dot(h_ref[...], w_ref[...], preferred_element_type=F32).astype(o_ref.dtype)


def norm_matmul(x, norm_w, w, out_dtype, tm=512, tn=1024):
    T, D = x.shape
    N = w.shape[1]
    tm, tn = min(tm, T), min(tn, N)
    return pl.pallas_call(
        _norm_matmul_kernel,
        grid=(T // tm, N // tn),
        in_specs=[pl.BlockSpec((tm, D), lambda i, j: (i, 0)),
                  pl.BlockSpec((1, D), lambda i, j: (0, 0)),
                  pl.BlockSpec((D, tn), lambda i, j: (0, j))],
        out_specs=pl.BlockSpec((tm, tn), lambda i, j: (i, j)),
        out_shape=jax.ShapeDtypeStruct((T, N), out_dtype),
        scratch_shapes=[pltpu.VMEM((tm, D), BF)],
        compiler_params=_cparams(("parallel", "arbitrary")),
        name="norm_matmul",
    )(x, norm_w.reshape(1, D), w)


def _matmul_residual_kernel(a_ref, w_ref, r_ref, o_ref):
    o_ref[...] = r_ref[...] + jnp.dot(a_ref[...], w_ref[...], preferred_element_type=F32)


def matmul_residual(a, w, r, tm=512, tn=1024):
    T, K = a.shape
    N = w.shape[1]
    tm, tn = min(tm, T), min(tn, N)
    return pl.pallas_call(
        _matmul_residual_kernel,
        grid=(T // tm, N // tn),
        in_specs=[pl.BlockSpec((tm, K), lambda i, j: (i, 0)),
                  pl.BlockSpec((K, tn), lambda i, j: (0, j)),
                  pl.BlockSpec((tm, tn), lambda i, j: (i, j))],
        out_specs=pl.BlockSpec((tm, tn), lambda i, j: (i, j)),
        out_shape=jax.ShapeDtypeStruct((T, N), F32),
        compiler_params=_cparams(("parallel", "arbitrary")),
        name="matmul_residual",
    )(a, w, r)


def _rmsnorm_kernel(x_ref, w_ref, o_ref):
    o_ref[...] = _rms(x_ref[...], w_ref[...], 1e-6)


def rmsnorm(x, w, tm=512):
    T, D = x.shape
    tm = min(tm, T)
    return pl.pallas_call(
        _rmsnorm_kernel,
        grid=(T // tm,),
        in_specs=[pl.BlockSpec((tm, D), lambda i: (i, 0)),
                  pl.BlockSpec((1, D), lambda i: (0, 0))],
        out_specs=pl.BlockSpec((tm, D), lambda i: (i, 0)),
        out_shape=jax.ShapeDtypeStruct((T, D), F32),
        compiler_params=_cparams(("parallel",)),
        name="final_rmsnorm",
    )(x, w.reshape(1, D))


def _hgrn2_kernel(layer, q_ref, f_ref, i_ref, g_ref, lb_ref, nw_ref, o_ref, state_ref):
    @pl.when(pl.program_id(1) == 0)
    def _():
        state_ref[...] = jnp.zeros_like(state_ref)

    lb_all = lb_ref[...]
    e = jnp.exp(lb_all - jnp.max(lb_all, axis=0, keepdims=True))
    soft = e / jnp.sum(e, axis=0, keepdims=True)
    lrow = lax.broadcasted_iota(jnp.int32, soft.shape, 0)
    lower_all = jnp.sum(jnp.where((lrow >= 1) & (lrow <= layer), soft, 0.0), axis=0, keepdims=True)

    C, SUB = HG_CHUNK, HG_SUB
    tt = q_ref.shape[0]
    row = lax.broadcasted_iota(jnp.int32, (C, C), 0)
    col = lax.broadcasted_iota(jnp.int32, (C, C), 1)
    tri = jnp.where(col <= row, 1.0, 0.0).astype(BF)
    tri_sub = jnp.where(col < (row // SUB) * SUB, 1.0, 0.0).astype(BF)
    tri2 = jnp.concatenate([tri, tri_sub], axis=0)
    nsub = C // SUB
    mrow = lax.broadcasted_iota(jnp.int32, (C, nsub * C), 0)
    mcol = lax.broadcasted_iota(jnp.int32, (C, nsub * C), 1)
    keep = jnp.where(mcol // C == mrow // SUB, mcol % C, C) <= mrow
    w = MIX_W

    def chunk(c, states):
        rows = slice(c * C, (c + 1) * C)
        f_all = lower_all + (1.0 - lower_all) * _sigmoid(f_ref[rows, :])
        log_f = jnp.log(f_all)
        hi, mid, lo = _split3(log_f)
        cum = jnp.dot(tri2, jnp.concatenate([hi, mid, lo], axis=1), preferred_element_type=F32)
        cum = cum[:, 0:w] + cum[:, w:2 * w] + cum[:, 2 * w:3 * w]
        b_all, b_sub_all = cum[0:C], cum[C:2 * C]
        b_end_all = jnp.sum(log_f, axis=0, keepdims=True)
        new_states = []
        for h in range(N_HEADS):
            cols = slice(h * HEAD_DIM, (h + 1) * HEAD_DIM)
            b, b_sub, b_end = b_all[:, cols], b_sub_all[:, cols], b_end_all[:, cols]
            q = _silu(q_ref[rows, cols])
            k = 1.0 - f_all[:, cols]
            v = i_ref[rows, cols]
            o_inter = _dot_nt(q * jnp.exp(b), states[h])
            qt = q * jnp.exp(b - b_sub)
            kt = jnp.concatenate(
                [k * jnp.exp(jnp.minimum(b_sub[s * SUB:s * SUB + 1] - b, HG_EXP_CLAMP)) for s in range(nsub)],
                axis=0)
            sc = jnp.where(keep, _dot_nt(qt, kt), 0.0)
            o = o_inter + _dot(sc, jnp.concatenate([v] * nsub, axis=0))
            o = _rms(o, nw_ref[...], 1e-6) * _silu(g_ref[rows, cols])
            o_ref[rows, cols] = o.astype(o_ref.dtype)
            new_states.append(states[h] * jnp.exp(b_end) + _dot_tn(v, k * jnp.exp(b_end - b)))
        return new_states

    states = [state_ref[h] for h in range(N_HEADS)]
    for c in range(tt // C):
        states = chunk(c, states)
    for h in range(N_HEADS):
        state_ref[h] = states[h]


def hgrn2_mixer(proj, lower_bounds, norm_w, layer, tt=256):
    B, S, _ = proj.shape
    L = lower_bounds.shape[0]
    tt = min(tt, S)
    seg = lambda off: pl.BlockSpec((None, tt, MIX_W), lambda b, t: (b, t, (BLK_HG + off) // 4))
    return pl.pallas_call(
        functools.partial(_hgrn2_kernel, layer),
        grid=(B, S // tt),
        in_specs=[seg(0), seg(4), seg(8), seg(12),
                  pl.BlockSpec((L, MIX_W), lambda b, t: (0, 0)),
                  pl.BlockSpec((1, LANE), lambda b, t: (0, 0))],
        out_specs=pl.BlockSpec((None, tt, MIX_W), lambda b, t: (b, t, 0)),
        out_shape=jax.ShapeDtypeStruct((B, S, MIX_W), BF),
        scratch_shapes=[pltpu.VMEM((N_HEADS, HEAD_DIM, HEAD_DIM), F32)],
        compiler_params=_cparams(("parallel", "arbitrary")),
        name="hgrn2",
    )(proj, proj, proj, proj, lower_bounds, norm_w.reshape(1, LANE))


def _flash_init(m_ref, acc_ref, vx_ref):
    m_ref[...] = jnp.full_like(m_ref, NEG)
    acc_ref[...] = jnp.zeros_like(acc_ref)
    vx_ref[:, LANE:2 * LANE] = jnp.ones((vx_ref.shape[0], LANE), BF)


def _flash_step(qs_ref, k_ref, v_ref, vx_ref, s_ref, p_ref, m_ref, acc_ref, rb, bias_fn):
    nrows, tk = s_ref.shape
    k = k_ref[...].astype(BF)
    vx_ref[:, 0:LANE] = v_ref[...].astype(BF)
    for c in range(nrows // FLASH_MM_ROWS):
        rows = slice(c * FLASH_MM_ROWS, (c + 1) * FLASH_MM_ROWS)
        s_ref[rows, :] = _dot_nt(qs_ref[rows, :], k)
    for r in range(nrows // rb):
        rows = slice(r * rb, (r + 1) * rb)
        s = s_ref[rows, :]
        if bias_fn is not None:
            s = s + bias_fn(r * rb)
        m_prev = m_ref[rows, :]
        m_new = jnp.maximum(m_prev, jnp.max(s, axis=-1, keepdims=True))
        p_ref[rows, :] = jnp.exp2(s - pltpu.repeat(m_new, tk // LANE, axis=1)).astype(BF)
        alpha = jnp.exp2(m_prev - m_new)
        acc_ref[rows, :] = acc_ref[rows, :] * pltpu.repeat(alpha, 2, axis=1)
        m_ref[rows, :] = m_new
    vx = vx_ref[...]
    for c in range(nrows // FLASH_MM_ROWS):
        rows = slice(c * FLASH_MM_ROWS, (c + 1) * FLASH_MM_ROWS)
        acc_ref[rows, :] += jnp.dot(p_ref[rows, :], vx, preferred_element_type=F32)


def _diff_attn_kernel(lambda_init, tq, tk, rb, q_ref, k_ref, v_ref, lam_ref, nw_ref, o_ref,
                      qs_ref, m_ref, acc_ref, vx_ref, s_ref, p_ref):
    i, j = pl.program_id(2), pl.program_id(3)
    dc = HEAD_DIM // 2

    @pl.when(j == 0)
    def _():
        q = q_ref[...] * (dc ** -0.5 * LOG2E)
        first = lax.broadcasted_iota(jnp.int32, q.shape, 1) < dc
        qs_ref[0:tq, :] = jnp.where(first, q, 0.0).astype(BF)
        qs_ref[tq:2 * tq, :] = jnp.where(first, 0.0, q).astype(BF)
        _flash_init(m_ref, acc_ref, vx_ref)

    def causal_bias(r0):
        r = r0 + lax.broadcasted_iota(jnp.int32, (rb, tk), 0)
        qloc = jnp.where(r >= tq, r - tq, r)
        col = lax.broadcasted_iota(jnp.int32, (rb, tk), 1)
        return jnp.where(col - qloc <= i * tq - j * tk, 0.0, NEG)

    needed = j * tk <= i * tq + tq - 1
    unmasked = (j + 1) * tk - 1 <= i * tq

    @pl.when(needed & unmasked)
    def _():
        _flash_step(qs_ref, k_ref, v_ref, vx_ref, s_ref, p_ref, m_ref, acc_ref, rb, None)

    @pl.when(needed & jnp.logical_not(unmasked))
    def _():
        _flash_step(qs_ref, k_ref, v_ref, vx_ref, s_ref, p_ref, m_ref, acc_ref, rb, causal_bias)

    @pl.when(j == pl.num_programs(3) - 1)
    def _():
        lam = (jnp.exp(jnp.sum(lam_ref[0:1] * lam_ref[1:2], axis=-1, keepdims=True))
               - jnp.exp(jnp.sum(lam_ref[2:3] * lam_ref[3:4], axis=-1, keepdims=True)) + lambda_init)
        o = acc_ref[:, 0:LANE] / acc_ref[:, LANE:2 * LANE]
        o = o[0:tq] - lam * o[tq:2 * tq]
        o = _rms(o, nw_ref[...], 1e-6) * (1.0 - lambda_init)
        o_ref[...] = o.astype(o_ref.dtype)


def diff_attention_mixer(proj, lam_params, norm_w, lambda_init, tq=1024, tk=512, rb=32):
    B, S, _ = proj.shape
    tq, tk = min(tq, S), min(tk, S)
    last = lambda i: (i * tq + tq - 1) // tk
    return pl.pallas_call(
        functools.partial(_diff_attn_kernel, lambda_init, tq, tk, rb),
        grid=(B, N_HEADS, S // tq, S // tk),
        in_specs=[pl.BlockSpec((None, tq, LANE), lambda b, h, i, j: (b, i, BLK_DA + h)),
                  pl.BlockSpec((None, tk, LANE),
                               lambda b, h, i, j: (b, jnp.minimum(j, last(i)), BLK_DA + 4 + h)),
                  pl.BlockSpec((None, tk, LANE),
                               lambda b, h, i, j: (b, jnp.minimum(j, last(i)), BLK_DA + 8 + h)),
                  pl.BlockSpec((4, HEAD_DIM // 2), lambda b, h, i, j: (0, 0)),
                  pl.BlockSpec((1, LANE), lambda b, h, i, j: (0, 0))],
        out_specs=pl.BlockSpec((None, tq, LANE), lambda b, h, i, j: (b, i, h)),
        out_shape=jax.ShapeDtypeStruct((B, S, MIX_W), BF),
        scratch_shapes=[pltpu.VMEM((2 * tq, LANE), BF),
                        pltpu.VMEM((2 * tq, LANE), F32),
                        pltpu.VMEM((2 * tq, 2 * LANE), F32),
                        pltpu.VMEM((tk, 2 * LANE), BF),
                        pltpu.VMEM((2 * tq, tk), F32),
                        pltpu.VMEM((2 * tq, tk), BF)],
        compiler_params=_cparams(("parallel", "parallel", "parallel", "arbitrary")),
        name="diff_attention",
    )(proj, proj, proj, lam_params, norm_w.reshape(1, LANE))


def _compress_kernel(t_ref, pos_ref, w1_ref, w2_ref, o_ref):
    half = t_ref.shape[1]
    t = t_ref[...]
    a = _dot(t, w1_ref[0:half, :])
    bnext = _dot(t, w1_ref[half:2 * half, :])
    c0 = _dot(pos_ref[...], w1_ref[...])
    n = a.shape[0]
    hidden = a + pltpu.roll(bnext, n - 1, 0) + c0
    o_ref[...] = _dot(_silu(hidden), w2_ref[...])


def compress_tokens(t, pos, w1, w2):
    B, S, d = t.shape
    n_chunks = S // CMP_STRIDE
    hid = w1.shape[1]
    t = t.reshape(B, n_chunks, CMP_STRIDE * d)
    return pl.pallas_call(
        _compress_kernel,
        grid=(B,),
        in_specs=[pl.BlockSpec((None, n_chunks, CMP_STRIDE * d), lambda b: (b, 0, 0)),
                  pl.BlockSpec((1, CMP_LEN * d), lambda b: (0, 0)),
                  pl.BlockSpec((CMP_LEN * d, hid), lambda b: (0, 0)),
                  pl.BlockSpec((hid, d), lambda b: (0, 0))],
        out_specs=pl.BlockSpec((None, n_chunks, d), lambda b: (b, 0, 0)),
        out_shape=jax.ShapeDtypeStruct((B, n_chunks, d), F32),
        compiler_params=_cparams(("parallel",)),
        name="nsa_compress",
    )(t, pos.reshape(1, CMP_LEN * d), w1, w2)


def _cmp_select_kernel(tq, q_ref, kc_ref, vc_ref, pool_ref, o_ref, sel_ref):
    i = pl.program_id(1)
    ncp = kc_ref.shape[0]
    n_slc = sel_ref.shape[1]
    scale = HEAD_DIM ** -0.5
    qpos = i * tq + lax.broadcasted_iota(jnp.int32, (tq, 1), 0)
    cmp_end = lax.broadcasted_iota(jnp.int32, (1, ncp), 1) * CMP_STRIDE + (CMP_LEN - 1)
    mask = cmp_end <= qpos
    kc = kc_ref[...].astype(BF)
    vc = vc_ref[...].astype(BF)
    imp = jnp.zeros((tq, ncp), F32)
    for h in range(N_HEADS):
        cols = slice(h * HEAD_DIM, (h + 1) * HEAD_DIM)
        s = _dot_nt(q_ref[:, cols], kc) * scale
        s = jnp.where(mask, s, NEG)
        m = jnp.max(s, axis=-1, keepdims=True)
        p = jnp.where(mask, jnp.exp(s - m), 0.0)
        p = p / jnp.maximum(jnp.sum(p, axis=-1, keepdims=True), TINY)
        o_ref[:, cols] = _dot(p, vc)
        imp = imp + p
    imp_slc = _dot_exact_lhs(imp, pool_ref[...])
    blk = lax.broadcasted_iota(jnp.int32, (1, n_slc), 1)
    cur = qpos // SLC_BLOCK
    valid = blk <= cur
    forced = (blk == 0) | (blk > cur - N_LOCAL_BLOCKS)
    score = jnp.where(valid, jnp.where(forced, FORCE_SCORE, imp_slc), -1.0)
    sel = jnp.zeros((tq, n_slc), F32)
    for _ in range(min(SLC_TOPK, n_slc)):
        idx = jnp.argmax(score, axis=-1, keepdims=True)
        pick = blk == idx
        sel = jnp.where(pick, 1.0, sel)
        score = jnp.where(pick, -2.0, score)
    sel_ref[...] = sel.astype(sel_ref.dtype)


def nsa_compressed_and_select(proj, kc, vc, tq=256):
    B, S, _ = proj.shape
    ncp = kc.shape[1]
    n_slc = S // SLC_BLOCK
    n = jnp.arange(ncp)[:, None]
    jj = jnp.arange(n_slc)[None, :]
    w = n - (SLC_BLOCK // CMP_STRIDE) * jj + 1
    pool = jnp.where((w >= 0) & (w <= 4), jnp.where((w == 0) | (w == 4), 1.0, 2.0), 0.0).astype(BF)
    return pl.pallas_call(
        functools.partial(_cmp_select_kernel, tq),
        grid=(B, S // tq),
        in_specs=[pl.BlockSpec((None, tq, MIX_W), lambda b, i: (b, i, BLK_NQ // 4)),
                  pl.BlockSpec((None, ncp, HEAD_DIM), lambda b, i: (b, 0, 0)),
                  pl.BlockSpec((None, ncp, HEAD_DIM), lambda b, i: (b, 0, 0)),
                  pl.BlockSpec((ncp, n_slc), lambda b, i: (0, 0))],
        out_specs=[pl.BlockSpec((None, tq, MIX_W), lambda b, i: (b, i, 0)),
                   pl.BlockSpec((None, tq, n_slc), lambda b, i: (b, i, 0))],
        out_shape=[jax.ShapeDtypeStruct((B, S, MIX_W), F32),
                   jax.ShapeDtypeStruct((B, S, n_slc), BF)],
        compiler_params=_cparams(("parallel", "parallel")),
        name="nsa_cmp_select",
    )(proj, kc, vc, pool)


def _slc_attn_kernel(tq, tk, rb, q_ref, k_ref, v_ref, sel_ref, exp_ref, o_ref,
                     qs_ref, m_ref, acc_ref, vx_ref, s_ref, p_ref, bias_ref):
    i, j = pl.program_id(1), pl.program_id(2)

    @pl.when(j == 0)
    def _():
        for h in range(N_HEADS):
            q = q_ref[:, h * HEAD_DIM:(h + 1) * HEAD_DIM] * (HEAD_DIM ** -0.5 * LOG2E)
            qs_ref[h * tq:(h + 1) * tq, :] = q.astype(BF)
        _flash_init(m_ref, acc_ref, vx_ref)

    @pl.when(j * tk <= i * tq + tq - 1)
    def _():
        picked = jnp.dot(sel_ref[...], exp_ref[...], preferred_element_type=F32)
        qpos = i * tq + lax.broadcasted_iota(jnp.int32, (tq, tk), 0)
        kpos = j * tk + lax.broadcasted_iota(jnp.int32, (tq, tk), 1)
        bias_ref[...] = jnp.where(kpos <= qpos, (picked - 1.0) * (-NEG), NEG)
        _flash_step(qs_ref, k_ref, v_ref, vx_ref, s_ref, p_ref, m_ref, acc_ref, rb,
                    lambda r0: bias_ref[(r0 % tq):(r0 % tq) + rb, :])

    @pl.when(j == pl.num_programs(2) - 1)
    def _():
        for h in range(N_HEADS):
            rows = slice(h * tq, (h + 1) * tq)
            o_ref[:, h * HEAD_DIM:(h + 1) * HEAD_DIM] = acc_ref[rows, 0:LANE] / acc_ref[rows, LANE:2 * LANE]


def nsa_selected_attention(proj, sel, tq=256, tk=512, rb=32):
    B, S, _ = proj.shape
    tq, tk = min(tq, S), min(tk, S)
    n_slc = S // SLC_BLOCK
    expand = (jnp.arange(n_slc)[:, None] == (jnp.arange(S)[None, :] // SLC_BLOCK)).astype(BF)
    last = lambda i: (i * tq + tq - 1) // tk
    kv = lambda off: pl.BlockSpec((None, tk, LANE),
                                  lambda b, i, j: (b, jnp.minimum(j, last(i)), BLK_NKV + off))
    rows = N_HEADS * tq
    return pl.pallas_call(
        functools.partial(_slc_attn_kernel, tq, tk, rb),
        grid=(B, S // tq, S // tk),
        in_specs=[pl.BlockSpec((None, tq, MIX_W), lambda b, i, j: (b, i, BLK_NQ // 4)),
                  kv(2), kv(3),
                  pl.BlockSpec((None, tq, n_slc), lambda b, i, j: (b, i, 0)),
                  pl.BlockSpec((n_slc, tk), lambda b, i, j: (0, jnp.minimum(j, last(i))))],
        out_specs=pl.BlockSpec((None, tq, MIX_W), lambda b, i, j: (b, i, 0)),
        out_shape=jax.ShapeDtypeStruct((B, S, MIX_W), F32),
        scratch_shapes=[pltpu.VMEM((rows, LANE), BF),
                        pltpu.VMEM((rows, LANE), F32),
                        pltpu.VMEM((rows, 2 * LANE), F32),
                        pltpu.VMEM((tk, 2 * LANE), BF),
                        pltpu.VMEM((rows, tk), F32),
                        pltpu.VMEM((rows, tk), BF),
                        pltpu.VMEM((tq, tk), F32)],
        compiler_params=_cparams(("parallel", "parallel", "arbitrary")),
        name="nsa_selected",
    )(proj, proj, proj, sel, expand)


def _swa_combine_kernel(tq, q_ref, kp_ref, kc_ref, vp_ref, vc_ref, g_ref, ocmp_ref, oslc_ref, o_ref):
    i = pl.program_id(1)
    scale = HEAD_DIM ** -0.5
    k = jnp.concatenate([kp_ref[...], kc_ref[...]], axis=0).astype(BF)
    v = jnp.concatenate([vp_ref[...], vc_ref[...]], axis=0).astype(BF)
    nk = k.shape[0]
    qpos = i * tq + lax.broadcasted_iota(jnp.int32, (tq, nk), 0)
    kpos = i * tq - SWA_WINDOW + lax.broadcasted_iota(jnp.int32, (tq, nk), 1)
    dist = qpos - kpos
    mask = (dist >= 0) & (dist < SWA_WINDOW) & (kpos >= 0)
    gates = _sigmoid(g_ref[...])
    for h in range(N_HEADS):
        cols = slice(h * HEAD_DIM, (h + 1) * HEAD_DIM)
        s = jnp.where(mask, _dot_nt(q_ref[:, cols] * scale, k), NEG)
        m = jnp.max(s, axis=-1, keepdims=True)
        p = jnp.where(mask, jnp.exp(s - m), 0.0)
        p = p / jnp.maximum(jnp.sum(p, axis=-1, keepdims=True), TINY)
        o_swa = _dot(p, v)
        o = (gates[:, 3 * h:3 * h + 1] * ocmp_ref[:, cols]
             + gates[:, 3 * h + 1:3 * h + 2] * oslc_ref[:, cols]
             + gates[:, 3 * h + 2:3 * h + 3] * o_swa)
        o_ref[:, cols] = o.astype(o_ref.dtype)


def nsa_window_and_combine(proj, o_cmp, o_slc):
    B, S, _ = proj.shape
    tq = SWA_WINDOW
    kvp = lambda off: pl.BlockSpec((None, tq, LANE), lambda b, i: (b, jnp.maximum(i - 1, 0), BLK_NKV + off))
    kvc = lambda off: pl.BlockSpec((None, tq, LANE), lambda b, i: (b, i, BLK_NKV + off))
    full = lambda: pl.BlockSpec((None, tq, MIX_W), lambda b, i: (b, i, 0))
    return pl.pallas_call(
        functools.partial(_swa_combine_kernel, tq),
        grid=(B, S // tq),
        in_specs=[pl.BlockSpec((None, tq, MIX_W), lambda b, i: (b, i, BLK_NQ // 4)),
                  kvp(4), kvc(4), kvp(5), kvc(5),
                  pl.BlockSpec((None, tq, LANE), lambda b, i: (b, i, BLK_NG)),
                  full(), full()],
        out_specs=full(),
        out_shape=jax.ShapeDtypeStruct((B, S, MIX_W), BF),
        compiler_params=_cparams(("parallel", "parallel")),
        name="nsa_window_combine",
    )(proj, proj, proj, proj, proj, proj, o_cmp, o_slc)


def _gelu_tanh(x):
    return 0.5 * x * (1.0 + jnp.tanh(math.sqrt(2.0 / math.pi) * (x + 0.044715 * (x * x * x))))


def _gmlp_kernel(u_ref, v_ref, lnw_ref, lnb_ref, ws_ref, bs_ref, o_ref):
    C = ws_ref.shape[1]
    row = lax.broadcasted_iota(jnp.int32, (C, C), 0)
    col = lax.broadcasted_iota(jnp.int32, (C, C), 1)
    causal = col <= row
    for c in range(u_ref.shape[0] // C):
        rows = slice(c * C, (c + 1) * C)
        u = _gelu_tanh(u_ref[rows, :])
        v = _gelu_tanh(v_ref[rows, :])
        mu = jnp.mean(v, axis=-1, keepdims=True)
        var = jnp.mean(jnp.square(v - mu), axis=-1, keepdims=True)
        v = (v - mu) * lax.rsqrt(var + 1e-5) * lnw_ref[...] + lnb_ref[...]
        for g in range(N_HEADS):
            cols = slice(g * HEAD_DIM, (g + 1) * HEAD_DIM)
            w = jnp.where(causal, ws_ref[g], 0.0)
            vm = _dot(w, v[:, cols]) + bs_ref[:, cols]
            o_ref[rows, cols] = (u[:, cols] * vm).astype(o_ref.dtype)


def chunk_gmlp_mixer(proj, ln_w, ln_b, w_s, b_s, tt=512):
    B, S, _ = proj.shape
    G, C, _ = w_s.shape
    tt = min(tt, S)
    bias = jnp.repeat(b_s.T, HEAD_DIM, axis=1)
    return pl.pallas_call(
        _gmlp_kernel,
        grid=(B, S // tt),
        in_specs=[pl.BlockSpec((None, tt, MIX_W), lambda b, t: (b, t, BLK_GM // 4)),
                  pl.BlockSpec((None, tt, MIX_W), lambda b, t: (b, t, BLK_GM // 4 + 1)),
                  pl.BlockSpec((1, MIX_W), lambda b, t: (0, 0)),
                  pl.BlockSpec((1, MIX_W), lambda b, t: (0, 0)),
                  pl.BlockSpec((G, C, C), lambda b, t: (0, 0, 0)),
                  pl.BlockSpec((C, MIX_W), lambda b, t: (0, 0))],
        out_specs=pl.BlockSpec((None, tt, MIX_W), lambda b, t: (b, t, 0)),
        out_shape=jax.ShapeDtypeStruct((B, S, MIX_W), BF),
        compiler_params=_cparams(("parallel", "parallel")),
        name="chunk_gmlp",
    )(proj, proj, ln_w.reshape(1, MIX_W), ln_b.reshape(1, MIX_W), w_s, bias)


def _merge_kernel(oa_ref, ob_ref, oc_ref, od_ref, ga_ref, gb_ref, gc_ref, gd_ref, wb_ref, wo_ref, x_ref,
                  o_ref, acc_ref):
    j = pl.program_id(1)

    @pl.when(j == 0)
    def _():
        acc_ref[...] = jnp.zeros_like(acc_ref)

    merged = None
    for b, (o_b, g_b) in enumerate(((oa_ref, ga_ref), (ob_ref, gb_ref), (oc_ref, gc_ref), (od_ref, gd_ref))):
        term = _sigmoid(g_b[...]) * jnp.dot(o_b[...], wb_ref[b], preferred_element_type=F32)
        merged = term if merged is None else merged + term
    acc_ref[...] += jnp.dot(merged.astype(BF), wo_ref[...], preferred_element_type=F32)

    @pl.when(j == pl.num_programs(1) - 1)
    def _():
        o_ref[...] = x_ref[...] + acc_ref[...]


def merge_project(x, proj, branches, w_branch, w_out, tm=512, tn=512):
    T, D = x.shape
    tm = min(tm, T)
    per = D // tn
    o_spec = pl.BlockSpec((tm, MIX_W), lambda i, j: (i, 0))
    g_spec = lambda b: pl.BlockSpec((tm, tn), lambda i, j: (i, (BLK_MG * LANE) // tn + b * per + j))
    return pl.pallas_call(
        _merge_kernel,
        grid=(T // tm, per),
        in_specs=[o_spec, o_spec, o_spec, o_spec, g_spec(0), g_spec(1), g_spec(2), g_spec(3),
                  pl.BlockSpec((N_BRANCH, MIX_W, tn), lambda i, j: (0, 0, j)),
                  pl.BlockSpec((tn, D), lambda i, j: (j, 0)),
                  pl.BlockSpec((tm, D), lambda i, j: (i, 0))],
        out_specs=pl.BlockSpec((tm, D), lambda i, j: (i, 0)),
        out_shape=jax.ShapeDtypeStruct((T, D), F32),
        scratch_shapes=[pltpu.VMEM((tm, D), F32)],
        compiler_params=_cparams(("parallel", "arbitrary")),
        name="merge_project",
    )(*branches, proj, proj, proj, proj, w_branch, w_out, x)


def _cross_attn_kernel(q_ref, k_ref, v_ref, o_ref):
    dh = q_ref.shape[1] // XA_HEADS
    for h in range(XA_HEADS):
        cols = slice(h * dh, (h + 1) * dh)
        s = _dot_nt(q_ref[:, cols], k_ref[:, cols]) * (dh ** -0.5)
        m = jnp.max(s, axis=-1, keepdims=True)
        p = jnp.exp(s - m)
        p = p / jnp.sum(p, axis=-1, keepdims=True)
        o_ref[:, cols] = _dot(p, v_ref[:, cols]).astype(o_ref.dtype)


def cross_attention_core(q, kv, tq=512):
    B, S, D = q.shape
    M = kv.shape[1]
    tq = min(tq, S)
    return pl.pallas_call(
        _cross_attn_kernel,
        grid=(B, S // tq),
        in_specs=[pl.BlockSpec((None, tq, D), lambda b, i: (b, i, 0)),
                  pl.BlockSpec((None, M, D), lambda b, i: (b, 0, 0)),
                  pl.BlockSpec((None, M, D), lambda b, i: (b, 0, 1))],
        out_specs=pl.BlockSpec((None, tq, D), lambda b, i: (b, i, 0)),
        out_shape=jax.ShapeDtypeStruct((B, S, D), BF),
        compiler_params=_cparams(("parallel", "parallel")),
        name="cross_attention",
    )(q, kv, kv)


def _ffn_kernel(x_ref, nw_ref, wg_ref, wu_ref, wo_ref, o_ref, h_ref, acc_ref):
    j = pl.program_id(1)

    @pl.when(j == 0)
    def _():
        h_ref[...] = _rms(x_ref[...], nw_ref[...], 1e-6).astype(BF)
        acc_ref[...] = jnp.zeros_like(acc_ref)

    h = h_ref[...]
    gate = jnp.dot(h, wg_ref[...], preferred_element_type=F32)
    up = jnp.dot(h, wu_ref[...], preferred_element_type=F32)
    acc_ref[...] += jnp.dot((_silu(gate) * up).astype(BF), wo_ref[...], preferred_element_type=F32)

    @pl.when(j == pl.num_programs(1) - 1)
    def _():
        o_ref[...] = x_ref[...] + acc_ref[...]


def swiglu_ffn(x, norm_w, w_in, w_out, tm=512, tf=512):
    T, D = x.shape
    F = w_out.shape[0]
    tm = min(tm, T)
    nf = F // tf
    return pl.pallas_call(
        _ffn_kernel,
        grid=(T // tm, nf),
        in_specs=[pl.BlockSpec((tm, D), lambda i, j: (i, 0)),
                  pl.BlockSpec((1, D), lambda i, j: (0, 0)),
                  pl.BlockSpec((D, tf), lambda i, j: (0, j)),
                  pl.BlockSpec((D, tf), lambda i, j: (0, nf + j)),
                  pl.BlockSpec((tf, D), lambda i, j: (j, 0))],
        out_specs=pl.BlockSpec((tm, D), lambda i, j: (i, 0)),
        out_shape=jax.ShapeDtypeStruct((T, D), F32),
        scratch_shapes=[pltpu.VMEM((tm, D), BF), pltpu.VMEM((tm, D), F32)],
        compiler_params=_cparams(("parallel", "arbitrary")),
        name="swiglu_ffn",
    )(x, norm_w.reshape(1, D), w_in, w_in, w_out)


def _pack_w_in(w):
    d = w.shape[0]
    n_gate = 3 * N_HEADS
    gate_off = BLK_NG * LANE
    z = lambda n: jnp.zeros((d, n), w.dtype)
    rest = w[:, gate_off + n_gate:]
    packed = jnp.concatenate([w[:, :gate_off + n_gate], z(BLK_GM * LANE - gate_off - n_gate), rest], axis=1)
    assert packed.shape[1] == N_PROJ, packed.shape
    return packed.astype(BF)


def kernel(x, mem, mem_norm_w, mix_norm_w, w_in, hg_lower_bounds, hg_norm_w, da_lambda_q1, da_lambda_k1, da_lambda_q2, da_lambda_k2, da_norm_w, nsa_pos_k, nsa_cmp_w1_k, nsa_cmp_w2_k, nsa_pos_v, nsa_cmp_w1_v, nsa_cmp_w2_v, gm_ln_w, gm_ln_b, gm_w_s, gm_b_s, w_branch, w_out, xa_norm_w, xa_w_q, xa_w_kv, xa_w_o, ffn_norm_w, ffn_w_in, ffn_w_out, final_norm_w):
    B, S, D = x.shape
    M = mem.shape[1]
    depth = w_in.shape[0]
    assert D == BLK_MG * LANE // 3 and S % SWA_WINDOW == 0, (S, D)
    T = B * S
    x = x.reshape(T, D)
    mem2 = mem.reshape(B * M, D)
    for l in range(depth):
        lambda_init = 0.8 - 0.6 * math.exp(-0.3 * l)
        proj = norm_matmul(x, mix_norm_w[l], _pack_w_in(w_in[l]), F32).reshape(B, S, N_PROJ)
        o_a = hgrn2_mixer(proj, hg_lower_bounds, hg_norm_w[l], l)
        lam_params = jnp.stack([da_lambda_q1[l], da_lambda_k1[l], da_lambda_q2[l], da_lambda_k2[l]])
        o_b = diff_attention_mixer(proj, lam_params, da_norm_w[l], lambda_init)
        nkv = lambda off: proj[:, :, (BLK_NKV + off) * LANE:(BLK_NKV + off + 1) * LANE]
        kc = compress_tokens(nkv(0), nsa_pos_k[l], nsa_cmp_w1_k[l].astype(BF), nsa_cmp_w2_k[l].astype(BF))
        vc = compress_tokens(nkv(1), nsa_pos_v[l], nsa_cmp_w1_v[l].astype(BF), nsa_cmp_w2_v[l].astype(BF))
        o_cmp, sel = nsa_compressed_and_select(proj, kc, vc)
        o_slc = nsa_selected_attention(proj, sel)
        o_c = nsa_window_and_combine(proj, o_cmp, o_slc)
        o_d = chunk_gmlp_mixer(proj, gm_ln_w[l], gm_ln_b[l], gm_w_s[l], gm_b_s[l])
        branches = [o.reshape(T, MIX_W) for o in (o_a, o_b, o_c, o_d)]
        x = merge_project(x, proj.reshape(T, N_PROJ), branches, w_branch[l].astype(BF), w_out[l].astype(BF))
        q = norm_matmul(x, xa_norm_w[l], xa_w_q[l].astype(BF), BF).reshape(B, S, D)
        kv = norm_matmul(mem2, mem_norm_w, xa_w_kv[l].astype(BF), F32).reshape(B, M, 2 * D)
        att = cross_attention_core(q, kv).reshape(T, D)
        x = matmul_residual(att, xa_w_o[l].astype(BF), x)
        x = swiglu_ffn(x, ffn_norm_w[l], ffn_w_in[l].astype(BF), ffn_w_out[l].astype(BF))
    return rmsnorm(x, final_norm_w).reshape(B, S, D)
```

```python
import functools
import math

import jax
import jax.numpy as jnp
from jax import lax
from jax.experimental import pallas as pl
from jax.experimental.pallas import tpu as pltpu

BF = jnp.bfloat16
F32 = jnp.float32

LANE = 128
HEAD_DIM = 128
N_HEADS = 4
MIX_W = N_HEADS * HEAD_DIM
N_BRANCH = 4
HG_CHUNK = 64
HG_SUB = 16
HG_EXP_CLAMP = 80.0
CMP_LEN = 32
CMP_STRIDE = 16
SLC_BLOCK = 64
SLC_TOPK = 16
N_LOCAL_BLOCKS = 2
SWA_WINDOW = 512
XA_HEADS = 4
NEG = -1e30
TINY = 1e-30
FORCE_SCORE = 1e4
LOG2E = math.log2(math.e)
FLASH_MM_ROWS = 256
VMEM_LIMIT = 56 * 1024 * 1024

BLK_HG = 0
BLK_DA = 16
BLK_NQ = 28
BLK_NKV = 32
BLK_NG = 38
BLK_GM = 40
BLK_MG = 48
N_PROJ = 112 * LANE


def _cparams(sem):
    return pltpu.CompilerParams(dimension_semantics=sem, vmem_limit_bytes=VMEM_LIMIT)


def _dot(a, b):
    return jnp.dot(a.astype(BF), b.astype(BF), preferred_element_type=F32)


def _dot_nt(a, b):
    return lax.dot_general(a.astype(BF), b.astype(BF), (((1,), (1,)), ((), ())),
                           preferred_element_type=F32)


def _dot_tn(a, b):
    return lax.dot_general(a.astype(BF), b.astype(BF), (((0,), (0,)), ((), ())),
                           preferred_element_type=F32)


def _split3(a):
    hi = a.astype(BF)
    r = a - hi.astype(F32)
    mid = r.astype(BF)
    lo = (r - mid.astype(F32)).astype(BF)
    return hi, mid, lo


def _dot_exact_rhs(a_bf, b):
    hi, mid, lo = _split3(b)
    f = lambda t: jnp.dot(a_bf, t, preferred_element_type=F32)
    return f(hi) + f(mid) + f(lo)


def _dot_exact_lhs(a, b_bf):
    hi, mid, lo = _split3(a)
    f = lambda t: jnp.dot(t, b_bf, preferred_element_type=F32)
    return f(hi) + f(mid) + f(lo)


def _rms(x, w, eps):
    return x * lax.rsqrt(jnp.mean(x * x, axis=-1, keepdims=True) + eps) * w


def _sigmoid(x):
    return 1.0 / (1.0 + jnp.exp(-x))


def _silu(x):
    return x * _sigmoid(x)


def _norm_matmul_kernel(w_is_transposed, x_ref, nw_ref, w_ref, o_ref, h_ref):
    @pl.when(pl.program_id(1) == 0)
    def _():
        h_ref[...] = _rms(x_ref[...], nw_ref[...], 1e-6).astype(BF)

    if w_is_transposed:
        o = lax.dot_general(h_ref[...], w_ref[...], (((1,), (1,)), ((), ())), preferred_element_type=F32)
    else:
        o = jnp.dot(h_ref[...], w_ref[...], preferred_element_type=F32)
    o_ref[...] = o.astype(o_ref.dtype)


def norm_matmul(x, norm_w, w, layer, out_dtype, tm=512, tn=1024, w_is_transposed=False):
    T, D = x.shape
    N = w.shape[1] if w_is_transposed else w.shape[2]
    tm, tn = min(tm, T), min(tn, N)
    w_spec = (pl.BlockSpec((None, tn, D), lambda i, j: (layer, j, 0)) if w_is_transposed
              else pl.BlockSpec((None, D, tn), lambda i, j: (layer, 0, j)))
    return pl.pallas_call(
        functools.partial(_norm_matmul_kernel, w_is_transposed),
        grid=(T // tm, N // tn),
        in_specs=[pl.BlockSpec((tm, D), lambda i, j: (i, 0)),
                  pl.BlockSpec((1, D), lambda i, j: (0, 0)),
                  w_spec],
        out_specs=pl.BlockSpec((tm, tn), lambda i, j: (i, j)),
        out_shape=jax.ShapeDtypeStruct((T, N), out_dtype),
        scratch_shapes=[pltpu.VMEM((tm, D), BF)],
        compiler_params=_cparams(("parallel", "arbitrary")),
        name="norm_matmul",
    )(x, norm_w.reshape(1, D), w)


def _matmul_residual_kernel(a_ref, w_ref, r_ref, o_ref):
    o_ref[...] = r_ref[...] + jnp.dot(a_ref[...], w_ref[...], preferred_element_type=F32)


def matmul_residual(a, w, layer, r, tm=512, tn=1024):
    T, K = a.shape
    N = w.shape[2]
    tm, tn = min(tm, T), min(tn, N)
    return pl.pallas_call(
        _matmul_residual_kernel,
        grid=(T // tm, N // tn),
        in_specs=[pl.BlockSpec((tm, K), lambda i, j: (i, 0)),
                  pl.BlockSpec((None, K, tn), lambda i, j: (layer, 0, j)),
                  pl.BlockSpec((tm, tn), lambda i, j: (i, j))],
        out_specs=pl.BlockSpec((tm, tn), lambda i, j: (i, j)),
        out_shape=jax.ShapeDtypeStruct((T, N), F32),
        compiler_params=_cparams(("parallel", "arbitrary")),
        name="matmul_residual",
    )(a, w, r)


def _rmsnorm_kernel(x_ref, w_ref, o_ref):
    o_ref[...] = _rms(x_ref[...], w_ref[...], 1e-6)


def rmsnorm(x, w, tm=512):
    T, D = x.shape
    tm = min(tm, T)
    return pl.pallas_call(
        _rmsnorm_kernel,
        grid=(T // tm,),
        in_specs=[pl.BlockSpec((tm, D), lambda i: (i, 0)),
                  pl.BlockSpec((1, D), lambda i: (0, 0))],
        out_specs=pl.BlockSpec((tm, D), lambda i: (i, 0)),
        out_shape=jax.ShapeDtypeStruct((T, D), F32),
        compiler_params=_cparams(("parallel",)),
        name="final_rmsnorm",
    )(x, w.reshape(1, D))


def _hgrn2_kernel(layer, q_ref, f_ref, i_ref, g_ref, lb_ref, nw_ref, o_ref, state_ref):
    @pl.when(pl.program_id(1) == 0)
    def _():
        state_ref[...] = jnp.zeros_like(state_ref)

    lb_all = lb_ref[...]
    e = jnp.exp(lb_all - jnp.max(lb_all, axis=0, keepdims=True))
    soft = e / jnp.sum(e, axis=0, keepdims=True)
    lrow = lax.broadcasted_iota(jnp.int32, soft.shape, 0)
    lower_all = jnp.sum(jnp.where((lrow >= 1) & (lrow <= layer), soft, 0.0), axis=0, keepdims=True)

    C, SUB = HG_CHUNK, HG_SUB
    tt = q_ref.shape[0]
    row = lax.broadcasted_iota(jnp.int32, (C, C), 0)
    col = lax.broadcasted_iota(jnp.int32, (C, C), 1)
    tri = jnp.where(col <= row, 1.0, 0.0).astype(BF)
    tri_sub = jnp.where(col < (row // SUB) * SUB, 1.0, 0.0).astype(BF)
    tri2 = jnp.concatenate([tri, tri_sub], axis=0)
    nsub = C // SUB
    mrow = lax.broadcasted_iota(jnp.int32, (C, nsub * C), 0)
    mcol = lax.broadcasted_iota(jnp.int32, (C, nsub * C), 1)
    keep = jnp.where(mcol // C == mrow // SUB, mcol % C, C) <= mrow
    w = MIX_W

    def chunk(c, states):
        rows = slice(c * C, (c + 1) * C)
        f_all = lower_all + (1.0 - lower_all) * _sigmoid(f_ref[rows, :])
        log_f = jnp.log(f_all)
        hi, mid, lo = _split3(log_f)
        cum = jnp.dot(tri2, jnp.concatenate([hi, mid, lo], axis=1), preferred_element_type=F32)
        cum = cum[:, 0:w] + cum[:, w:2 * w] + cum[:, 2 * w:3 * w]
        b_all, b_sub_all = cum[0:C], cum[C:2 * C]
        b_end_all = jnp.sum(log_f, axis=0, keepdims=True)
        new_states = []
        for h in range(N_HEADS):
            cols = slice(h * HEAD_DIM, (h + 1) * HEAD_DIM)
            b, b_sub, b_end = b_all[:, cols], b_sub_all[:, cols], b_end_all[:, cols]
            q = _silu(q_ref[rows, cols])
            k = 1.0 - f_all[:, cols]
            v = i_ref[rows, cols]
            o_inter = _dot_nt(q * jnp.exp(b), states[h])
            qt = q * jnp.exp(b - b_sub)
            kt = jnp.concatenate(
                [k * jnp.exp(jnp.minimum(b_sub[s * SUB:s * SUB + 1] - b, HG_EXP_CLAMP)) for s in range(nsub)],
                axis=0)
            sc = jnp.where(keep, _dot_nt(qt, kt), 0.0)
            o = o_inter + _dot(sc, jnp.concatenate([v] * nsub, axis=0))
            o = _rms(o, nw_ref[...], 1e-6) * _silu(g_ref[rows, cols])
            o_ref[rows, cols] = o.astype(o_ref.dtype)
            new_states.append(states[h] * jnp.exp(b_end) + _dot_tn(v, k * jnp.exp(b_end - b)))
        return new_states

    states = [state_ref[h] for h in range(N_HEADS)]
    for c in range(tt // C):
        states = chunk(c, states)
    for h in range(N_HEADS):
        state_ref[h] = states[h]


def hgrn2_mixer(proj, lower_bounds, norm_w, layer, tt=256):
    B, S, _ = proj.shape
    L = lower_bounds.shape[0]
    tt = min(tt, S)
    seg = lambda off: pl.BlockSpec((None, tt, MIX_W), lambda b, t: (b, t, (BLK_HG + off) // 4))
    return pl.pallas_call(
        functools.partial(_hgrn2_kernel, layer),
        grid=(B, S // tt),
        in_specs=[seg(0), seg(4), seg(8), seg(12),
                  pl.BlockSpec((L, MIX_W), lambda b, t: (0, 0)),
                  pl.BlockSpec((1, LANE), lambda b, t: (0, 0))],
        out_specs=pl.BlockSpec((None, tt, MIX_W), lambda b, t: (b, t, 0)),
        out_shape=jax.ShapeDtypeStruct((B, S, MIX_W), BF),
        scratch_shapes=[pltpu.VMEM((N_HEADS, HEAD_DIM, HEAD_DIM), F32)],
        compiler_params=_cparams(("parallel", "arbitrary")),
        name="hgrn2",
    )(proj, proj, proj, proj, lower_bounds, norm_w.reshape(1, LANE))


def _flash_init(m_ref, acc_ref, vx_ref):
    m_ref[...] = jnp.full_like(m_ref, NEG)
    acc_ref[...] = jnp.zeros_like(acc_ref)
    vx_ref[:, LANE:2 * LANE] = jnp.ones((vx_ref.shape[0], LANE), BF)


def _flash_step(qs_ref, k_ref, v_ref, vx_ref, s_ref, p_ref, m_ref, acc_ref, rb, bias_fn):
    nrows, tk = s_ref.shape
    k = k_ref[...].astype(BF)
    vx_ref[:, 0:LANE] = v_ref[...].astype(BF)
    for c in range(nrows // FLASH_MM_ROWS):
        rows = slice(c * FLASH_MM_ROWS, (c + 1) * FLASH_MM_ROWS)
        s_ref[rows, :] = _dot_nt(qs_ref[rows, :], k)
    for r in range(nrows // rb):
        rows = slice(r * rb, (r + 1) * rb)
        s = s_ref[rows, :]
        if bias_fn is not None:
            s = s + bias_fn(r * rb)
        m_prev = m_ref[rows, :]
        m_new = jnp.maximum(m_prev, jnp.max(s, axis=-1, keepdims=True))
        p_ref[rows, :] = jnp.exp2(s - jnp.concatenate([m_new] * (tk // LANE), axis=1)).astype(BF)
        alpha = jnp.exp2(m_prev - m_new)
        acc_ref[rows, :] = acc_ref[rows, :] * jnp.concatenate([alpha, alpha], axis=1)
        m_ref[rows, :] = m_new
    vx = vx_ref[...]
    for c in range(nrows // FLASH_MM_ROWS):
        rows = slice(c * FLASH_MM_ROWS, (c + 1) * FLASH_MM_ROWS)
        acc_ref[rows, :] += jnp.dot(p_ref[rows, :], vx, preferred_element_type=F32)


def _diff_attn_kernel(lambda_init, tq, tk, rb, q_ref, k_ref, v_ref, lam_ref, nw_ref, o_ref,
                      qs_ref, m_ref, acc_ref, vx_ref, s_ref, p_ref):
    i, j = pl.program_id(2), pl.program_id(3)
    dc = HEAD_DIM // 2

    @pl.when(j == 0)
    def _():
        q = q_ref[...] * (dc ** -0.5 * LOG2E)
        first = lax.broadcasted_iota(jnp.int32, q.shape, 1) < dc
        qs_ref[0:tq, :] = jnp.where(first, q, 0.0).astype(BF)
        qs_ref[tq:2 * tq, :] = jnp.where(first, 0.0, q).astype(BF)
        _flash_init(m_ref, acc_ref, vx_ref)

    def causal_bias(r0):
        r = r0 + lax.broadcasted_iota(jnp.int32, (rb, tk), 0)
        qloc = jnp.where(r >= tq, r - tq, r)
        col = lax.broadcasted_iota(jnp.int32, (rb, tk), 1)
        return jnp.where(col - qloc <= i * tq - j * tk, 0.0, NEG)

    needed = j * tk <= i * tq + tq - 1
    unmasked = (j + 1) * tk - 1 <= i * tq

    @pl.when(needed & unmasked)
    def _():
        _flash_step(qs_ref, k_ref, v_ref, vx_ref, s_ref, p_ref, m_ref, acc_ref, rb, None)

    @pl.when(needed & jnp.logical_not(unmasked))
    def _():
        _flash_step(qs_ref, k_ref, v_ref, vx_ref, s_ref, p_ref, m_ref, acc_ref, rb, causal_bias)

    @pl.when(j == pl.num_programs(3) - 1)
    def _():
        lam = (jnp.exp(jnp.sum(lam_ref[0:1] * lam_ref[1:2], axis=-1, keepdims=True))
               - jnp.exp(jnp.sum(lam_ref[2:3] * lam_ref[3:4], axis=-1, keepdims=True)) + lambda_init)
        o = acc_ref[:, 0:LANE] / acc_ref[:, LANE:2 * LANE]
        o = o[0:tq] - lam * o[tq:2 * tq]
        o = _rms(o, nw_ref[...], 1e-6) * (1.0 - lambda_init)
        o_ref[...] = o.astype(o_ref.dtype)


def diff_attention_mixer(proj, lam_params, norm_w, lambda_init, tq=1024, tk=512, rb=32):
    B, S, _ = proj.shape
    tq, tk = min(tq, S), min(tk, S)
    last = lambda i: (i * tq + tq - 1) // tk
    return pl.pallas_call(
        functools.partial(_diff_attn_kernel, lambda_init, tq, tk, rb),
        grid=(B, N_HEADS, S // tq, S // tk),
        in_specs=[pl.BlockSpec((None, tq, LANE), lambda b, h, i, j: (b, i, BLK_DA + h)),
                  pl.BlockSpec((None, tk, LANE),
                               lambda b, h, i, j: (b, jnp.minimum(j, last(i)), BLK_DA + 4 + h)),
                  pl.BlockSpec((None, tk, LANE),
                               lambda b, h, i, j: (b, jnp.minimum(j, last(i)), BLK_DA + 8 + h)),
                  pl.BlockSpec((4, HEAD_DIM // 2), lambda b, h, i, j: (0, 0)),
                  pl.BlockSpec((1, LANE), lambda b, h, i, j: (0, 0))],
        out_specs=pl.BlockSpec((None, tq, LANE), lambda b, h, i, j: (b, i, h)),
        out_shape=jax.ShapeDtypeStruct((B, S, MIX_W), BF),
        scratch_shapes=[pltpu.VMEM((2 * tq, LANE), BF),
                        pltpu.VMEM((2 * tq, LANE), F32),
                        pltpu.VMEM((2 * tq, 2 * LANE), F32),
                        pltpu.VMEM((tk, 2 * LANE), BF),
                        pltpu.VMEM((2 * tq, tk), F32),
                        pltpu.VMEM((2 * tq, tk), BF)],
        compiler_params=_cparams(("parallel", "parallel", "parallel", "arbitrary")),
        name="diff_attention",
    )(proj, proj, proj, lam_params, norm_w.reshape(1, LANE))


def _compress_kernel(t_ref, pos_ref, w1_ref, w2_ref, o_ref):
    half = t_ref.shape[1]
    t = t_ref[...]
    a = _dot(t, w1_ref[0:half, :])
    bnext = _dot(t, w1_ref[half:2 * half, :])
    c0 = _dot(pos_ref[...], w1_ref[...])
    n = a.shape[0]
    hidden = a + pltpu.roll(bnext, n - 1, 0) + c0
    o_ref[...] = _dot(_silu(hidden), w2_ref[...])


def compress_tokens(t, pos, w1, w2):
    B, S, d = t.shape
    n_chunks = S // CMP_STRIDE
    hid = w1.shape[1]
    t = t.reshape(B, n_chunks, CMP_STRIDE * d)
    return pl.pallas_call(
        _compress_kernel,
        grid=(B,),
        in_specs=[pl.BlockSpec((None, n_chunks, CMP_STRIDE * d), lambda b: (b, 0, 0)),
                  pl.BlockSpec((1, CMP_LEN * d), lambda b: (0, 0)),
                  pl.BlockSpec((CMP_LEN * d, hid), lambda b: (0, 0)),
                  pl.BlockSpec((hid, d), lambda b: (0, 0))],
        out_specs=pl.BlockSpec((None, n_chunks, d), lambda b: (b, 0, 0)),
        out_shape=jax.ShapeDtypeStruct((B, n_chunks, d), F32),
        compiler_params=_cparams(("parallel",)),
        name="nsa_compress",
    )(t, pos.reshape(1, CMP_LEN * d), w1, w2)


def _cmp_select_kernel(tq, q_ref, kc_ref, vc_ref, pool_ref, o_ref, sel_ref):
    i = pl.program_id(1)
    ncp = kc_ref.shape[0]
    n_slc = sel_ref.shape[1]
    scale = HEAD_DIM ** -0.5
    qpos = i * tq + lax.broadcasted_iota(jnp.int32, (tq, 1), 0)
    cmp_end = lax.broadcasted_iota(jnp.int32, (1, ncp), 1) * CMP_STRIDE + (CMP_LEN - 1)
    mask = cmp_end <= qpos
    kc = kc_ref[...].astype(BF)
    vc = vc_ref[...].astype(BF)
    imp = jnp.zeros((tq, ncp), F32)
    for h in range(N_HEADS):
        cols = slice(h * HEAD_DIM, (h + 1) * HEAD_DIM)
        s = _dot_nt(q_ref[:, cols], kc) * scale
        s = jnp.where(mask, s, NEG)
        m = jnp.max(s, axis=-1, keepdims=True)
        p = jnp.where(mask, jnp.exp(s - m), 0.0)
        p = p / jnp.maximum(jnp.sum(p, axis=-1, keepdims=True), TINY)
        o_ref[:, cols] = _dot(p, vc)
        imp = imp + p
    imp_slc = _dot_exact_lhs(imp, pool_ref[...])
    blk = lax.broadcasted_iota(jnp.int32, (1, n_slc), 1)
    cur = qpos // SLC_BLOCK
    valid = blk <= cur
    forced = (blk == 0) | (blk > cur - N_LOCAL_BLOCKS)
    score = jnp.where(valid, jnp.where(forced, FORCE_SCORE, imp_slc), -1.0)
    sel = jnp.zeros((tq, n_slc), F32)
    for _ in range(min(SLC_TOPK, n_slc)):
        idx = jnp.argmax(score, axis=-1, keepdims=True)
        pick = blk == idx
        sel = jnp.where(pick, 1.0, sel)
        score = jnp.where(pick, -2.0, score)
    sel_ref[...] = sel.astype(sel_ref.dtype)


def nsa_compressed_and_select(proj, kc, vc, tq=256):
    B, S, _ = proj.shape
    ncp = kc.shape[1]
    n_slc = S // SLC_BLOCK
    n = jnp.arange(ncp)[:, None]
    jj = jnp.arange(n_slc)[None, :]
    w = n - (SLC_BLOCK // CMP_STRIDE) * jj + 1
    pool = jnp.where((w >= 0) & (w <= 4), jnp.where((w == 0) | (w == 4), 1.0, 2.0), 0.0).astype(BF)
    return pl.pallas_call(
        functools.partial(_cmp_select_kernel, tq),
        grid=(B, S // tq),
        in_specs=[pl.BlockSpec((None, tq, MIX_W), lambda b, i: (b, i, BLK_NQ // 4)),
                  pl.BlockSpec((None, ncp, HEAD_DIM), lambda b, i: (b, 0, 0)),
                  pl.BlockSpec((None, ncp, HEAD_DIM), lambda b, i: (b, 0, 0)),
                  pl.BlockSpec((ncp, n_slc), lambda b, i: (0, 0))],
        out_specs=[pl.BlockSpec((None, tq, MIX_W), lambda b, i: (b, i, 0)),
                   pl.BlockSpec((None, tq, n_slc), lambda b, i: (b, i, 0))],
        out_shape=[jax.ShapeDtypeStruct((B, S, MIX_W), F32),
                   jax.ShapeDtypeStruct((B, S, n_slc), BF)],
        compiler_params=_cparams(("parallel", "parallel")),
        name="nsa_cmp_select",
    )(proj, kc, vc, pool)


def _slc_attn_kernel(tq, tk, rb, q_ref, k_ref, v_ref, sel_ref, exp_ref, o_ref,
                     qs_ref, m_ref, acc_ref, vx_ref, s_ref, p_ref, bias_ref):
    i, j = pl.program_id(1), pl.program_id(2)

    @pl.when(j == 0)
    def _():
        for h in range(N_HEADS):
            q = q_ref[:, h * HEAD_DIM:(h + 1) * HEAD_DIM] * (HEAD_DIM ** -0.5 * LOG2E)
            qs_ref[h * tq:(h + 1) * tq, :] = q.astype(BF)
        _flash_init(m_ref, acc_ref, vx_ref)

    @pl.when(j * tk <= i * tq + tq - 1)
    def _():
        picked = jnp.dot(sel_ref[...], exp_ref[...], preferred_element_type=F32)
        qpos = i * tq + lax.broadcasted_iota(jnp.int32, (tq, tk), 0)
        kpos = j * tk + lax.broadcasted_iota(jnp.int32, (tq, tk), 1)
        bias_ref[...] = jnp.where(kpos <= qpos, (picked - 1.0) * (-NEG), NEG)
        _flash_step(qs_ref, k_ref, v_ref, vx_ref, s_ref, p_ref, m_ref, acc_ref, rb,
                    lambda r0: bias_ref[(r0 % tq):(r0 % tq) + rb, :])

    @pl.when(j == pl.num_programs(2) - 1)
    def _():
        for h in range(N_HEADS):
            rows = slice(h * tq, (h + 1) * tq)
            o_ref[:, h * HEAD_DIM:(h + 1) * HEAD_DIM] = acc_ref[rows, 0:LANE] / acc_ref[rows, LANE:2 * LANE]


def nsa_selected_attention(proj, sel, tq=256, tk=512, rb=32):
    B, S, _ = proj.shape
    tq, tk = min(tq, S), min(tk, S)
    n_slc = S // SLC_BLOCK
    expand = (jnp.arange(n_slc)[:, None] == (jnp.arange(S)[None, :] // SLC_BLOCK)).astype(BF)
    last = lambda i: (i * tq + tq - 1) // tk
    kv = lambda off: pl.BlockSpec((None, tk, LANE),
                                  lambda b, i, j: (b, jnp.minimum(j, last(i)), BLK_NKV + off))
    rows = N_HEADS * tq
    return pl.pallas_call(
        functools.partial(_slc_attn_kernel, tq, tk, rb),
        grid=(B, S // tq, S // tk),
        in_specs=[pl.BlockSpec((None, tq, MIX_W), lambda b, i, j: (b, i, BLK_NQ // 4)),
                  kv(2), kv(3),
                  pl.BlockSpec((None, tq, n_slc), lambda b, i, j: (b, i, 0)),
                  pl.BlockSpec((n_slc, tk), lambda b, i, j: (0, jnp.minimum(j, last(i))))],
        out_specs=pl.BlockSpec((None, tq, MIX_W), lambda b, i, j: (b, i, 0)),
        out_shape=jax.ShapeDtypeStruct((B, S, MIX_W), F32),
        scratch_shapes=[pltpu.VMEM((rows, LANE), BF),
                        pltpu.VMEM((rows, LANE), F32),
                        pltpu.VMEM((rows, 2 * LANE), F32),
                        pltpu.VMEM((tk, 2 * LANE), BF),
                        pltpu.VMEM((rows, tk), F32),
                        pltpu.VMEM((rows, tk), BF),
                        pltpu.VMEM((tq, tk), F32)],
        compiler_params=_cparams(("parallel", "parallel", "arbitrary")),
        name="nsa_selected",
    )(proj, proj, proj, sel, expand)


def _swa_combine_kernel(tq, q_ref, kp_ref, kc_ref, vp_ref, vc_ref, g_ref, ocmp_ref, oslc_ref, o_ref):
    i = pl.program_id(1)
    scale = HEAD_DIM ** -0.5
    k = jnp.concatenate([kp_ref[...], kc_ref[...]], axis=0).astype(BF)
    v = jnp.concatenate([vp_ref[...], vc_ref[...]], axis=0).astype(BF)
    nk = k.shape[0]
    qpos = i * tq + lax.broadcasted_iota(jnp.int32, (tq, nk), 0)
    kpos = i * tq - SWA_WINDOW + lax.broadcasted_iota(jnp.int32, (tq, nk), 1)
    dist = qpos - kpos
    mask = (dist >= 0) & (dist < SWA_WINDOW) & (kpos >= 0)
    gates = _sigmoid(g_ref[...])
    for h in range(N_HEADS):
        cols = slice(h * HEAD_DIM, (h + 1) * HEAD_DIM)
        s = jnp.where(mask, _dot_nt(q_ref[:, cols] * scale, k), NEG)
        m = jnp.max(s, axis=-1, keepdims=True)
        p = jnp.where(mask, jnp.exp(s - m), 0.0)
        p = p / jnp.maximum(jnp.sum(p, axis=-1, keepdims=True), TINY)
        o_swa = _dot(p, v)
        o = (gates[:, 3 * h:3 * h + 1] * ocmp_ref[:, cols]
             + gates[:, 3 * h + 1:3 * h + 2] * oslc_ref[:, cols]
             + gates[:, 3 * h + 2:3 * h + 3] * o_swa)
        o_ref[:, cols] = o.astype(o_ref.dtype)


def nsa_window_and_combine(proj, o_cmp, o_slc):
    B, S, _ = proj.shape
    tq = SWA_WINDOW
    kvp = lambda off: pl.BlockSpec((None, tq, LANE), lambda b, i: (b, jnp.maximum(i - 1, 0), BLK_NKV + off))
    kvc = lambda off: pl.BlockSpec((None, tq, LANE), lambda b, i: (b, i, BLK_NKV + off))
    full = lambda: pl.BlockSpec((None, tq, MIX_W), lambda b, i: (b, i, 0))
    return pl.pallas_call(
        functools.partial(_swa_combine_kernel, tq),
        grid=(B, S // tq),
        in_specs=[pl.BlockSpec((None, tq, MIX_W), lambda b, i: (b, i, BLK_NQ // 4)),
                  kvp(4), kvc(4), kvp(5), kvc(5),
                  pl.BlockSpec((None, tq, LANE), lambda b, i: (b, i, BLK_NG)),
                  full(), full()],
        out_specs=full(),
        out_shape=jax.ShapeDtypeStruct((B, S, MIX_W), BF),
        compiler_params=_cparams(("parallel", "parallel")),
        name="nsa_window_combine",
    )(proj, proj, proj, proj, proj, proj, o_cmp, o_slc)


def _gelu_tanh(x):
    return 0.5 * x * (1.0 + jnp.tanh(math.sqrt(2.0 / math.pi) * (x + 0.044715 * (x * x * x))))


def _gmlp_kernel(u_ref, v_ref, lnw_ref, lnb_ref, ws_ref, bs_ref, o_ref):
    C = ws_ref.shape[1]
    row = lax.broadcasted_iota(jnp.int32, (C, C), 0)
    col = lax.broadcasted_iota(jnp.int32, (C, C), 1)
    causal = col <= row
    for c in range(u_ref.shape[0] // C):
        rows = slice(c * C, (c + 1) * C)
        u = _gelu_tanh(u_ref[rows, :])
        v = _gelu_tanh(v_ref[rows, :])
        mu = jnp.mean(v, axis=-1, keepdims=True)
        var = jnp.mean(jnp.square(v - mu), axis=-1, keepdims=True)
        v = (v - mu) * lax.rsqrt(var + 1e-5) * lnw_ref[...] + lnb_ref[...]
        for g in range(N_HEADS):
            cols = slice(g * HEAD_DIM, (g + 1) * HEAD_DIM)
            w = jnp.where(causal, ws_ref[g], 0.0)
            vm = _dot(w, v[:, cols]) + bs_ref[:, cols]
            o_ref[rows, cols] = (u[:, cols] * vm).astype(o_ref.dtype)


def chunk_gmlp_mixer(proj, ln_w, ln_b, w_s, b_s, tt=512):
    B, S, _ = proj.shape
    G, C, _ = w_s.shape
    tt = min(tt, S)
    bias = jnp.repeat(b_s.T, HEAD_DIM, axis=1)
    return pl.pallas_call(
        _gmlp_kernel,
        grid=(B, S // tt),
        in_specs=[pl.BlockSpec((None, tt, MIX_W), lambda b, t: (b, t, BLK_GM // 4)),
                  pl.BlockSpec((None, tt, MIX_W), lambda b, t: (b, t, BLK_GM // 4 + 1)),
                  pl.BlockSpec((1, MIX_W), lambda b, t: (0, 0)),
                  pl.BlockSpec((1, MIX_W), lambda b, t: (0, 0)),
                  pl.BlockSpec((G, C, C), lambda b, t: (0, 0, 0)),
                  pl.BlockSpec((C, MIX_W), lambda b, t: (0, 0))],
        out_specs=pl.BlockSpec((None, tt, MIX_W), lambda b, t: (b, t, 0)),
        out_shape=jax.ShapeDtypeStruct((B, S, MIX_W), BF),
        compiler_params=_cparams(("parallel", "parallel")),
        name="chunk_gmlp",
    )(proj, proj, ln_w.reshape(1, MIX_W), ln_b.reshape(1, MIX_W), w_s, bias)


def _merge_kernel(oa_ref, ob_ref, oc_ref, od_ref, ga_ref, gb_ref, gc_ref, gd_ref, wb_ref, wo_ref, x_ref,
                  o_ref, acc_ref):
    j = pl.program_id(1)

    @pl.when(j == 0)
    def _():
        acc_ref[...] = jnp.zeros_like(acc_ref)

    merged = None
    for b, (o_b, g_b) in enumerate(((oa_ref, ga_ref), (ob_ref, gb_ref), (oc_ref, gc_ref), (od_ref, gd_ref))):
        term = _sigmoid(g_b[...]) * jnp.dot(o_b[...], wb_ref[b], preferred_element_type=F32)
        merged = term if merged is None else merged + term
    acc_ref[...] += jnp.dot(merged.astype(BF), wo_ref[...], preferred_element_type=F32)

    @pl.when(j == pl.num_programs(1) - 1)
    def _():
        o_ref[...] = x_ref[...] + acc_ref[...]


def merge_project(x, proj, branches, w_branch, w_out, layer, tm=512, tn=512):
    T, D = x.shape
    tm = min(tm, T)
    per = D // tn
    o_spec = pl.BlockSpec((tm, MIX_W), lambda i, j: (i, 0))
    g_spec = lambda b: pl.BlockSpec((tm, tn), lambda i, j: (i, (BLK_MG * LANE) // tn + b * per + j))
    return pl.pallas_call(
        _merge_kernel,
        grid=(T // tm, per),
        in_specs=[o_spec, o_spec, o_spec, o_spec, g_spec(0), g_spec(1), g_spec(2), g_spec(3),
                  pl.BlockSpec((None, N_BRANCH, MIX_W, tn), lambda i, j: (layer, 0, 0, j)),
                  pl.BlockSpec((None, tn, D), lambda i, j: (layer, j, 0)),
                  pl.BlockSpec((tm, D), lambda i, j: (i, 0))],
        out_specs=pl.BlockSpec((tm, D), lambda i, j: (i, 0)),
        out_shape=jax.ShapeDtypeStruct((T, D), F32),
        scratch_shapes=[pltpu.VMEM((tm, D), F32)],
        compiler_params=_cparams(("parallel", "arbitrary")),
        name="merge_project",
    )(*branches, proj, proj, proj, proj, w_branch, w_out, x)


def _cross_attn_kernel(q_ref, k_ref, v_ref, o_ref):
    dh = q_ref.shape[1] // XA_HEADS
    for h in range(XA_HEADS):
        cols = slice(h * dh, (h + 1) * dh)
        s = _dot_nt(q_ref[:, cols], k_ref[:, cols]) * (dh ** -0.5)
        m = jnp.max(s, axis=-1, keepdims=True)
        p = jnp.exp(s - m)
        p = p / jnp.sum(p, axis=-1, keepdims=True)
        o_ref[:, cols] = _dot(p, v_ref[:, cols]).astype(o_ref.dtype)


def cross_attention_core(q, kv, tq=512):
    B, S, D = q.shape
    M = kv.shape[1]
    tq = min(tq, S)
    return pl.pallas_call(
        _cross_attn_kernel,
        grid=(B, S // tq),
        in_specs=[pl.BlockSpec((None, tq, D), lambda b, i: (b, i, 0)),
                  pl.BlockSpec((None, M, D), lambda b, i: (b, 0, 0)),
                  pl.BlockSpec((None, M, D), lambda b, i: (b, 0, 1))],
        out_specs=pl.BlockSpec((None, tq, D), lambda b, i: (b, i, 0)),
        out_shape=jax.ShapeDtypeStruct((B, S, D), BF),
        compiler_params=_cparams(("parallel", "parallel")),
        name="cross_attention",
    )(q, kv, kv)


def _ffn_kernel(x_ref, nw_ref, wg_ref, wu_ref, wo_ref, o_ref, h_ref, acc_ref):
    j = pl.program_id(1)

    @pl.when(j == 0)
    def _():
        h_ref[...] = _rms(x_ref[...], nw_ref[...], 1e-6).astype(BF)
        acc_ref[...] = jnp.zeros_like(acc_ref)

    h = h_ref[...]
    gate = jnp.dot(h, wg_ref[...], preferred_element_type=F32)
    up = jnp.dot(h, wu_ref[...], preferred_element_type=F32)
    acc_ref[...] += jnp.dot((_silu(gate) * up).astype(BF), wo_ref[...], preferred_element_type=F32)

    @pl.when(j == pl.num_programs(1) - 1)
    def _():
        o_ref[...] = x_ref[...] + acc_ref[...]


def swiglu_ffn(x, norm_w, w_in, w_out, layer, tm=512, tf=512):
    T, D = x.shape
    F = w_out.shape[1]
    tm = min(tm, T)
    nf = F // tf
    return pl.pallas_call(
        _ffn_kernel,
        grid=(T // tm, nf),
        in_specs=[pl.BlockSpec((tm, D), lambda i, j: (i, 0)),
                  pl.BlockSpec((1, D), lambda i, j: (0, 0)),
                  pl.BlockSpec((None, D, tf), lambda i, j: (layer, 0, j)),
                  pl.BlockSpec((None, D, tf), lambda i, j: (layer, 0, nf + j)),
                  pl.BlockSpec((None, tf, D), lambda i, j: (layer, j, 0))],
        out_specs=pl.BlockSpec((tm, D), lambda i, j: (i, 0)),
        out_shape=jax.ShapeDtypeStruct((T, D), F32),
        scratch_shapes=[pltpu.VMEM((tm, D), BF), pltpu.VMEM((tm, D), F32)],
        compiler_params=_cparams(("parallel", "arbitrary")),
        name="swiglu_ffn",
    )(x, norm_w.reshape(1, D), w_in, w_in, w_out)


def _pack_w_in(w):
    n_gate = 3 * N_HEADS
    gate_end = BLK_NG * LANE + n_gate
    wt = jnp.swapaxes(w, 1, 2)
    pad = jnp.zeros((w.shape[0], BLK_GM * LANE - gate_end, w.shape[1]), w.dtype)
    packed = jnp.concatenate([wt[:, :gate_end], pad, wt[:, gate_end:]], axis=1)
    assert packed.shape[1] == N_PROJ, packed.shape
    return packed.astype(BF)


def kernel(x, mem, mem_norm_w, mix_norm_w, w_in, hg_lower_bounds, hg_norm_w, da_lambda_q1, da_lambda_k1, da_lambda_q2, da_lambda_k2, da_norm_w, nsa_pos_k, nsa_cmp_w1_k, nsa_cmp_w2_k, nsa_pos_v, nsa_cmp_w1_v, nsa_cmp_w2_v, gm_ln_w, gm_ln_b, gm_w_s, gm_b_s, w_branch, w_out, xa_norm_w, xa_w_q, xa_w_kv, xa_w_o, ffn_norm_w, ffn_w_in, ffn_w_out, final_norm_w):
    B, S, D = x.shape
    M = mem.shape[1]
    depth = w_in.shape[0]
    assert D == BLK_MG * LANE // 3 and S % SWA_WINDOW == 0, (S, D)
    T = B * S
    x = x.reshape(T, D)
    mem2 = mem.reshape(B * M, D)
    w_in_p = _pack_w_in(w_in)
    w_branch_b, w_out_b = w_branch.astype(BF), w_out.astype(BF)
    xa_w_q_b, xa_w_kv_b, xa_w_o_b = xa_w_q.astype(BF), xa_w_kv.astype(BF), xa_w_o.astype(BF)
    ffn_w_in_b, ffn_w_out_b = ffn_w_in.astype(BF), ffn_w_out.astype(BF)
    for l in range(depth):
        lambda_init = 0.8 - 0.6 * math.exp(-0.3 * l)
        proj = norm_matmul(x, mix_norm_w[l], w_in_p, l, F32, tm=1024, w_is_transposed=True).reshape(B, S, N_PROJ)
        o_a = hgrn2_mixer(proj, hg_lower_bounds, hg_norm_w[l], l)
        lam_params = jnp.stack([da_lambda_q1[l], da_lambda_k1[l], da_lambda_q2[l], da_lambda_k2[l]])
        o_b = diff_attention_mixer(proj, lam_params, da_norm_w[l], lambda_init)
        nkv = lambda off: proj[:, :, (BLK_NKV + off) * LANE:(BLK_NKV + off + 1) * LANE]
        kc = compress_tokens(nkv(0), nsa_pos_k[l], nsa_cmp_w1_k[l].astype(BF), nsa_cmp_w2_k[l].astype(BF))
        vc = compress_tokens(nkv(1), nsa_pos_v[l], nsa_cmp_w1_v[l].astype(BF), nsa_cmp_w2_v[l].astype(BF))
        o_cmp, sel = nsa_compressed_and_select(proj, kc, vc)
        o_slc = nsa_selected_attention(proj, sel)
        o_c = nsa_window_and_combine(proj, o_cmp, o_slc)
        o_d = chunk_gmlp_mixer(proj, gm_ln_w[l], gm_ln_b[l], gm_w_s[l], gm_b_s[l])
        branches = [o.reshape(T, MIX_W) for o in (o_a, o_b, o_c, o_d)]
        x = merge_project(x, proj.reshape(T, N_PROJ), branches, w_branch_b, w_out_b, l)
        q = norm_matmul(x, xa_norm_w[l], xa_w_q_b, l, BF, tm=1024).reshape(B, S, D)
        kv = norm_matmul(mem2, mem_norm_w, xa_w_kv_b, l, F32).reshape(B, M, 2 * D)
        att = cross_attention_core(q, kv).reshape(T, D)
        x = matmul_residual(att, xa_w_o_b, l, x, tm=1024)
        x = swiglu_ffn(x, ffn_norm_w[l], ffn_w_in_b, ffn_w_out_b, l)
    return rmsnorm(x, final_norm_w).reshape(B, S, D)
```

```python
import functools
import math

import jax
import jax.numpy as jnp
from jax import lax
from jax.experimental import pallas as pl
from jax.experimental.pallas import tpu as pltpu

BF = jnp.bfloat16
F32 = jnp.float32

LANE = 128
HEAD_DIM = 128
N_HEADS = 4
MIX_W = N_HEADS * HEAD_DIM
N_BRANCH = 4
HG_CHUNK = 64
HG_SUB = 16
HG_EXP_CLAMP = 80.0
CMP_LEN = 32
CMP_STRIDE = 16
SLC_BLOCK = 64
SLC_TOPK = 16
N_LOCAL_BLOCKS = 2
SWA_WINDOW = 512
XA_HEADS = 4
NEG = -1e30
TINY = 1e-30
FORCE_SCORE = 1e4
LOG2E = math.log2(math.e)
FLASH_MM_ROWS = 256
VMEM_LIMIT = 56 * 1024 * 1024

BLK_HG = 0
BLK_DA = 16
BLK_NQ = 28
BLK_NKV = 32
BLK_NG = 38
BLK_GM = 40
BLK_MG = 48
N_PROJ = 112 * LANE


def _cparams(sem):
    return pltpu.CompilerParams(dimension_semantics=sem, vmem_limit_bytes=VMEM_LIMIT)


def _dot(a, b):
    return jnp.dot(a.astype(BF), b.astype(BF), preferred_element_type=F32)


def _dot_nt(a, b):
    return lax.dot_general(a.astype(BF), b.astype(BF), (((1,), (1,)), ((), ())),
                           preferred_element_type=F32)


def _dot_tn(a, b):
    return lax.dot_general(a.astype(BF), b.astype(BF), (((0,), (0,)), ((), ())),
                           preferred_element_type=F32)


def _split3(a):
    hi = a.astype(BF)
    r = a - hi.astype(F32)
    mid = r.astype(BF)
    lo = (r - mid.astype(F32)).astype(BF)
    return hi, mid, lo


def _dot_exact_rhs(a_bf, b):
    hi, mid, lo = _split3(b)
    f = lambda t: jnp.dot(a_bf, t, preferred_element_type=F32)
    return f(hi) + f(mid) + f(lo)


def _dot_exact_lhs(a, b_bf):
    hi, mid, lo = _split3(a)
    f = lambda t: jnp.dot(t, b_bf, preferred_element_type=F32)
    return f(hi) + f(mid) + f(lo)


def _rms(x, w, eps):
    return x * lax.rsqrt(jnp.mean(x * x, axis=-1, keepdims=True) + eps) * w


def _sigmoid(x):
    return 1.0 / (1.0 + jnp.exp(-x))


def _silu(x):
    return x * _sigmoid(x)


def _norm_matmul_kernel(w_is_transposed, x_ref, nw_ref, w_ref, o_ref, h_ref):
    @pl.when(pl.program_id(1) == 0)
    def _():
        h_ref[...] = _rms(x_ref[...], nw_ref[...], 1e-6).astype(BF)

    if w_is_transposed:
        o = lax.dot_general(h_ref[...], w_ref[...], (((1,), (1,)), ((), ())), preferred_element_type=F32)
    else:
        o = jnp.dot(h_ref[...], w_ref[...], preferred_element_type=F32)
    o_ref[...] = o.astype(o_ref.dtype)


def norm_matmul(x, norm_w, w, layer, out_dtype, tm=512, tn=1024, w_is_transposed=False):
    T, D = x.shape
    N = w.shape[1] if w_is_transposed else w.shape[2]
    tm, tn = min(tm, T), min(tn, N)
    w_spec = (pl.BlockSpec((None, tn, D), lambda i, j: (layer, j, 0)) if w_is_transposed
              else pl.BlockSpec((None, D, tn), lambda i, j: (layer, 0, j)))
    return pl.pallas_call(
        functools.partial(_norm_matmul_kernel, w_is_transposed),
        grid=(T // tm, N // tn),
        in_specs=[pl.BlockSpec((tm, D), lambda i, j: (i, 0)),
                  pl.BlockSpec((1, D), lambda i, j: (0, 0)),
                  w_spec],
        out_specs=pl.BlockSpec((tm, tn), lambda i, j: (i, j)),
        out_shape=jax.ShapeDtypeStruct((T, N), out_dtype),
        scratch_shapes=[pltpu.VMEM((tm, D), BF)],
        compiler_params=_cparams(("parallel", "arbitrary")),
        name="norm_matmul",
    )(x, norm_w.reshape(1, D), w)


def _matmul_residual_kernel(a_ref, w_ref, r_ref, o_ref):
    o_ref[...] = r_ref[...] + jnp.dot(a_ref[...], w_ref[...], preferred_element_type=F32)


def matmul_residual(a, w, layer, r, tm=512, tn=1024):
    T, K = a.shape
    N = w.shape[2]
    tm, tn = min(tm, T), min(tn, N)
    return pl.pallas_call(
        _matmul_residual_kernel,
        grid=(T // tm, N // tn),
        in_specs=[pl.BlockSpec((tm, K), lambda i, j: (i, 0)),
                  pl.BlockSpec((None, K, tn), lambda i, j: (layer, 0, j)),
                  pl.BlockSpec((tm, tn), lambda i, j: (i, j))],
        out_specs=pl.BlockSpec((tm, tn), lambda i, j: (i, j)),
        out_shape=jax.ShapeDtypeStruct((T, N), F32),
        compiler_params=_cparams(("parallel", "arbitrary")),
        name="matmul_residual",
    )(a, w, r)


def _rmsnorm_kernel(x_ref, w_ref, o_ref):
    o_ref[...] = _rms(x_ref[...], w_ref[...], 1e-6)


def rmsnorm(x, w, tm=512):
    T, D = x.shape
    tm = min(tm, T)
    return pl.pallas_call(
        _rmsnorm_kernel,
        grid=(T // tm,),
        in_specs=[pl.BlockSpec((tm, D), lambda i: (i, 0)),
                  pl.BlockSpec((1, D), lambda i: (0, 0))],
        out_specs=pl.BlockSpec((tm, D), lambda i: (i, 0)),
        out_shape=jax.ShapeDtypeStruct((T, D), F32),
        compiler_params=_cparams(("parallel",)),
        name="final_rmsnorm",
    )(x, w.reshape(1, D))


def _hgrn2_kernel(layer, q_ref, f_ref, i_ref, g_ref, lb_ref, nw_ref, o_ref, state_ref):
    @pl.when(pl.program_id(1) == 0)
    def _():
        state_ref[...] = jnp.zeros_like(state_ref)

    lb_all = lb_ref[...]
    e = jnp.exp(lb_all - jnp.max(lb_all, axis=0, keepdims=True))
    soft = e / jnp.sum(e, axis=0, keepdims=True)
    lrow = lax.broadcasted_iota(jnp.int32, soft.shape, 0)
    lower_all = jnp.sum(jnp.where((lrow >= 1) & (lrow <= layer), soft, 0.0), axis=0, keepdims=True)

    C, SUB = HG_CHUNK, HG_SUB
    tt = q_ref.shape[0]
    row = lax.broadcasted_iota(jnp.int32, (C, C), 0)
    col = lax.broadcasted_iota(jnp.int32, (C, C), 1)
    tri = jnp.where(col <= row, 1.0, 0.0).astype(BF)
    tri_sub = jnp.where(col < (row // SUB) * SUB, 1.0, 0.0).astype(BF)
    tri2 = jnp.concatenate([tri, tri_sub], axis=0)
    nsub = C // SUB
    mrow = lax.broadcasted_iota(jnp.int32, (C, nsub * C), 0)
    mcol = lax.broadcasted_iota(jnp.int32, (C, nsub * C), 1)
    keep = jnp.where(mcol // C == mrow // SUB, mcol % C, C) <= mrow
    w = MIX_W

    def chunk(c, states):
        rows = slice(c * C, (c + 1) * C)
        f_all = lower_all + (1.0 - lower_all) * _sigmoid(f_ref[rows, :])
        log_f = jnp.log(f_all)
        hi, mid, lo = _split3(log_f)
        cum = jnp.dot(tri2, jnp.concatenate([hi, mid, lo], axis=1), preferred_element_type=F32)
        cum = cum[:, 0:w] + cum[:, w:2 * w] + cum[:, 2 * w:3 * w]
        b_all, b_sub_all = cum[0:C], cum[C:2 * C]
        b_end_all = jnp.sum(log_f, axis=0, keepdims=True)
        new_states = []
        for h in range(N_HEADS):
            cols = slice(h * HEAD_DIM, (h + 1) * HEAD_DIM)
            b, b_sub, b_end = b_all[:, cols], b_sub_all[:, cols], b_end_all[:, cols]
            q = _silu(q_ref[rows, cols])
            k = 1.0 - f_all[:, cols]
            v = i_ref[rows, cols]
            o_inter = _dot_nt(q * jnp.exp(b), states[h])
            qt = q * jnp.exp(b - b_sub)
            kt = jnp.concatenate(
                [k * jnp.exp(jnp.minimum(b_sub[s * SUB:s * SUB + 1] - b, HG_EXP_CLAMP)) for s in range(nsub)],
                axis=0)
            sc = jnp.where(keep, _dot_nt(qt, kt), 0.0)
            o = o_inter + _dot(sc, jnp.concatenate([v] * nsub, axis=0))
            o = _rms(o, nw_ref[...], 1e-6) * _silu(g_ref[rows, cols])
            o_ref[rows, cols] = o.astype(o_ref.dtype)
            new_states.append(states[h] * jnp.exp(b_end) + _dot_tn(v, k * jnp.exp(b_end - b)))
        return new_states

    states = [state_ref[h] for h in range(N_HEADS)]
    for c in range(tt // C):
        states = chunk(c, states)
    for h in range(N_HEADS):
        state_ref[h] = states[h]


def hgrn2_mixer(proj, lower_bounds, norm_w, layer, tt=256):
    B, S, _ = proj.shape
    L = lower_bounds.shape[0]
    tt = min(tt, S)
    seg = lambda off: pl.BlockSpec((None, tt, MIX_W), lambda b, t: (b, t, (BLK_HG + off) // 4))
    return pl.pallas_call(
        functools.partial(_hgrn2_kernel, layer),
        grid=(B, S // tt),
        in_specs=[seg(0), seg(4), seg(8), seg(12),
                  pl.BlockSpec((L, MIX_W), lambda b, t: (0, 0)),
                  pl.BlockSpec((1, LANE), lambda b, t: (0, 0))],
        out_specs=pl.BlockSpec((None, tt, MIX_W), lambda b, t: (b, t, 0)),
        out_shape=jax.ShapeDtypeStruct((B, S, MIX_W), BF),
        scratch_shapes=[pltpu.VMEM((N_HEADS, HEAD_DIM, HEAD_DIM), F32)],
        compiler_params=_cparams(("parallel", "arbitrary")),
        name="hgrn2",
    )(proj, proj, proj, proj, lower_bounds, norm_w.reshape(1, LANE))


def _flash_init(m_ref, acc_ref, vx_ref):
    m_ref[...] = jnp.full_like(m_ref, NEG)
    acc_ref[...] = jnp.zeros_like(acc_ref)
    vx_ref[:, LANE:2 * LANE] = jnp.ones((vx_ref.shape[0], LANE), BF)


def _flash_step(qs_ref, k_ref, v_ref, vx_ref, s_ref, p_ref, m_ref, acc_ref, rb, bias_fn):
    nrows, tk = s_ref.shape
    k = k_ref[...].astype(BF)
    vx_ref[:, 0:LANE] = v_ref[...].astype(BF)
    for c in range(nrows // FLASH_MM_ROWS):
        rows = slice(c * FLASH_MM_ROWS, (c + 1) * FLASH_MM_ROWS)
        s_ref[rows, :] = _dot_nt(qs_ref[rows, :], k)
    for r in range(nrows // rb):
        rows = slice(r * rb, (r + 1) * rb)
        s = s_ref[rows, :]
        if bias_fn is not None:
            s = s + bias_fn(r * rb)
        m_prev = m_ref[rows, :]
        m_new = jnp.maximum(m_prev, jnp.max(s, axis=-1, keepdims=True))
        p_ref[rows, :] = jnp.exp2(s - jnp.concatenate([m_new] * (tk // LANE), axis=1)).astype(BF)
        alpha = jnp.exp2(m_prev - m_new)
        acc_ref[rows, :] = acc_ref[rows, :] * jnp.concatenate([alpha, alpha], axis=1)
        m_ref[rows, :] = m_new
    vx = vx_ref[...]
    for c in range(nrows // FLASH_MM_ROWS):
        rows = slice(c * FLASH_MM_ROWS, (c + 1) * FLASH_MM_ROWS)
        acc_ref[rows, :] += jnp.dot(p_ref[rows, :], vx, preferred_element_type=F32)


def _diff_attn_kernel(lambda_init, tq, tk, rb, q_ref, k_ref, v_ref, lam_ref, nw_ref, o_ref,
                      qs_ref, m_ref, acc_ref, vx_ref, s_ref, p_ref):
    i, j = pl.program_id(2), pl.program_id(3)
    dc = HEAD_DIM // 2

    @pl.when(j == 0)
    def _():
        q = q_ref[...] * (dc ** -0.5 * LOG2E)
        first = lax.broadcasted_iota(jnp.int32, q.shape, 1) < dc
        qs_ref[0:tq, :] = jnp.where(first, q, 0.0).astype(BF)
        qs_ref[tq:2 * tq, :] = jnp.where(first, 0.0, q).astype(BF)
        _flash_init(m_ref, acc_ref, vx_ref)

    def causal_bias(r0):
        r = r0 + lax.broadcasted_iota(jnp.int32, (rb, tk), 0)
        qloc = jnp.where(r >= tq, r - tq, r)
        col = lax.broadcasted_iota(jnp.int32, (rb, tk), 1)
        return jnp.where(col - qloc <= i * tq - j * tk, 0.0, NEG)

    needed = j * tk <= i * tq + tq - 1
    unmasked = (j + 1) * tk - 1 <= i * tq

    @pl.when(needed & unmasked)
    def _():
        _flash_step(qs_ref, k_ref, v_ref, vx_ref, s_ref, p_ref, m_ref, acc_ref, rb, None)

    @pl.when(needed & jnp.logical_not(unmasked))
    def _():
        _flash_step(qs_ref, k_ref, v_ref, vx_ref, s_ref, p_ref, m_ref, acc_ref, rb, causal_bias)

    @pl.when(j == pl.num_programs(3) - 1)
    def _():
        lam = (jnp.exp(jnp.sum(lam_ref[0:1] * lam_ref[1:2], axis=-1, keepdims=True))
               - jnp.exp(jnp.sum(lam_ref[2:3] * lam_ref[3:4], axis=-1, keepdims=True)) + lambda_init)
        o = acc_ref[:, 0:LANE] / acc_ref[:, LANE:2 * LANE]
        o = o[0:tq] - lam * o[tq:2 * tq]
        o = _rms(o, nw_ref[...], 1e-6) * (1.0 - lambda_init)
        o_ref[...] = o.astype(o_ref.dtype)


def diff_attention_mixer(proj, lam_params, norm_w, lambda_init, tq=1024, tk=1024, rb=16):
    B, S, _ = proj.shape
    tq, tk = min(tq, S), min(tk, S)
    last = lambda i: (i * tq + tq - 1) // tk
    return pl.pallas_call(
        functools.partial(_diff_attn_kernel, lambda_init, tq, tk, rb),
        grid=(B, N_HEADS, S // tq, S // tk),
        in_specs=[pl.BlockSpec((None, tq, LANE), lambda b, h, i, j: (b, i, BLK_DA + h)),
                  pl.BlockSpec((None, tk, LANE),
                               lambda b, h, i, j: (b, jnp.minimum(j, last(i)), BLK_DA + 4 + h)),
                  pl.BlockSpec((None, tk, LANE),
                               lambda b, h, i, j: (b, jnp.minimum(j, last(i)), BLK_DA + 8 + h)),
                  pl.BlockSpec((4, HEAD_DIM // 2), lambda b, h, i, j: (0, 0)),
                  pl.BlockSpec((1, LANE), lambda b, h, i, j: (0, 0))],
        out_specs=pl.BlockSpec((None, tq, LANE), lambda b, h, i, j: (b, i, h)),
        out_shape=jax.ShapeDtypeStruct((B, S, MIX_W), BF),
        scratch_shapes=[pltpu.VMEM((2 * tq, LANE), BF),
                        pltpu.VMEM((2 * tq, LANE), F32),
                        pltpu.VMEM((2 * tq, 2 * LANE), F32),
                        pltpu.VMEM((tk, 2 * LANE), BF),
                        pltpu.VMEM((2 * tq, tk), F32),
                        pltpu.VMEM((2 * tq, tk), BF)],
        compiler_params=_cparams(("parallel", "parallel", "parallel", "arbitrary")),
        name="diff_attention",
    )(proj, proj, proj, lam_params, norm_w.reshape(1, LANE))


def _compress_kernel(t_ref, pos_ref, w1_ref, w2_ref, o_ref):
    half = t_ref.shape[1]
    t = t_ref[...]
    a = _dot(t, w1_ref[0:half, :])
    bnext = _dot(t, w1_ref[half:2 * half, :])
    c0 = _dot(pos_ref[...], w1_ref[...])
    n = a.shape[0]
    hidden = a + pltpu.roll(bnext, n - 1, 0) + c0
    o_ref[...] = _dot(_silu(hidden), w2_ref[...])


def compress_tokens(t, pos, w1, w2):
    B, S, d = t.shape
    n_chunks = S // CMP_STRIDE
    hid = w1.shape[1]
    t = t.reshape(B, n_chunks, CMP_STRIDE * d)
    return pl.pallas_call(
        _compress_kernel,
        grid=(B,),
        in_specs=[pl.BlockSpec((None, n_chunks, CMP_STRIDE * d), lambda b: (b, 0, 0)),
                  pl.BlockSpec((1, CMP_LEN * d), lambda b: (0, 0)),
                  pl.BlockSpec((CMP_LEN * d, hid), lambda b: (0, 0)),
                  pl.BlockSpec((hid, d), lambda b: (0, 0))],
        out_specs=pl.BlockSpec((None, n_chunks, d), lambda b: (b, 0, 0)),
        out_shape=jax.ShapeDtypeStruct((B, n_chunks, d), F32),
        compiler_params=_cparams(("parallel",)),
        name="nsa_compress",
    )(t, pos.reshape(1, CMP_LEN * d), w1, w2)


def _cmp_select_kernel(tq, q_ref, kc_ref, vc_ref, pool_ref, o_ref, sel_ref):
    i = pl.program_id(1)
    ncp = kc_ref.shape[0]
    n_slc = sel_ref.shape[1]
    scale = HEAD_DIM ** -0.5
    qpos = i * tq + lax.broadcasted_iota(jnp.int32, (tq, 1), 0)
    cmp_end = lax.broadcasted_iota(jnp.int32, (1, ncp), 1) * CMP_STRIDE + (CMP_LEN - 1)
    mask = cmp_end <= qpos
    kc = kc_ref[...].astype(BF)
    vc = vc_ref[...].astype(BF)
    imp = jnp.zeros((tq, ncp), F32)
    for h in range(N_HEADS):
        cols = slice(h * HEAD_DIM, (h + 1) * HEAD_DIM)
        s = _dot_nt(q_ref[:, cols], kc) * scale
        s = jnp.where(mask, s, NEG)
        m = jnp.max(s, axis=-1, keepdims=True)
        p = jnp.where(mask, jnp.exp(s - m), 0.0)
        p = p / jnp.maximum(jnp.sum(p, axis=-1, keepdims=True), TINY)
        o_ref[:, cols] = _dot(p, vc)
        imp = imp + p
    imp_slc = _dot_exact_lhs(imp, pool_ref[...])
    blk = lax.broadcasted_iota(jnp.int32, (1, n_slc), 1)
    cur = qpos // SLC_BLOCK
    valid = blk <= cur
    forced = (blk == 0) | (blk > cur - N_LOCAL_BLOCKS)
    score = jnp.where(valid, jnp.where(forced, FORCE_SCORE, imp_slc), -1.0)
    sel = jnp.zeros((tq, n_slc), F32)
    for _ in range(min(SLC_TOPK, n_slc)):
        idx = jnp.argmax(score, axis=-1, keepdims=True)
        pick = blk == idx
        sel = jnp.where(pick, 1.0, sel)
        score = jnp.where(pick, -2.0, score)
    sel_ref[...] = sel.astype(sel_ref.dtype)


def nsa_compressed_and_select(proj, kc, vc, tq=256):
    B, S, _ = proj.shape
    ncp = kc.shape[1]
    n_slc = S // SLC_BLOCK
    n = jnp.arange(ncp)[:, None]
    jj = jnp.arange(n_slc)[None, :]
    w = n - (SLC_BLOCK // CMP_STRIDE) * jj + 1
    pool = jnp.where((w >= 0) & (w <= 4), jnp.where((w == 0) | (w == 4), 1.0, 2.0), 0.0).astype(BF)
    return pl.pallas_call(
        functools.partial(_cmp_select_kernel, tq),
        grid=(B, S // tq),
        in_specs=[pl.BlockSpec((None, tq, MIX_W), lambda b, i: (b, i, BLK_NQ // 4)),
                  pl.BlockSpec((None, ncp, HEAD_DIM), lambda b, i: (b, 0, 0)),
                  pl.BlockSpec((None, ncp, HEAD_DIM), lambda b, i: (b, 0, 0)),
                  pl.BlockSpec((ncp, n_slc), lambda b, i: (0, 0))],
        out_specs=[pl.BlockSpec((None, tq, MIX_W), lambda b, i: (b, i, 0)),
                   pl.BlockSpec((None, tq, n_slc), lambda b, i: (b, i, 0))],
        out_shape=[jax.ShapeDtypeStruct((B, S, MIX_W), F32),
                   jax.ShapeDtypeStruct((B, S, n_slc), BF)],
        compiler_params=_cparams(("parallel", "parallel")),
        name="nsa_cmp_select",
    )(proj, kc, vc, pool)


def _slc_attn_kernel(tq, tk, rb, q_ref, k_ref, v_ref, sel_ref, exp_ref, o_ref,
                     qs_ref, m_ref, acc_ref, vx_ref, s_ref, p_ref, bias_ref):
    i, j = pl.program_id(1), pl.program_id(2)

    @pl.when(j == 0)
    def _():
        for h in range(N_HEADS):
            q = q_ref[:, h * HEAD_DIM:(h + 1) * HEAD_DIM] * (HEAD_DIM ** -0.5 * LOG2E)
            qs_ref[h * tq:(h + 1) * tq, :] = q.astype(BF)
        _flash_init(m_ref, acc_ref, vx_ref)

    @pl.when(j * tk <= i * tq + tq - 1)
    def _():
        picked = jnp.dot(sel_ref[...], exp_ref[...], preferred_element_type=F32)
        qpos = i * tq + lax.broadcasted_iota(jnp.int32, (tq, tk), 0)
        kpos = j * tk + lax.broadcasted_iota(jnp.int32, (tq, tk), 1)
        bias_ref[...] = jnp.where(kpos <= qpos, (picked - 1.0) * (-NEG), NEG)
        _flash_step(qs_ref, k_ref, v_ref, vx_ref, s_ref, p_ref, m_ref, acc_ref, rb,
                    lambda r0: bias_ref[(r0 % tq):(r0 % tq) + rb, :])

    @pl.when(j == pl.num_programs(2) - 1)
    def _():
        for h in range(N_HEADS):
            rows = slice(h * tq, (h + 1) * tq)
            o_ref[:, h * HEAD_DIM:(h + 1) * HEAD_DIM] = acc_ref[rows, 0:LANE] / acc_ref[rows, LANE:2 * LANE]


def nsa_selected_attention(proj, sel, tq=512, tk=1024, rb=16):
    B, S, _ = proj.shape
    tq, tk = min(tq, S), min(tk, S)
    n_slc = S // SLC_BLOCK
    expand = (jnp.arange(n_slc)[:, None] == (jnp.arange(S)[None, :] // SLC_BLOCK)).astype(BF)
    last = lambda i: (i * tq + tq - 1) // tk
    kv = lambda off: pl.BlockSpec((None, tk, LANE),
                                  lambda b, i, j: (b, jnp.minimum(j, last(i)), BLK_NKV + off))
    rows = N_HEADS * tq
    return pl.pallas_call(
        functools.partial(_slc_attn_kernel, tq, tk, rb),
        grid=(B, S // tq, S // tk),
        in_specs=[pl.BlockSpec((None, tq, MIX_W), lambda b, i, j: (b, i, BLK_NQ // 4)),
                  kv(2), kv(3),
                  pl.BlockSpec((None, tq, n_slc), lambda b, i, j: (b, i, 0)),
                  pl.BlockSpec((n_slc, tk), lambda b, i, j: (0, jnp.minimum(j, last(i))))],
        out_specs=pl.BlockSpec((None, tq, MIX_W), lambda b, i, j: (b, i, 0)),
        out_shape=jax.ShapeDtypeStruct((B, S, MIX_W), F32),
        scratch_shapes=[pltpu.VMEM((rows, LANE), BF),
                        pltpu.VMEM((rows, LANE), F32),
                        pltpu.VMEM((rows, 2 * LANE), F32),
                        pltpu.VMEM((tk, 2 * LANE), BF),
                        pltpu.VMEM((rows, tk), F32),
                        pltpu.VMEM((rows, tk), BF),
                        pltpu.VMEM((tq, tk), F32)],
        compiler_params=_cparams(("parallel", "parallel", "arbitrary")),
        name="nsa_selected",
    )(proj, proj, proj, sel, expand)


def _swa_combine_kernel(tq, q_ref, kp_ref, kc_ref, vp_ref, vc_ref, g_ref, ocmp_ref, oslc_ref, o_ref):
    i = pl.program_id(1)
    scale = HEAD_DIM ** -0.5
    k = jnp.concatenate([kp_ref[...], kc_ref[...]], axis=0).astype(BF)
    v = jnp.concatenate([vp_ref[...], vc_ref[...]], axis=0).astype(BF)
    nk = k.shape[0]
    qpos = i * tq + lax.broadcasted_iota(jnp.int32, (tq, nk), 0)
    kpos = i * tq - SWA_WINDOW + lax.broadcasted_iota(jnp.int32, (tq, nk), 1)
    dist = qpos - kpos
    mask = (dist >= 0) & (dist < SWA_WINDOW) & (kpos >= 0)
    gates = _sigmoid(g_ref[...])
    for h in range(N_HEADS):
        cols = slice(h * HEAD_DIM, (h + 1) * HEAD_DIM)
        s = jnp.where(mask, _dot_nt(q_ref[:, cols] * scale, k), NEG)
        m = jnp.max(s, axis=-1, keepdims=True)
        p = jnp.where(mask, jnp.exp(s - m), 0.0)
        p = p / jnp.maximum(jnp.sum(p, axis=-1, keepdims=True), TINY)
        o_swa = _dot(p, v)
        o = (gates[:, 3 * h:3 * h + 1] * ocmp_ref[:, cols]
             + gates[:, 3 * h + 1:3 * h + 2] * oslc_ref[:, cols]
             + gates[:, 3 * h + 2:3 * h + 3] * o_swa)
        o_ref[:, cols] = o.astype(o_ref.dtype)


def nsa_window_and_combine(proj, o_cmp, o_slc):
    B, S, _ = proj.shape
    tq = SWA_WINDOW
    kvp = lambda off: pl.BlockSpec((None, tq, LANE), lambda b, i: (b, jnp.maximum(i - 1, 0), BLK_NKV + off))
    kvc = lambda off: pl.BlockSpec((None, tq, LANE), lambda b, i: (b, i, BLK_NKV + off))
    full = lambda: pl.BlockSpec((None, tq, MIX_W), lambda b, i: (b, i, 0))
    return pl.pallas_call(
        functools.partial(_swa_combine_kernel, tq),
        grid=(B, S // tq),
        in_specs=[pl.BlockSpec((None, tq, MIX_W), lambda b, i: (b, i, BLK_NQ // 4)),
                  kvp(4), kvc(4), kvp(5), kvc(5),
                  pl.BlockSpec((None, tq, LANE), lambda b, i: (b, i, BLK_NG)),
                  full(), full()],
        out_specs=full(),
        out_shape=jax.ShapeDtypeStruct((B, S, MIX_W), BF),
        compiler_params=_cparams(("parallel", "parallel")),
        name="nsa_window_combine",
    )(proj, proj, proj, proj, proj, proj, o_cmp, o_slc)


def _gelu_tanh(x):
    return 0.5 * x * (1.0 + jnp.tanh(math.sqrt(2.0 / math.pi) * (x + 0.044715 * (x * x * x))))


def _gmlp_kernel(u_ref, v_ref, lnw_ref, lnb_ref, ws_ref, bs_ref, o_ref):
    C = ws_ref.shape[1]
    row = lax.broadcasted_iota(jnp.int32, (C, C), 0)
    col = lax.broadcasted_iota(jnp.int32, (C, C), 1)
    causal = col <= row
    for c in range(u_ref.shape[0] // C):
        rows = slice(c * C, (c + 1) * C)
        u = _gelu_tanh(u_ref[rows, :])
        v = _gelu_tanh(v_ref[rows, :])
        mu = jnp.mean(v, axis=-1, keepdims=True)
        var = jnp.mean(jnp.square(v - mu), axis=-1, keepdims=True)
        v = (v - mu) * lax.rsqrt(var + 1e-5) * lnw_ref[...] + lnb_ref[...]
        for g in range(N_HEADS):
            cols = slice(g * HEAD_DIM, (g + 1) * HEAD_DIM)
            w = jnp.where(causal, ws_ref[g], 0.0)
            vm = _dot(w, v[:, cols]) + bs_ref[:, cols]
            o_ref[rows, cols] = (u[:, cols] * vm).astype(o_ref.dtype)


def chunk_gmlp_mixer(proj, ln_w, ln_b, w_s, b_s, tt=512):
    B, S, _ = proj.shape
    G, C, _ = w_s.shape
    tt = min(tt, S)
    bias = jnp.repeat(b_s.T, HEAD_DIM, axis=1)
    return pl.pallas_call(
        _gmlp_kernel,
        grid=(B, S // tt),
        in_specs=[pl.BlockSpec((None, tt, MIX_W), lambda b, t: (b, t, BLK_GM // 4)),
                  pl.BlockSpec((None, tt, MIX_W), lambda b, t: (b, t, BLK_GM // 4 + 1)),
                  pl.BlockSpec((1, MIX_W), lambda b, t: (0, 0)),
                  pl.BlockSpec((1, MIX_W), lambda b, t: (0, 0)),
                  pl.BlockSpec((G, C, C), lambda b, t: (0, 0, 0)),
                  pl.BlockSpec((C, MIX_W), lambda b, t: (0, 0))],
        out_specs=pl.BlockSpec((None, tt, MIX_W), lambda b, t: (b, t, 0)),
        out_shape=jax.ShapeDtypeStruct((B, S, MIX_W), BF),
        compiler_params=_cparams(("parallel", "parallel")),
        name="chunk_gmlp",
    )(proj, proj, ln_w.reshape(1, MIX_W), ln_b.reshape(1, MIX_W), w_s, bias)


def _merge_kernel(oa_ref, ob_ref, oc_ref, od_ref, ga_ref, gb_ref, gc_ref, gd_ref, wb_ref, wo_ref, x_ref,
                  o_ref, acc_ref):
    j = pl.program_id(1)

    @pl.when(j == 0)
    def _():
        acc_ref[...] = jnp.zeros_like(acc_ref)

    merged = None
    for b, (o_b, g_b) in enumerate(((oa_ref, ga_ref), (ob_ref, gb_ref), (oc_ref, gc_ref), (od_ref, gd_ref))):
        term = _sigmoid(g_b[...]) * jnp.dot(o_b[...], wb_ref[b], preferred_element_type=F32)
        merged = term if merged is None else merged + term
    acc_ref[...] += jnp.dot(merged.astype(BF), wo_ref[...], preferred_element_type=F32)

    @pl.when(j == pl.num_programs(1) - 1)
    def _():
        o_ref[...] = x_ref[...] + acc_ref[...]


def merge_project(x, proj, branches, w_branch, w_out, layer, tm=512, tn=512):
    T, D = x.shape
    tm = min(tm, T)
    per = D // tn
    o_spec = pl.BlockSpec((tm, MIX_W), lambda i, j: (i, 0))
    g_spec = lambda b: pl.BlockSpec((tm, tn), lambda i, j: (i, (BLK_MG * LANE) // tn + b * per + j))
    return pl.pallas_call(
        _merge_kernel,
        grid=(T // tm, per),
        in_specs=[o_spec, o_spec, o_spec, o_spec, g_spec(0), g_spec(1), g_spec(2), g_spec(3),
                  pl.BlockSpec((None, N_BRANCH, MIX_W, tn), lambda i, j: (layer, 0, 0, j)),
                  pl.BlockSpec((None, tn, D), lambda i, j: (layer, j, 0)),
                  pl.BlockSpec((tm, D), lambda i, j: (i, 0))],
        out_specs=pl.BlockSpec((tm, D), lambda i, j: (i, 0)),
        out_shape=jax.ShapeDtypeStruct((T, D), F32),
        scratch_shapes=[pltpu.VMEM((tm, D), F32)],
        compiler_params=_cparams(("parallel", "arbitrary")),
        name="merge_project",
    )(*branches, proj, proj, proj, proj, w_branch, w_out, x)


def _cross_attn_kernel(q_ref, k_ref, v_ref, o_ref):
    dh = q_ref.shape[1] // XA_HEADS
    for h in range(XA_HEADS):
        cols = slice(h * dh, (h + 1) * dh)
        s = _dot_nt(q_ref[:, cols], k_ref[:, cols]) * (dh ** -0.5)
        m = jnp.max(s, axis=-1, keepdims=True)
        p = jnp.exp(s - m)
        p = p / jnp.sum(p, axis=-1, keepdims=True)
        o_ref[:, cols] = _dot(p, v_ref[:, cols]).astype(o_ref.dtype)


def cross_attention_core(q, kv, tq=512):
    B, S, D = q.shape
    M = kv.shape[1]
    tq = min(tq, S)
    return pl.pallas_call(
        _cross_attn_kernel,
        grid=(B, S // tq),
        in_specs=[pl.BlockSpec((None, tq, D), lambda b, i: (b, i, 0)),
                  pl.BlockSpec((None, M, D), lambda b, i: (b, 0, 0)),
                  pl.BlockSpec((None, M, D), lambda b, i: (b, 0, 1))],
        out_specs=pl.BlockSpec((None, tq, D), lambda b, i: (b, i, 0)),
        out_shape=jax.ShapeDtypeStruct((B, S, D), BF),
        compiler_params=_cparams(("parallel", "parallel")),
        name="cross_attention",
    )(q, kv, kv)


def _ffn_kernel(x_ref, nw_ref, wg_ref, wu_ref, wo_ref, o_ref, h_ref, acc_ref):
    j = pl.program_id(1)

    @pl.when(j == 0)
    def _():
        h_ref[...] = _rms(x_ref[...], nw_ref[...], 1e-6).astype(BF)
        acc_ref[...] = jnp.zeros_like(acc_ref)

    h = h_ref[...]
    gate = jnp.dot(h, wg_ref[...], preferred_element_type=F32)
    up = jnp.dot(h, wu_ref[...], preferred_element_type=F32)
    acc_ref[...] += jnp.dot((_silu(gate) * up).astype(BF), wo_ref[...], preferred_element_type=F32)

    @pl.when(j == pl.num_programs(1) - 1)
    def _():
        o_ref[...] = x_ref[...] + acc_ref[...]


def swiglu_ffn(x, norm_w, w_in, w_out, layer, tm=512, tf=512):
    T, D = x.shape
    F = w_out.shape[1]
    tm = min(tm, T)
    nf = F // tf
    return pl.pallas_call(
        _ffn_kernel,
        grid=(T // tm, nf),
        in_specs=[pl.BlockSpec((tm, D), lambda i, j: (i, 0)),
                  pl.BlockSpec((1, D), lambda i, j: (0, 0)),
                  pl.BlockSpec((None, D, tf), lambda i, j: (layer, 0, j)),
                  pl.BlockSpec((None, D, tf), lambda i, j: (layer, 0, nf + j)),
                  pl.BlockSpec((None, tf, D), lambda i, j: (layer, j, 0))],
        out_specs=pl.BlockSpec((tm, D), lambda i, j: (i, 0)),
        out_shape=jax.ShapeDtypeStruct((T, D), F32),
        scratch_shapes=[pltpu.VMEM((tm, D), BF), pltpu.VMEM((tm, D), F32)],
        compiler_params=_cparams(("parallel", "arbitrary")),
        name="swiglu_ffn",
    )(x, norm_w.reshape(1, D), w_in, w_in, w_out)


def _pack_w_in(w):
    n_gate = 3 * N_HEADS
    gate_end = BLK_NG * LANE + n_gate
    wt = jnp.swapaxes(w, 1, 2)
    pad = jnp.zeros((w.shape[0], BLK_GM * LANE - gate_end, w.shape[1]), w.dtype)
    packed = jnp.concatenate([wt[:, :gate_end], pad, wt[:, gate_end:]], axis=1)
    assert packed.shape[1] == N_PROJ, packed.shape
    return packed.astype(BF)


def kernel(x, mem, mem_norm_w, mix_norm_w, w_in, hg_lower_bounds, hg_norm_w, da_lambda_q1, da_lambda_k1, da_lambda_q2, da_lambda_k2, da_norm_w, nsa_pos_k, nsa_cmp_w1_k, nsa_cmp_w2_k, nsa_pos_v, nsa_cmp_w1_v, nsa_cmp_w2_v, gm_ln_w, gm_ln_b, gm_w_s, gm_b_s, w_branch, w_out, xa_norm_w, xa_w_q, xa_w_kv, xa_w_o, ffn_norm_w, ffn_w_in, ffn_w_out, final_norm_w):
    B, S, D = x.shape
    M = mem.shape[1]
    depth = w_in.shape[0]
    assert D == BLK_MG * LANE // 3 and S % SWA_WINDOW == 0, (S, D)
    T = B * S
    x = x.reshape(T, D)
    mem2 = mem.reshape(B * M, D)
    w_in_p = _pack_w_in(w_in)
    w_branch_b, w_out_b = w_branch.astype(BF), w_out.astype(BF)
    xa_w_q_b, xa_w_kv_b, xa_w_o_b = xa_w_q.astype(BF), xa_w_kv.astype(BF), xa_w_o.astype(BF)
    ffn_w_in_b, ffn_w_out_b = ffn_w_in.astype(BF), ffn_w_out.astype(BF)
    for l in range(depth):
        lambda_init = 0.8 - 0.6 * math.exp(-0.3 * l)
        proj = norm_matmul(x, mix_norm_w[l], w_in_p, l, F32, tm=1024, w_is_transposed=True).reshape(B, S, N_PROJ)
        o_a = hgrn2_mixer(proj, hg_lower_bounds, hg_norm_w[l], l)
        lam_params = jnp.stack([da_lambda_q1[l], da_lambda_k1[l], da_lambda_q2[l], da_lambda_k2[l]])
        o_b = diff_attention_mixer(proj, lam_params, da_norm_w[l], lambda_init)
        nkv = lambda off: proj[:, :, (BLK_NKV + off) * LANE:(BLK_NKV + off + 1) * LANE]
        kc = compress_tokens(nkv(0), nsa_pos_k[l], nsa_cmp_w1_k[l].astype(BF), nsa_cmp_w2_k[l].astype(BF))
        vc = compress_tokens(nkv(1), nsa_pos_v[l], nsa_cmp_w1_v[l].astype(BF), nsa_cmp_w2_v[l].astype(BF))
        o_cmp, sel = nsa_compressed_and_select(proj, kc, vc)
        o_slc = nsa_selected_attention(proj, sel)
        o_c = nsa_window_and_combine(proj, o_cmp, o_slc)
        o_d = chunk_gmlp_mixer(proj, gm_ln_w[l], gm_ln_b[l], gm_w_s[l], gm_b_s[l])
        branches = [o.reshape(T, MIX_W) for o in (o_a, o_b, o_c, o_d)]
        x = merge_project(x, proj.reshape(T, N_PROJ), branches, w_branch_b, w_out_b, l)
        q = norm_matmul(x, xa_norm_w[l], xa_w_q_b, l, BF, tm=1024).reshape(B, S, D)
        kv = norm_matmul(mem2, mem_norm_w, xa_w_kv_b, l, F32).reshape(B, M, 2 * D)
        att = cross_attention_core(q, kv).reshape(T, D)
        x = matmul_residual(att, xa_w_o_b, l, x, tm=1024)
        x = swiglu_ffn(x, ffn_norm_w[l], ffn_w_in_b, ffn_w_out_b, l)
    return rmsnorm(x, final_norm_w).reshape(B, S, D)
```

```python
import functools
import math

import jax
import jax.numpy as jnp
from jax import lax
from jax.experimental import pallas as pl
from jax.experimental.pallas import tpu as pltpu

BF = jnp.bfloat16
F32 = jnp.float32

LANE = 128
HEAD_DIM = 128
N_HEADS = 4
MIX_W = N_HEADS * HEAD_DIM
N_BRANCH = 4
HG_CHUNK = 64
HG_SUB = 16
HG_EXP_CLAMP = 80.0
CMP_LEN = 32
CMP_STRIDE = 16
SLC_BLOCK = 64
SLC_TOPK = 16
N_LOCAL_BLOCKS = 2
SWA_WINDOW = 512
XA_HEADS = 4
NEG = -1e30
TINY = 1e-30
FORCE_SCORE = 1e4
LOG2E = math.log2(math.e)
FLASH_MM_ROWS = 256
VMEM_LIMIT = 56 * 1024 * 1024

BLK_HG = 0
BLK_DA = 16
BLK_NQ = 28
BLK_NKV = 32
BLK_NG = 38
BLK_GM = 40
BLK_MG = 48
N_PROJ = 112 * LANE


def _cparams(sem):
    return pltpu.CompilerParams(dimension_semantics=sem, vmem_limit_bytes=VMEM_LIMIT)


def _dot(a, b):
    return jnp.dot(a.astype(BF), b.astype(BF), preferred_element_type=F32)


def _dot_nt(a, b):
    return lax.dot_general(a.astype(BF), b.astype(BF), (((1,), (1,)), ((), ())),
                           preferred_element_type=F32)


def _dot_tn(a, b):
    return lax.dot_general(a.astype(BF), b.astype(BF), (((0,), (0,)), ((), ())),
                           preferred_element_type=F32)


def _split3(a):
    hi = a.astype(BF)
    r = a - hi.astype(F32)
    mid = r.astype(BF)
    lo = (r - mid.astype(F32)).astype(BF)
    return hi, mid, lo


def _dot_exact_rhs(a_bf, b):
    hi, mid, lo = _split3(b)
    f = lambda t: jnp.dot(a_bf, t, preferred_element_type=F32)
    return f(hi) + f(mid) + f(lo)


def _dot_exact_lhs(a, b_bf):
    hi, mid, lo = _split3(a)
    f = lambda t: jnp.dot(t, b_bf, preferred_element_type=F32)
    return f(hi) + f(mid) + f(lo)


def _rms(x, w, eps):
    return x * lax.rsqrt(jnp.mean(x * x, axis=-1, keepdims=True) + eps) * w


def _sigmoid(x):
    return 1.0 / (1.0 + jnp.exp(-x))


def _silu(x):
    return x * _sigmoid(x)


def _norm_matmul_kernel(w_is_transposed, x_ref, nw_ref, w_ref, o_ref, h_ref):
    @pl.when(pl.program_id(1) == 0)
    def _():
        h_ref[...] = _rms(x_ref[...], nw_ref[...], 1e-6).astype(BF)

    if w_is_transposed:
        o = lax.dot_general(h_ref[...], w_ref[...], (((1,), (1,)), ((), ())), preferred_element_type=F32)
    else:
        o = jnp.dot(h_ref[...], w_ref[...], preferred_element_type=F32)
    o_ref[...] = o.astype(o_ref.dtype)


def norm_matmul(x, norm_w, w, layer, out_dtype, tm=512, tn=1024, w_is_transposed=False):
    T, D = x.shape
    N = w.shape[1] if w_is_transposed else w.shape[2]
    tm, tn = min(tm, T), min(tn, N)
    w_spec = (pl.BlockSpec((None, tn, D), lambda i, j: (layer, j, 0)) if w_is_transposed
              else pl.BlockSpec((None, D, tn), lambda i, j: (layer, 0, j)))
    return pl.pallas_call(
        functools.partial(_norm_matmul_kernel, w_is_transposed),
        grid=(T // tm, N // tn),
        in_specs=[pl.BlockSpec((tm, D), lambda i, j: (i, 0)),
                  pl.BlockSpec((1, D), lambda i, j: (0, 0)),
                  w_spec],
        out_specs=pl.BlockSpec((tm, tn), lambda i, j: (i, j)),
        out_shape=jax.ShapeDtypeStruct((T, N), out_dtype),
        scratch_shapes=[pltpu.VMEM((tm, D), BF)],
        compiler_params=_cparams(("parallel", "arbitrary")),
        name="norm_matmul",
    )(x, norm_w.reshape(1, D), w)


def _matmul_residual_kernel(a_ref, w_ref, r_ref, o_ref):
    o_ref[...] = r_ref[...] + jnp.dot(a_ref[...], w_ref[...], preferred_element_type=F32)


def matmul_residual(a, w, layer, r, tm=512, tn=1024):
    T, K = a.shape
    N = w.shape[2]
    tm, tn = min(tm, T), min(tn, N)
    return pl.pallas_call(
        _matmul_residual_kernel,
        grid=(T // tm, N // tn),
        in_specs=[pl.BlockSpec((tm, K), lambda i, j: (i, 0)),
                  pl.BlockSpec((None, K, tn), lambda i, j: (layer, 0, j)),
                  pl.BlockSpec((tm, tn), lambda i, j: (i, j))],
        out_specs=pl.BlockSpec((tm, tn), lambda i, j: (i, j)),
        out_shape=jax.ShapeDtypeStruct((T, N), F32),
        compiler_params=_cparams(("parallel", "arbitrary")),
        name="matmul_residual",
    )(a, w, r)


def _rmsnorm_kernel(x_ref, w_ref, o_ref):
    o_ref[...] = _rms(x_ref[...], w_ref[...], 1e-6)


def rmsnorm(x, w, tm=512):
    T, D = x.shape
    tm = min(tm, T)
    return pl.pallas_call(
        _rmsnorm_kernel,
        grid=(T // tm,),
        in_specs=[pl.BlockSpec((tm, D), lambda i: (i, 0)),
                  pl.BlockSpec((1, D), lambda i: (0, 0))],
        out_specs=pl.BlockSpec((tm, D), lambda i: (i, 0)),
        out_shape=jax.ShapeDtypeStruct((T, D), F32),
        compiler_params=_cparams(("parallel",)),
        name="final_rmsnorm",
    )(x, w.reshape(1, D))


def _hgrn2_kernel(layer, q_ref, f_ref, i_ref, g_ref, lb_ref, nw_ref, o_ref, state_ref):
    @pl.when(pl.program_id(1) == 0)
    def _():
        state_ref[...] = jnp.zeros_like(state_ref)

    lb_all = lb_ref[...]
    e = jnp.exp(lb_all - jnp.max(lb_all, axis=0, keepdims=True))
    soft = e / jnp.sum(e, axis=0, keepdims=True)
    lrow = lax.broadcasted_iota(jnp.int32, soft.shape, 0)
    lower_all = jnp.sum(jnp.where((lrow >= 1) & (lrow <= layer), soft, 0.0), axis=0, keepdims=True)

    C, SUB = HG_CHUNK, HG_SUB
    tt = q_ref.shape[0]
    row = lax.broadcasted_iota(jnp.int32, (C, C), 0)
    col = lax.broadcasted_iota(jnp.int32, (C, C), 1)
    tri = jnp.where(col <= row, 1.0, 0.0).astype(BF)
    tri_sub = jnp.where(col < (row // SUB) * SUB, 1.0, 0.0).astype(BF)
    tri2 = jnp.concatenate([tri, tri_sub], axis=0)
    nsub = C // SUB
    mrow = lax.broadcasted_iota(jnp.int32, (C, nsub * C), 0)
    mcol = lax.broadcasted_iota(jnp.int32, (C, nsub * C), 1)
    keep = jnp.where(mcol // C == mrow // SUB, mcol % C, C) <= mrow
    w = MIX_W

    def chunk(c, states):
        rows = slice(c * C, (c + 1) * C)
        f_all = lower_all + (1.0 - lower_all) * _sigmoid(f_ref[rows, :])
        log_f = jnp.log(f_all)
        hi, mid, lo = _split3(log_f)
        cum = jnp.dot(tri2, jnp.concatenate([hi, mid, lo], axis=1), preferred_element_type=F32)
        cum = cum[:, 0:w] + cum[:, w:2 * w] + cum[:, 2 * w:3 * w]
        b_all, b_sub_all = cum[0:C], cum[C:2 * C]
        b_end_all = jnp.sum(log_f, axis=0, keepdims=True)
        new_states = []
        for h in range(N_HEADS):
            cols = slice(h * HEAD_DIM, (h + 1) * HEAD_DIM)
            b, b_sub, b_end = b_all[:, cols], b_sub_all[:, cols], b_end_all[:, cols]
            q = _silu(q_ref[rows, cols])
            k = 1.0 - f_all[:, cols]
            v = i_ref[rows, cols]
            o_inter = _dot_nt(q * jnp.exp(b), states[h])
            qt = q * jnp.exp(b - b_sub)
            kt = jnp.concatenate(
                [k * jnp.exp(jnp.minimum(b_sub[s * SUB:s * SUB + 1] - b, HG_EXP_CLAMP)) for s in range(nsub)],
                axis=0)
            sc = jnp.where(keep, _dot_nt(qt, kt), 0.0)
            o = o_inter + _dot(sc, jnp.concatenate([v] * nsub, axis=0))
            o = _rms(o, nw_ref[...], 1e-6) * _silu(g_ref[rows, cols])
            o_ref[rows, cols] = o.astype(o_ref.dtype)
            new_states.append(states[h] * jnp.exp(b_end) + _dot_tn(v, k * jnp.exp(b_end - b)))
        return new_states

    states = [state_ref[h] for h in range(N_HEADS)]
    for c in range(tt // C):
        states = chunk(c, states)
    for h in range(N_HEADS):
        state_ref[h] = states[h]


def hgrn2_mixer(proj, lower_bounds, norm_w, layer, tt=256):
    B, S, _ = proj.shape
    L = lower_bounds.shape[0]
    tt = min(tt, S)
    seg = lambda off: pl.BlockSpec((None, tt, MIX_W), lambda b, t: (b, t, (BLK_HG + off) // 4))
    return pl.pallas_call(
        functools.partial(_hgrn2_kernel, layer),
        grid=(B, S // tt),
        in_specs=[seg(0), seg(4), seg(8), seg(12),
                  pl.BlockSpec((L, MIX_W), lambda b, t: (0, 0)),
                  pl.BlockSpec((1, LANE), lambda b, t: (0, 0))],
        out_specs=pl.BlockSpec((None, tt, MIX_W), lambda b, t: (b, t, 0)),
        out_shape=jax.ShapeDtypeStruct((B, S, MIX_W), BF),
        scratch_shapes=[pltpu.VMEM((N_HEADS, HEAD_DIM, HEAD_DIM), F32)],
        compiler_params=_cparams(("parallel", "arbitrary")),
        name="hgrn2",
    )(proj, proj, proj, proj, lower_bounds, norm_w.reshape(1, LANE))


def _flash_init(m_ref, acc_ref, vx_ref):
    m_ref[...] = jnp.full_like(m_ref, NEG)
    acc_ref[...] = jnp.zeros_like(acc_ref)
    vx_ref[:, LANE:2 * LANE] = jnp.ones((vx_ref.shape[0], LANE), BF)


def _flash_step(qs_ref, k_ref, v_ref, vx_ref, s_ref, p_ref, m_ref, acc_ref, rb, bias_fn):
    nrows, tk = s_ref.shape
    k = k_ref[...].astype(BF)
    vx_ref[:, 0:LANE] = v_ref[...].astype(BF)
    for c in range(nrows // FLASH_MM_ROWS):
        rows = slice(c * FLASH_MM_ROWS, (c + 1) * FLASH_MM_ROWS)
        s_ref[rows, :] = _dot_nt(qs_ref[rows, :], k)
    for r in range(nrows // rb):
        rows = slice(r * rb, (r + 1) * rb)
        s = s_ref[rows, :]
        if bias_fn is not None:
            s = s + bias_fn(r * rb)
        m_prev = m_ref[rows, :]
        m_new = jnp.maximum(m_prev, jnp.max(s, axis=-1, keepdims=True))
        p_ref[rows, :] = jnp.exp2(s - jnp.concatenate([m_new] * (tk // LANE), axis=1)).astype(BF)
        alpha = jnp.exp2(m_prev - m_new)
        acc_ref[rows, :] = acc_ref[rows, :] * jnp.concatenate([alpha, alpha], axis=1)
        m_ref[rows, :] = m_new
    vx = vx_ref[...]
    for c in range(nrows // FLASH_MM_ROWS):
        rows = slice(c * FLASH_MM_ROWS, (c + 1) * FLASH_MM_ROWS)
        acc_ref[rows, :] += jnp.dot(p_ref[rows, :], vx, preferred_element_type=F32)


def _diff_attn_kernel(lambda_init, tq, tk, rb, q_ref, k_ref, v_ref, lam_ref, nw_ref, o_ref,
                      qs_ref, m_ref, acc_ref, vx_ref, s_ref, p_ref):
    i, j = pl.program_id(2), pl.program_id(3)
    dc = HEAD_DIM // 2

    @pl.when(j == 0)
    def _():
        q = q_ref[...] * (dc ** -0.5 * LOG2E)
        first = lax.broadcasted_iota(jnp.int32, q.shape, 1) < dc
        qs_ref[0:tq, :] = jnp.where(first, q, 0.0).astype(BF)
        qs_ref[tq:2 * tq, :] = jnp.where(first, 0.0, q).astype(BF)
        _flash_init(m_ref, acc_ref, vx_ref)

    def causal_bias(r0):
        r = r0 + lax.broadcasted_iota(jnp.int32, (rb, tk), 0)
        qloc = jnp.where(r >= tq, r - tq, r)
        col = lax.broadcasted_iota(jnp.int32, (rb, tk), 1)
        return jnp.where(col - qloc <= i * tq - j * tk, 0.0, NEG)

    needed = j * tk <= i * tq + tq - 1
    unmasked = (j + 1) * tk - 1 <= i * tq

    @pl.when(needed & unmasked)
    def _():
        _flash_step(qs_ref, k_ref, v_ref, vx_ref, s_ref, p_ref, m_ref, acc_ref, rb, None)

    @pl.when(needed & jnp.logical_not(unmasked))
    def _():
        _flash_step(qs_ref, k_ref, v_ref, vx_ref, s_ref, p_ref, m_ref, acc_ref, rb, causal_bias)

    @pl.when(j == pl.num_programs(3) - 1)
    def _():
        lam = (jnp.exp(jnp.sum(lam_ref[0:1] * lam_ref[1:2], axis=-1, keepdims=True))
               - jnp.exp(jnp.sum(lam_ref[2:3] * lam_ref[3:4], axis=-1, keepdims=True)) + lambda_init)
        o = acc_ref[:, 0:LANE] / acc_ref[:, LANE:2 * LANE]
        o = o[0:tq] - lam * o[tq:2 * tq]
        o = _rms(o, nw_ref[...], 1e-6) * (1.0 - lambda_init)
        o_ref[...] = o.astype(o_ref.dtype)


def diff_attention_mixer(proj, lam_params, norm_w, lambda_init, tq=1024, tk=1024, rb=16):
    B, S, _ = proj.shape
    tq, tk = min(tq, S), min(tk, S)
    last = lambda i: (i * tq + tq - 1) // tk
    return pl.pallas_call(
        functools.partial(_diff_attn_kernel, lambda_init, tq, tk, rb),
        grid=(B, N_HEADS, S // tq, S // tk),
        in_specs=[pl.BlockSpec((None, tq, LANE), lambda b, h, i, j: (b, i, BLK_DA + h)),
                  pl.BlockSpec((None, tk, LANE),
                               lambda b, h, i, j: (b, jnp.minimum(j, last(i)), BLK_DA + 4 + h)),
                  pl.BlockSpec((None, tk, LANE),
                               lambda b, h, i, j: (b, jnp.minimum(j, last(i)), BLK_DA + 8 + h)),
                  pl.BlockSpec((4, HEAD_DIM // 2), lambda b, h, i, j: (0, 0)),
                  pl.BlockSpec((1, LANE), lambda b, h, i, j: (0, 0))],
        out_specs=pl.BlockSpec((None, tq, LANE), lambda b, h, i, j: (b, i, h)),
        out_shape=jax.ShapeDtypeStruct((B, S, MIX_W), BF),
        scratch_shapes=[pltpu.VMEM((2 * tq, LANE), BF),
                        pltpu.VMEM((2 * tq, LANE), F32),
                        pltpu.VMEM((2 * tq, 2 * LANE), F32),
                        pltpu.VMEM((tk, 2 * LANE), BF),
                        pltpu.VMEM((2 * tq, tk), F32),
                        pltpu.VMEM((2 * tq, tk), BF)],
        compiler_params=_cparams(("parallel", "parallel", "parallel", "arbitrary")),
        name="diff_attention",
    )(proj, proj, proj, lam_params, norm_w.reshape(1, LANE))


def _compress_kernel(t_ref, pos_ref, w1_ref, w2_ref, o_ref):
    half = t_ref.shape[1]
    t = t_ref[...]
    a = _dot(t, w1_ref[0:half, :])
    bnext = _dot(t, w1_ref[half:2 * half, :])
    c0 = _dot(pos_ref[...], w1_ref[...])
    n = a.shape[0]
    hidden = a + pltpu.roll(bnext, n - 1, 0) + c0
    o_ref[...] = _dot(_silu(hidden), w2_ref[...])


def compress_tokens(t, pos, w1, w2):
    B, S, d = t.shape
    n_chunks = S // CMP_STRIDE
    hid = w1.shape[1]
    t = t.reshape(B, n_chunks, CMP_STRIDE * d)
    return pl.pallas_call(
        _compress_kernel,
        grid=(B,),
        in_specs=[pl.BlockSpec((None, n_chunks, CMP_STRIDE * d), lambda b: (b, 0, 0)),
                  pl.BlockSpec((1, CMP_LEN * d), lambda b: (0, 0)),
                  pl.BlockSpec((CMP_LEN * d, hid), lambda b: (0, 0)),
                  pl.BlockSpec((hid, d), lambda b: (0, 0))],
        out_specs=pl.BlockSpec((None, n_chunks, d), lambda b: (b, 0, 0)),
        out_shape=jax.ShapeDtypeStruct((B, n_chunks, d), F32),
        compiler_params=_cparams(("parallel",)),
        name="nsa_compress",
    )(t, pos.reshape(1, CMP_LEN * d), w1, w2)


def _cmp_select_kernel(tq, rb, q_ref, kc_ref, vc_ref, pool_ref, o_ref, sel_ref, s_ref, p_ref, imp_ref, bias_ref):
    i = pl.program_id(1)
    ncp = kc_ref.shape[0]
    n_slc = sel_ref.shape[1]
    scale = HEAD_DIM ** -0.5
    qpos = i * tq + lax.broadcasted_iota(jnp.int32, (tq, 1), 0)
    cmp_end = lax.broadcasted_iota(jnp.int32, (1, ncp), 1) * CMP_STRIDE + (CMP_LEN - 1)
    bias_ref[...] = jnp.where(cmp_end <= qpos, 0.0, NEG)
    kc = kc_ref[...].astype(BF)
    vc = vc_ref[...].astype(BF)
    for h in range(N_HEADS):
        cols = slice(h * HEAD_DIM, (h + 1) * HEAD_DIM)
        slot = h % 2
        s_ref[slot] = _dot_nt(q_ref[:, cols] * scale, kc)
        for r in range(tq // rb):
            rows = slice(r * rb, (r + 1) * rb)
            s = s_ref[slot, rows, :] + bias_ref[rows, :]
            e = jnp.exp(s - jnp.max(s, axis=-1, keepdims=True))
            seen = i * tq + r * rb + lax.broadcasted_iota(jnp.int32, (rb, 1), 0) >= CMP_LEN - 1
            p = e * jnp.where(seen, 1.0 / jnp.sum(e, axis=-1, keepdims=True), 0.0)
            p_ref[slot, rows, :] = p.astype(BF)
            imp_ref[rows, :] = p if h == 0 else imp_ref[rows, :] + p
        o_ref[:, cols] = jnp.dot(p_ref[slot], vc, preferred_element_type=F32)
    imp = imp_ref[...]
    pool_t = pool_ref[...]
    imp_hi, imp_mid, _ = _split3(imp)
    imp_slc = _dot_nt(pool_t, imp_hi) + _dot_nt(pool_t, imp_mid)
    blk = lax.broadcasted_iota(jnp.int32, (n_slc, 1), 0)
    cur = (i * tq + lax.broadcasted_iota(jnp.int32, (1, tq), 1)) // SLC_BLOCK
    valid = blk <= cur
    forced = (blk == 0) | (blk > cur - N_LOCAL_BLOCKS)
    n_forced = 1 + N_LOCAL_BLOCKS
    score = jnp.where(valid, jnp.where(forced, -2.0, imp_slc), -1.0)
    blk_f = blk.astype(F32)
    for _ in range(min(SLC_TOPK, n_slc) - n_forced):
        m = jnp.max(score, axis=0, keepdims=True)
        first = jnp.min(jnp.where(score == m, blk_f, float(n_slc)), axis=0, keepdims=True)
        score = jnp.where(blk_f == first, -2.0, score)
    sel_ref[...] = jnp.where(score == -2.0, 1.0, 0.0).T.astype(sel_ref.dtype)


def nsa_compressed_and_select(proj, kc, vc, tq=256, rb=16):
    B, S, _ = proj.shape
    ncp = kc.shape[1]
    n_slc = S // SLC_BLOCK
    n = jnp.arange(ncp)[:, None]
    jj = jnp.arange(n_slc)[None, :]
    w = n - (SLC_BLOCK // CMP_STRIDE) * jj + 1
    pool = jnp.where((w >= 0) & (w <= 4), jnp.where((w == 0) | (w == 4), 1.0, 2.0), 0.0).astype(BF).T
    return pl.pallas_call(
        functools.partial(_cmp_select_kernel, tq, rb),
        grid=(B, S // tq),
        in_specs=[pl.BlockSpec((None, tq, MIX_W), lambda b, i: (b, i, BLK_NQ // 4)),
                  pl.BlockSpec((None, ncp, HEAD_DIM), lambda b, i: (b, 0, 0)),
                  pl.BlockSpec((None, ncp, HEAD_DIM), lambda b, i: (b, 0, 0)),
                  pl.BlockSpec((n_slc, ncp), lambda b, i: (0, 0))],
        out_specs=[pl.BlockSpec((None, tq, MIX_W), lambda b, i: (b, i, 0)),
                   pl.BlockSpec((None, tq, n_slc), lambda b, i: (b, i, 0))],
        out_shape=[jax.ShapeDtypeStruct((B, S, MIX_W), F32),
                   jax.ShapeDtypeStruct((B, S, n_slc), BF)],
        scratch_shapes=[pltpu.VMEM((2, tq, ncp), F32), pltpu.VMEM((2, tq, ncp), BF),
                        pltpu.VMEM((tq, ncp), F32), pltpu.VMEM((tq, ncp), F32)],
        compiler_params=_cparams(("parallel", "parallel")),
        name="nsa_cmp_select",
    )(proj, kc, vc, pool)


def _slc_attn_kernel(tq, tk, rb, q_ref, k_ref, v_ref, sel_ref, exp_ref, o_ref,
                     qs_ref, m_ref, acc_ref, vx_ref, s_ref, p_ref, bias_ref):
    i, j = pl.program_id(1), pl.program_id(2)

    @pl.when(j == 0)
    def _():
        for h in range(N_HEADS):
            q = q_ref[:, h * HEAD_DIM:(h + 1) * HEAD_DIM] * (HEAD_DIM ** -0.5 * LOG2E)
            qs_ref[h * tq:(h + 1) * tq, :] = q.astype(BF)
        _flash_init(m_ref, acc_ref, vx_ref)

    @pl.when(j * tk <= i * tq + tq - 1)
    def _():
        picked = jnp.dot(sel_ref[...], exp_ref[...], preferred_element_type=F32)
        qpos = i * tq + lax.broadcasted_iota(jnp.int32, (tq, tk), 0)
        kpos = j * tk + lax.broadcasted_iota(jnp.int32, (tq, tk), 1)
        bias_ref[...] = jnp.where(kpos <= qpos, (picked - 1.0) * (-NEG), NEG)
        _flash_step(qs_ref, k_ref, v_ref, vx_ref, s_ref, p_ref, m_ref, acc_ref, rb,
                    lambda r0: bias_ref[(r0 % tq):(r0 % tq) + rb, :])

    @pl.when(j == pl.num_programs(2) - 1)
    def _():
        for h in range(N_HEADS):
            rows = slice(h * tq, (h + 1) * tq)
            o_ref[:, h * HEAD_DIM:(h + 1) * HEAD_DIM] = acc_ref[rows, 0:LANE] / acc_ref[rows, LANE:2 * LANE]


def nsa_selected_attention(proj, sel, tq=512, tk=1024, rb=16):
    B, S, _ = proj.shape
    tq, tk = min(tq, S), min(tk, S)
    n_slc = S // SLC_BLOCK
    expand = (jnp.arange(n_slc)[:, None] == (jnp.arange(S)[None, :] // SLC_BLOCK)).astype(BF)
    last = lambda i: (i * tq + tq - 1) // tk
    kv = lambda off: pl.BlockSpec((None, tk, LANE),
                                  lambda b, i, j: (b, jnp.minimum(j, last(i)), BLK_NKV + off))
    rows = N_HEADS * tq
    return pl.pallas_call(
        functools.partial(_slc_attn_kernel, tq, tk, rb),
        grid=(B, S // tq, S // tk),
        in_specs=[pl.BlockSpec((None, tq, MIX_W), lambda b, i, j: (b, i, BLK_NQ // 4)),
                  kv(2), kv(3),
                  pl.BlockSpec((None, tq, n_slc), lambda b, i, j: (b, i, 0)),
                  pl.BlockSpec((n_slc, tk), lambda b, i, j: (0, jnp.minimum(j, last(i))))],
        out_specs=pl.BlockSpec((None, tq, MIX_W), lambda b, i, j: (b, i, 0)),
        out_shape=jax.ShapeDtypeStruct((B, S, MIX_W), F32),
        scratch_shapes=[pltpu.VMEM((rows, LANE), BF),
                        pltpu.VMEM((rows, LANE), F32),
                        pltpu.VMEM((rows, 2 * LANE), F32),
                        pltpu.VMEM((tk, 2 * LANE), BF),
                        pltpu.VMEM((rows, tk), F32),
                        pltpu.VMEM((rows, tk), BF),
                        pltpu.VMEM((tq, tk), F32)],
        compiler_params=_cparams(("parallel", "parallel", "arbitrary")),
        name="nsa_selected",
    )(proj, proj, proj, sel, expand)


def _swa_combine_kernel(tq, rb, q_ref, kp_ref, kc_ref, vp_ref, vc_ref, g_ref, ocmp_ref, oslc_ref, bias_ref, o_ref,
                        s_ref, p_ref):
    scale = HEAD_DIM ** -0.5
    k = jnp.concatenate([kp_ref[...], kc_ref[...]], axis=0).astype(BF)
    v = jnp.concatenate([vp_ref[...], vc_ref[...]], axis=0).astype(BF)
    gates = _sigmoid(g_ref[...])
    for h in range(N_HEADS):
        cols = slice(h * HEAD_DIM, (h + 1) * HEAD_DIM)
        slot = h % 2
        s_ref[slot] = _dot_nt(q_ref[:, cols] * scale, k)
        for r in range(tq // rb):
            rows = slice(r * rb, (r + 1) * rb)
            s = s_ref[slot, rows, :] + bias_ref[rows, :]
            e = jnp.exp(s - jnp.max(s, axis=-1, keepdims=True))
            p_ref[slot, rows, :] = (e * (1.0 / jnp.sum(e, axis=-1, keepdims=True))).astype(BF)
        o_swa = jnp.dot(p_ref[slot], v, preferred_element_type=F32)
        o = (gates[:, 3 * h:3 * h + 1] * ocmp_ref[:, cols]
             + gates[:, 3 * h + 1:3 * h + 2] * oslc_ref[:, cols]
             + gates[:, 3 * h + 2:3 * h + 3] * o_swa)
        o_ref[:, cols] = o.astype(o_ref.dtype)


def nsa_window_and_combine(proj, o_cmp, o_slc):
    B, S, _ = proj.shape
    tq = SWA_WINDOW
    kvp = lambda off: pl.BlockSpec((None, tq, LANE), lambda b, i: (b, jnp.maximum(i - 1, 0), BLK_NKV + off))
    kvc = lambda off: pl.BlockSpec((None, tq, LANE), lambda b, i: (b, i, BLK_NKV + off))
    full = lambda: pl.BlockSpec((None, tq, MIX_W), lambda b, i: (b, i, 0))
    dist = jnp.arange(tq)[:, None] - (jnp.arange(SWA_WINDOW + tq)[None, :] - SWA_WINDOW)
    inside = (dist >= 0) & (dist < SWA_WINDOW)
    first = inside & (jnp.arange(SWA_WINDOW + tq)[None, :] >= SWA_WINDOW)
    bias = jnp.where(jnp.stack([first, inside]), 0.0, NEG).astype(F32)
    return pl.pallas_call(
        functools.partial(_swa_combine_kernel, tq, 16),
        grid=(B, S // tq),
        in_specs=[pl.BlockSpec((None, tq, MIX_W), lambda b, i: (b, i, BLK_NQ // 4)),
                  kvp(4), kvc(4), kvp(5), kvc(5),
                  pl.BlockSpec((None, tq, LANE), lambda b, i: (b, i, BLK_NG)),
                  full(), full(),
                  pl.BlockSpec((None, tq, SWA_WINDOW + tq), lambda b, i: (jnp.minimum(i, 1), 0, 0))],
        out_specs=full(),
        out_shape=jax.ShapeDtypeStruct((B, S, MIX_W), BF),
        scratch_shapes=[pltpu.VMEM((2, tq, SWA_WINDOW + tq), F32), pltpu.VMEM((2, tq, SWA_WINDOW + tq), BF)],
        compiler_params=_cparams(("parallel", "parallel")),
        name="nsa_window_combine",
    )(proj, proj, proj, proj, proj, proj, o_cmp, o_slc, bias)


def _gelu_tanh(x):
    return 0.5 * x * (1.0 + jnp.tanh(math.sqrt(2.0 / math.pi) * (x + 0.044715 * (x * x * x))))


def _gmlp_kernel(u_ref, v_ref, lnw_ref, lnb_ref, ws_ref, bs_ref, o_ref):
    C = ws_ref.shape[1]
    row = lax.broadcasted_iota(jnp.int32, (C, C), 0)
    col = lax.broadcasted_iota(jnp.int32, (C, C), 1)
    causal = col <= row
    for c in range(u_ref.shape[0] // C):
        rows = slice(c * C, (c + 1) * C)
        u = _gelu_tanh(u_ref[rows, :])
        v = _gelu_tanh(v_ref[rows, :])
        mu = jnp.mean(v, axis=-1, keepdims=True)
        var = jnp.mean(jnp.square(v - mu), axis=-1, keepdims=True)
        v = (v - mu) * lax.rsqrt(var + 1e-5) * lnw_ref[...] + lnb_ref[...]
        for g in range(N_HEADS):
            cols = slice(g * HEAD_DIM, (g + 1) * HEAD_DIM)
            w = jnp.where(causal, ws_ref[g], 0.0)
            vm = _dot(w, v[:, cols]) + bs_ref[:, cols]
            o_ref[rows, cols] = (u[:, cols] * vm).astype(o_ref.dtype)


def chunk_gmlp_mixer(proj, ln_w, ln_b, w_s, b_s, tt=512):
    B, S, _ = proj.shape
    G, C, _ = w_s.shape
    tt = min(tt, S)
    bias = jnp.repeat(b_s.T, HEAD_DIM, axis=1)
    return pl.pallas_call(
        _gmlp_kernel,
        grid=(B, S // tt),
        in_specs=[pl.BlockSpec((None, tt, MIX_W), lambda b, t: (b, t, BLK_GM // 4)),
                  pl.BlockSpec((None, tt, MIX_W), lambda b, t: (b, t, BLK_GM // 4 + 1)),
                  pl.BlockSpec((1, MIX_W), lambda b, t: (0, 0)),
                  pl.BlockSpec((1, MIX_W), lambda b, t: (0, 0)),
                  pl.BlockSpec((G, C, C), lambda b, t: (0, 0, 0)),
                  pl.BlockSpec((C, MIX_W), lambda b, t: (0, 0))],
        out_specs=pl.BlockSpec((None, tt, MIX_W), lambda b, t: (b, t, 0)),
        out_shape=jax.ShapeDtypeStruct((B, S, MIX_W), BF),
        compiler_params=_cparams(("parallel", "parallel")),
        name="chunk_gmlp",
    )(proj, proj, ln_w.reshape(1, MIX_W), ln_b.reshape(1, MIX_W), w_s, bias)


def _merge_kernel(oa_ref, ob_ref, oc_ref, od_ref, ga_ref, gb_ref, gc_ref, gd_ref, wb_ref, wo_ref, x_ref,
                  o_ref, acc_ref):
    j = pl.program_id(1)

    @pl.when(j == 0)
    def _():
        acc_ref[...] = jnp.zeros_like(acc_ref)

    merged = None
    for b, (o_b, g_b) in enumerate(((oa_ref, ga_ref), (ob_ref, gb_ref), (oc_ref, gc_ref), (od_ref, gd_ref))):
        term = _sigmoid(g_b[...]) * jnp.dot(o_b[...], wb_ref[b], preferred_element_type=F32)
        merged = term if merged is None else merged + term
    acc_ref[...] += jnp.dot(merged.astype(BF), wo_ref[...], preferred_element_type=F32)

    @pl.when(j == pl.num_programs(1) - 1)
    def _():
        o_ref[...] = x_ref[...] + acc_ref[...]


def merge_project(x, proj, branches, w_branch, w_out, layer, tm=512, tn=512):
    T, D = x.shape
    tm = min(tm, T)
    per = D // tn
    o_spec = pl.BlockSpec((tm, MIX_W), lambda i, j: (i, 0))
    g_spec = lambda b: pl.BlockSpec((tm, tn), lambda i, j: (i, (BLK_MG * LANE) // tn + b * per + j))
    return pl.pallas_call(
        _merge_kernel,
        grid=(T // tm, per),
        in_specs=[o_spec, o_spec, o_spec, o_spec, g_spec(0), g_spec(1), g_spec(2), g_spec(3),
                  pl.BlockSpec((None, N_BRANCH, MIX_W, tn), lambda i, j: (layer, 0, 0, j)),
                  pl.BlockSpec((None, tn, D), lambda i, j: (layer, j, 0)),
                  pl.BlockSpec((tm, D), lambda i, j: (i, 0))],
        out_specs=pl.BlockSpec((tm, D), lambda i, j: (i, 0)),
        out_shape=jax.ShapeDtypeStruct((T, D), F32),
        scratch_shapes=[pltpu.VMEM((tm, D), F32)],
        compiler_params=_cparams(("parallel", "arbitrary")),
        name="merge_project",
    )(*branches, proj, proj, proj, proj, w_branch, w_out, x)


def _cross_attn_kernel(q_ref, k_ref, v_ref, o_ref):
    dh = q_ref.shape[1] // XA_HEADS
    for h in range(XA_HEADS):
        cols = slice(h * dh, (h + 1) * dh)
        s = _dot_nt(q_ref[:, cols], k_ref[:, cols]) * (dh ** -0.5)
        m = jnp.max(s, axis=-1, keepdims=True)
        p = jnp.exp(s - m)
        p = p / jnp.sum(p, axis=-1, keepdims=True)
        o_ref[:, cols] = _dot(p, v_ref[:, cols]).astype(o_ref.dtype)


def cross_attention_core(q, kv, tq=512):
    B, S, D = q.shape
    M = kv.shape[1]
    tq = min(tq, S)
    return pl.pallas_call(
        _cross_attn_kernel,
        grid=(B, S // tq),
        in_specs=[pl.BlockSpec((None, tq, D), lambda b, i: (b, i, 0)),
                  pl.BlockSpec((None, M, D), lambda b, i: (b, 0, 0)),
                  pl.BlockSpec((None, M, D), lambda b, i: (b, 0, 1))],
        out_specs=pl.BlockSpec((None, tq, D), lambda b, i: (b, i, 0)),
        out_shape=jax.ShapeDtypeStruct((B, S, D), BF),
        compiler_params=_cparams(("parallel", "parallel")),
        name="cross_attention",
    )(q, kv, kv)


def _ffn_kernel(x_ref, nw_ref, wg_ref, wu_ref, wo_ref, o_ref, h_ref, acc_ref):
    j = pl.program_id(1)

    @pl.when(j == 0)
    def _():
        h_ref[...] = _rms(x_ref[...], nw_ref[...], 1e-6).astype(BF)
        acc_ref[...] = jnp.zeros_like(acc_ref)

    h = h_ref[...]
    gate = jnp.dot(h, wg_ref[...], preferred_element_type=F32)
    up = jnp.dot(h, wu_ref[...], preferred_element_type=F32)
    acc_ref[...] += jnp.dot((_silu(gate) * up).astype(BF), wo_ref[...], preferred_element_type=F32)

    @pl.when(j == pl.num_programs(1) - 1)
    def _():
        o_ref[...] = x_ref[...] + acc_ref[...]


def swiglu_ffn(x, norm_w, w_in, w_out, layer, tm=512, tf=512):
    T, D = x.shape
    F = w_out.shape[1]
    tm = min(tm, T)
    nf = F // tf
    return pl.pallas_call(
        _ffn_kernel,
        grid=(T // tm, nf),
        in_specs=[pl.BlockSpec((tm, D), lambda i, j: (i, 0)),
                  pl.BlockSpec((1, D), lambda i, j: (0, 0)),
                  pl.BlockSpec((None, D, tf), lambda i, j: (layer, 0, j)),
                  pl.BlockSpec((None, D, tf), lambda i, j: (layer, 0, nf + j)),
                  pl.BlockSpec((None, tf, D), lambda i, j: (layer, j, 0))],
        out_specs=pl.BlockSpec((tm, D), lambda i, j: (i, 0)),
        out_shape=jax.ShapeDtypeStruct((T, D), F32),
        scratch_shapes=[pltpu.VMEM((tm, D), BF), pltpu.VMEM((tm, D), F32)],
        compiler_params=_cparams(("parallel", "arbitrary")),
        name="swiglu_ffn",
    )(x, norm_w.reshape(1, D), w_in, w_in, w_out)


def _pack_w_in(w):
    n_gate = 3 * N_HEADS
    gate_end = BLK_NG * LANE + n_gate
    wt = jnp.swapaxes(w, 1, 2)
    pad = jnp.zeros((w.shape[0], BLK_GM * LANE - gate_end, w.shape[1]), w.dtype)
    packed = jnp.concatenate([wt[:, :gate_end], pad, wt[:, gate_end:]], axis=1)
    assert packed.shape[1] == N_PROJ, packed.shape
    return packed.astype(BF)


def kernel(x, mem, mem_norm_w, mix_norm_w, w_in, hg_lower_bounds, hg_norm_w, da_lambda_q1, da_lambda_k1, da_lambda_q2, da_lambda_k2, da_norm_w, nsa_pos_k, nsa_cmp_w1_k, nsa_cmp_w2_k, nsa_pos_v, nsa_cmp_w1_v, nsa_cmp_w2_v, gm_ln_w, gm_ln_b, gm_w_s, gm_b_s, w_branch, w_out, xa_norm_w, xa_w_q, xa_w_kv, xa_w_o, ffn_norm_w, ffn_w_in, ffn_w_out, final_norm_w):
    B, S, D = x.shape
    M = mem.shape[1]
    depth = w_in.shape[0]
    assert D == BLK_MG * LANE // 3 and S % SWA_WINDOW == 0, (S, D)
    T = B * S
    x = x.reshape(T, D)
    mem2 = mem.reshape(B * M, D)
    w_in_p = _pack_w_in(w_in)
    w_branch_b, w_out_b = w_branch.astype(BF), w_out.astype(BF)
    xa_w_q_b, xa_w_kv_b, xa_w_o_b = xa_w_q.astype(BF), xa_w_kv.astype(BF), xa_w_o.astype(BF)
    ffn_w_in_b, ffn_w_out_b = ffn_w_in.astype(BF), ffn_w_out.astype(BF)
    for l in range(depth):
        lambda_init = 0.8 - 0.6 * math.exp(-0.3 * l)
        proj = norm_matmul(x, mix_norm_w[l], w_in_p, l, F32, tm=1024, w_is_transposed=True).reshape(B, S, N_PROJ)
        o_a = hgrn2_mixer(proj, hg_lower_bounds, hg_norm_w[l], l)
        lam_params = jnp.stack([da_lambda_q1[l], da_lambda_k1[l], da_lambda_q2[l], da_lambda_k2[l]])
        o_b = diff_attention_mixer(proj, lam_params, da_norm_w[l], lambda_init)
        nkv = lambda off: proj[:, :, (BLK_NKV + off) * LANE:(BLK_NKV + off + 1) * LANE]
        kc = compress_tokens(nkv(0), nsa_pos_k[l], nsa_cmp_w1_k[l].astype(BF), nsa_cmp_w2_k[l].astype(BF))
        vc = compress_tokens(nkv(1), nsa_pos_v[l], nsa_cmp_w1_v[l].astype(BF), nsa_cmp_w2_v[l].astype(BF))
        o_cmp, sel = nsa_compressed_and_select(proj, kc, vc)
        o_slc = nsa_selected_attention(proj, sel)
        o_c = nsa_window_and_combine(proj, o_cmp, o_slc)
        o_d = chunk_gmlp_mixer(proj, gm_ln_w[l], gm_ln_b[l], gm_w_s[l], gm_b_s[l])
        branches = [o.reshape(T, MIX_W) for o in (o_a, o_b, o_c, o_d)]
        x = merge_project(x, proj.reshape(T, N_PROJ), branches, w_branch_b, w_out_b, l)
        q = norm_matmul(x, xa_norm_w[l], xa_w_q_b, l, BF, tm=1024).reshape(B, S, D)
        kv = norm_matmul(mem2, mem_norm_w, xa_w_kv_b, l, F32).reshape(B, M, 2 * D)
        att = cross_attention_core(q, kv).reshape(T, D)
        x = matmul_residual(att, xa_w_o_b, l, x, tm=1024)
        x = swiglu_ffn(x, ffn_norm_w[l], ffn_w_in_b, ffn_w_out_b, l)
    return rmsnorm(x, final_norm_w).reshape(B, S, D)
```

```python
import functools
import math

import jax
import jax.numpy as jnp
from jax import lax
from jax.experimental import pallas as pl
from jax.experimental.pallas import tpu as pltpu

BF = jnp.bfloat16
F32 = jnp.float32

LANE = 128
HEAD_DIM = 128
N_HEADS = 4
MIX_W = N_HEADS * HEAD_DIM
N_BRANCH = 4
HG_CHUNK = 64
HG_SUB = 16
HG_EXP_CLAMP = 80.0
CMP_LEN = 32
CMP_STRIDE = 16
SLC_BLOCK = 64
SLC_TOPK = 16
N_LOCAL_BLOCKS = 2
SWA_WINDOW = 512
XA_HEADS = 4
NEG = -1e30
TINY = 1e-30
FORCE_SCORE = 1e4
LOG2E = math.log2(math.e)
FLASH_MM_ROWS = 256
VMEM_LIMIT = 56 * 1024 * 1024

BLK_HG = 0
BLK_DA = 16
BLK_NQ = 28
BLK_NKV = 32
BLK_NG = 38
BLK_GM = 40
BLK_MG = 48
N_PROJ = 112 * LANE
N_MIXER_IN = BLK_MG * LANE


def _cparams(sem):
    return pltpu.CompilerParams(dimension_semantics=sem, vmem_limit_bytes=VMEM_LIMIT)


def _dot(a, b):
    return jnp.dot(a.astype(BF), b.astype(BF), preferred_element_type=F32)


def _dot_nt(a, b):
    return lax.dot_general(a.astype(BF), b.astype(BF), (((1,), (1,)), ((), ())),
                           preferred_element_type=F32)


def _dot_tn(a, b):
    return lax.dot_general(a.astype(BF), b.astype(BF), (((0,), (0,)), ((), ())),
                           preferred_element_type=F32)


def _split3(a):
    hi = a.astype(BF)
    r = a - hi.astype(F32)
    mid = r.astype(BF)
    lo = (r - mid.astype(F32)).astype(BF)
    return hi, mid, lo


def _dot_exact_rhs(a_bf, b):
    hi, mid, lo = _split3(b)
    f = lambda t: jnp.dot(a_bf, t, preferred_element_type=F32)
    return f(hi) + f(mid) + f(lo)


def _dot_exact_lhs(a, b_bf):
    hi, mid, lo = _split3(a)
    f = lambda t: jnp.dot(t, b_bf, preferred_element_type=F32)
    return f(hi) + f(mid) + f(lo)


def _rms(x, w, eps):
    return x * lax.rsqrt(jnp.mean(x * x, axis=-1, keepdims=True) + eps) * w


def _sigmoid(x):
    return 1.0 / (1.0 + jnp.exp(-x))


def _silu(x):
    return x * _sigmoid(x)


def _norm_matmul_kernel(w_is_transposed, n_head_tiles, x_ref, nw_ref, w_ref, o_ref, *rest):
    tail_ref = rest[0] if n_head_tiles is not None else None
    h_ref = rest[-1]
    j = pl.program_id(1)

    @pl.when(j == 0)
    def _():
        h_ref[...] = _rms(x_ref[...], nw_ref[...], 1e-6).astype(BF)

    def product():
        if w_is_transposed:
            return lax.dot_general(h_ref[...], w_ref[...], (((1,), (1,)), ((), ())), preferred_element_type=F32)
        return jnp.dot(h_ref[...], w_ref[...], preferred_element_type=F32)

    if n_head_tiles is None:
        o_ref[...] = product().astype(o_ref.dtype)
    else:
        @pl.when(j < n_head_tiles)
        def _():
            o_ref[...] = product().astype(o_ref.dtype)

        @pl.when(j >= n_head_tiles)
        def _():
            tail_ref[...] = product().astype(tail_ref.dtype)


def norm_matmul(x, norm_w, w, layer, out_dtype, tm=512, tn=1024, w_is_transposed=False, tail=None):
    T, D = x.shape
    N = w.shape[1] if w_is_transposed else w.shape[2]
    tm, tn = min(tm, T), min(tn, N)
    w_spec = (pl.BlockSpec((None, tn, D), lambda i, j: (layer, j, 0)) if w_is_transposed
              else pl.BlockSpec((None, D, tn), lambda i, j: (layer, 0, j)))
    if tail is None:
        nh = None
        out_specs = pl.BlockSpec((tm, tn), lambda i, j: (i, j))
        out_shape = jax.ShapeDtypeStruct((T, N), out_dtype)
    else:
        nh = (N - tail[0]) // tn
        out_specs = [pl.BlockSpec((tm, tn), lambda i, j: (i, jnp.minimum(j, nh - 1))),
                     pl.BlockSpec((tm, tn), lambda i, j: (i, jnp.maximum(j - nh, 0)))]
        out_shape = [jax.ShapeDtypeStruct((T, N - tail[0]), out_dtype), jax.ShapeDtypeStruct((T, tail[0]), tail[1])]
    return pl.pallas_call(
        functools.partial(_norm_matmul_kernel, w_is_transposed, nh),
        grid=(T // tm, N // tn),
        in_specs=[pl.BlockSpec((tm, D), lambda i, j: (i, 0)),
                  pl.BlockSpec((1, D), lambda i, j: (0, 0)),
                  w_spec],
        out_specs=out_specs,
        out_shape=out_shape,
        scratch_shapes=[pltpu.VMEM((tm, D), BF)],
        compiler_params=_cparams(("parallel", "arbitrary")),
        name="norm_matmul",
    )(x, norm_w.reshape(1, D), w)


def _matmul_residual_kernel(a_ref, w_ref, r_ref, o_ref):
    o_ref[...] = r_ref[...] + jnp.dot(a_ref[...], w_ref[...], preferred_element_type=F32)


def matmul_residual(a, w, layer, r, tm=512, tn=1024):
    T, K = a.shape
    N = w.shape[2]
    tm, tn = min(tm, T), min(tn, N)
    return pl.pallas_call(
        _matmul_residual_kernel,
        grid=(T // tm, N // tn),
        in_specs=[pl.BlockSpec((tm, K), lambda i, j: (i, 0)),
                  pl.BlockSpec((None, K, tn), lambda i, j: (layer, 0, j)),
                  pl.BlockSpec((tm, tn), lambda i, j: (i, j))],
        out_specs=pl.BlockSpec((tm, tn), lambda i, j: (i, j)),
        out_shape=jax.ShapeDtypeStruct((T, N), F32),
        compiler_params=_cparams(("parallel", "arbitrary")),
        name="matmul_residual",
    )(a, w, r)


def _rmsnorm_kernel(x_ref, w_ref, o_ref):
    o_ref[...] = _rms(x_ref[...], w_ref[...], 1e-6)


def rmsnorm(x, w, tm=512):
    T, D = x.shape
    tm = min(tm, T)
    return pl.pallas_call(
        _rmsnorm_kernel,
        grid=(T // tm,),
        in_specs=[pl.BlockSpec((tm, D), lambda i: (i, 0)),
                  pl.BlockSpec((1, D), lambda i: (0, 0))],
        out_specs=pl.BlockSpec((tm, D), lambda i: (i, 0)),
        out_shape=jax.ShapeDtypeStruct((T, D), F32),
        compiler_params=_cparams(("parallel",)),
        name="final_rmsnorm",
    )(x, w.reshape(1, D))


def _hgrn2_kernel(layer, q_ref, f_ref, i_ref, g_ref, lb_ref, nw_ref, o_ref, state_ref):
    @pl.when(pl.program_id(1) == 0)
    def _():
        state_ref[...] = jnp.zeros_like(state_ref)

    lb_all = lb_ref[...]
    e = jnp.exp(lb_all - jnp.max(lb_all, axis=0, keepdims=True))
    soft = e / jnp.sum(e, axis=0, keepdims=True)
    lrow = lax.broadcasted_iota(jnp.int32, soft.shape, 0)
    lower_all = jnp.sum(jnp.where((lrow >= 1) & (lrow <= layer), soft, 0.0), axis=0, keepdims=True)

    C, SUB = HG_CHUNK, HG_SUB
    tt = q_ref.shape[0]
    row = lax.broadcasted_iota(jnp.int32, (C, C), 0)
    col = lax.broadcasted_iota(jnp.int32, (C, C), 1)
    tri = jnp.where(col <= row, 1.0, 0.0).astype(BF)
    tri_sub = jnp.where(col < (row // SUB) * SUB, 1.0, 0.0).astype(BF)
    tri2 = jnp.concatenate([tri, tri_sub], axis=0)
    nsub = C // SUB
    mrow = lax.broadcasted_iota(jnp.int32, (C, nsub * C), 0)
    mcol = lax.broadcasted_iota(jnp.int32, (C, nsub * C), 1)
    keep = jnp.where(mcol // C == mrow // SUB, mcol % C, C) <= mrow
    w = MIX_W

    def chunk(c, states):
        rows = slice(c * C, (c + 1) * C)
        f_all = lower_all + (1.0 - lower_all) * _sigmoid(f_ref[rows, :])
        log_f = jnp.log(f_all)
        hi, mid, lo = _split3(log_f)
        cum = jnp.dot(tri2, jnp.concatenate([hi, mid, lo], axis=1), preferred_element_type=F32)
        cum = cum[:, 0:w] + cum[:, w:2 * w] + cum[:, 2 * w:3 * w]
        b_all, b_sub_all = cum[0:C], cum[C:2 * C]
        b_end_all = jnp.sum(log_f, axis=0, keepdims=True)
        new_states = []
        for h in range(N_HEADS):
            cols = slice(h * HEAD_DIM, (h + 1) * HEAD_DIM)
            b, b_sub, b_end = b_all[:, cols], b_sub_all[:, cols], b_end_all[:, cols]
            q = _silu(q_ref[rows, cols])
            k = 1.0 - f_all[:, cols]
            v = i_ref[rows, cols]
            o_inter = _dot_nt(q * jnp.exp(b), states[h])
            qt = q * jnp.exp(b - b_sub)
            kt = jnp.concatenate(
                [k * jnp.exp(jnp.minimum(b_sub[s * SUB:s * SUB + 1] - b, HG_EXP_CLAMP)) for s in range(nsub)],
                axis=0)
            sc = jnp.where(keep, _dot_nt(qt, kt), 0.0)
            o = o_inter + _dot(sc, jnp.concatenate([v] * nsub, axis=0))
            o = _rms(o, nw_ref[...], 1e-6) * _silu(g_ref[rows, cols])
            o_ref[rows, cols] = o.astype(o_ref.dtype)
            new_states.append(states[h] * jnp.exp(b_end) + _dot_tn(v, k * jnp.exp(b_end - b)))
        return new_states

    states = [state_ref[h] for h in range(N_HEADS)]
    for c in range(tt // C):
        states = chunk(c, states)
    for h in range(N_HEADS):
        state_ref[h] = states[h]


def hgrn2_mixer(proj, lower_bounds, norm_w, layer, tt=256):
    B, S, _ = proj.shape
    L = lower_bounds.shape[0]
    tt = min(tt, S)
    seg = lambda off: pl.BlockSpec((None, tt, MIX_W), lambda b, t: (b, t, (BLK_HG + off) // 4))
    return pl.pallas_call(
        functools.partial(_hgrn2_kernel, layer),
        grid=(B, S // tt),
        in_specs=[seg(0), seg(4), seg(8), seg(12),
                  pl.BlockSpec((L, MIX_W), lambda b, t: (0, 0)),
                  pl.BlockSpec((1, LANE), lambda b, t: (0, 0))],
        out_specs=pl.BlockSpec((None, tt, MIX_W), lambda b, t: (b, t, 0)),
        out_shape=jax.ShapeDtypeStruct((B, S, MIX_W), BF),
        scratch_shapes=[pltpu.VMEM((N_HEADS, HEAD_DIM, HEAD_DIM), F32)],
        compiler_params=_cparams(("parallel", "arbitrary")),
        name="hgrn2",
    )(proj, proj, proj, proj, lower_bounds, norm_w.reshape(1, LANE))


def _flash_init(m_ref, acc_ref, vx_ref):
    m_ref[...] = jnp.full_like(m_ref, NEG)
    acc_ref[...] = jnp.zeros_like(acc_ref)
    vx_ref[:, LANE:2 * LANE] = jnp.ones((vx_ref.shape[0], LANE), BF)


def _flash_step(qs_ref, k_ref, v_ref, vx_ref, s_ref, p_ref, m_ref, acc_ref, rb, bias_fn):
    nrows, tk = s_ref.shape
    k = k_ref[...].astype(BF)
    vx_ref[:, 0:LANE] = v_ref[...].astype(BF)
    for c in range(nrows // FLASH_MM_ROWS):
        rows = slice(c * FLASH_MM_ROWS, (c + 1) * FLASH_MM_ROWS)
        s_ref[rows, :] = _dot_nt(qs_ref[rows, :], k)
    for r in range(nrows // rb):
        rows = slice(r * rb, (r + 1) * rb)
        s = s_ref[rows, :]
        if bias_fn is not None:
            s = s + bias_fn(r * rb)
        m_prev = m_ref[rows, :]
        m_new = jnp.maximum(m_prev, jnp.max(s, axis=-1, keepdims=True))
        p_ref[rows, :] = jnp.exp2(s - jnp.concatenate([m_new] * (tk // LANE), axis=1)).astype(BF)
        alpha = jnp.exp2(m_prev - m_new)
        acc_ref[rows, :] = acc_ref[rows, :] * jnp.concatenate([alpha, alpha], axis=1)
        m_ref[rows, :] = m_new
    vx = vx_ref[...]
    for c in range(nrows // FLASH_MM_ROWS):
        rows = slice(c * FLASH_MM_ROWS, (c + 1) * FLASH_MM_ROWS)
        acc_ref[rows, :] += jnp.dot(p_ref[rows, :], vx, preferred_element_type=F32)


def _diff_attn_kernel(lambda_init, tq, tk, rb, q_ref, k_ref, v_ref, lam_ref, nw_ref, o_ref,
                      qs_ref, m_ref, acc_ref, vx_ref, s_ref, p_ref):
    i, j = pl.program_id(2), pl.program_id(3)
    dc = HEAD_DIM // 2

    @pl.when(j == 0)
    def _():
        q = q_ref[...] * (dc ** -0.5 * LOG2E)
        first = lax.broadcasted_iota(jnp.int32, q.shape, 1) < dc
        qs_ref[0:tq, :] = jnp.where(first, q, 0.0).astype(BF)
        qs_ref[tq:2 * tq, :] = jnp.where(first, 0.0, q).astype(BF)
        _flash_init(m_ref, acc_ref, vx_ref)

    def causal_bias(r0):
        r = r0 + lax.broadcasted_iota(jnp.int32, (rb, tk), 0)
        qloc = jnp.where(r >= tq, r - tq, r)
        col = lax.broadcasted_iota(jnp.int32, (rb, tk), 1)
        return jnp.where(col - qloc <= i * tq - j * tk, 0.0, NEG)

    needed = j * tk <= i * tq + tq - 1
    unmasked = (j + 1) * tk - 1 <= i * tq

    @pl.when(needed & unmasked)
    def _():
        _flash_step(qs_ref, k_ref, v_ref, vx_ref, s_ref, p_ref, m_ref, acc_ref, rb, None)

    @pl.when(needed & jnp.logical_not(unmasked))
    def _():
        _flash_step(qs_ref, k_ref, v_ref, vx_ref, s_ref, p_ref, m_ref, acc_ref, rb, causal_bias)

    @pl.when(j == pl.num_programs(3) - 1)
    def _():
        lam = (jnp.exp(jnp.sum(lam_ref[0:1] * lam_ref[1:2], axis=-1, keepdims=True))
               - jnp.exp(jnp.sum(lam_ref[2:3] * lam_ref[3:4], axis=-1, keepdims=True)) + lambda_init)
        o = acc_ref[:, 0:LANE] / acc_ref[:, LANE:2 * LANE]
        o = o[0:tq] - lam * o[tq:2 * tq]
        o = _rms(o, nw_ref[...], 1e-6) * (1.0 - lambda_init)
        o_ref[...] = o.astype(o_ref.dtype)


def diff_attention_mixer(proj, lam_params, norm_w, lambda_init, tq=1024, tk=1024, rb=16):
    B, S, _ = proj.shape
    tq, tk = min(tq, S), min(tk, S)
    last = lambda i: (i * tq + tq - 1) // tk
    return pl.pallas_call(
        functools.partial(_diff_attn_kernel, lambda_init, tq, tk, rb),
        grid=(B, N_HEADS, S // tq, S // tk),
        in_specs=[pl.BlockSpec((None, tq, LANE), lambda b, h, i, j: (b, i, BLK_DA + h)),
                  pl.BlockSpec((None, tk, LANE),
                               lambda b, h, i, j: (b, jnp.minimum(j, last(i)), BLK_DA + 4 + h)),
                  pl.BlockSpec((None, tk, LANE),
                               lambda b, h, i, j: (b, jnp.minimum(j, last(i)), BLK_DA + 8 + h)),
                  pl.BlockSpec((4, HEAD_DIM // 2), lambda b, h, i, j: (0, 0)),
                  pl.BlockSpec((1, LANE), lambda b, h, i, j: (0, 0))],
        out_specs=pl.BlockSpec((None, tq, LANE), lambda b, h, i, j: (b, i, h)),
        out_shape=jax.ShapeDtypeStruct((B, S, MIX_W), BF),
        scratch_shapes=[pltpu.VMEM((2 * tq, LANE), BF),
                        pltpu.VMEM((2 * tq, LANE), F32),
                        pltpu.VMEM((2 * tq, 2 * LANE), F32),
                        pltpu.VMEM((tk, 2 * LANE), BF),
                        pltpu.VMEM((2 * tq, tk), F32),
                        pltpu.VMEM((2 * tq, tk), BF)],
        compiler_params=_cparams(("parallel", "parallel", "parallel", "arbitrary")),
        name="diff_attention",
    )(proj, proj, proj, lam_params, norm_w.reshape(1, LANE))


def _compress_kernel(t_ref, pos_ref, w1_ref, w2_ref, o_ref):
    half = t_ref.shape[1]
    t = t_ref[...]
    a = _dot(t, w1_ref[0:half, :])
    bnext = _dot(t, w1_ref[half:2 * half, :])
    c0 = _dot(pos_ref[...], w1_ref[...])
    n = a.shape[0]
    hidden = a + pltpu.roll(bnext, n - 1, 0) + c0
    o_ref[...] = _dot(_silu(hidden), w2_ref[...])


def compress_tokens(t, pos, w1, w2):
    B, S, d = t.shape
    n_chunks = S // CMP_STRIDE
    hid = w1.shape[1]
    t = t.reshape(B, n_chunks, CMP_STRIDE * d)
    return pl.pallas_call(
        _compress_kernel,
        grid=(B,),
        in_specs=[pl.BlockSpec((None, n_chunks, CMP_STRIDE * d), lambda b: (b, 0, 0)),
                  pl.BlockSpec((1, CMP_LEN * d), lambda b: (0, 0)),
                  pl.BlockSpec((CMP_LEN * d, hid), lambda b: (0, 0)),
                  pl.BlockSpec((hid, d), lambda b: (0, 0))],
        out_specs=pl.BlockSpec((None, n_chunks, d), lambda b: (b, 0, 0)),
        out_shape=jax.ShapeDtypeStruct((B, n_chunks, d), F32),
        compiler_params=_cparams(("parallel",)),
        name="nsa_compress",
    )(t, pos.reshape(1, CMP_LEN * d), w1, w2)


def _cmp_select_kernel(tq, rb, q_ref, kc_ref, vc_ref, pool_ref, o_ref, sel_ref, s_ref, p_ref, imp_ref, bias_ref):
    i = pl.program_id(1)
    ncp = kc_ref.shape[0]
    n_slc = sel_ref.shape[1]
    scale = HEAD_DIM ** -0.5
    qpos = i * tq + lax.broadcasted_iota(jnp.int32, (tq, 1), 0)
    cmp_end = lax.broadcasted_iota(jnp.int32, (1, ncp), 1) * CMP_STRIDE + (CMP_LEN - 1)
    bias_ref[...] = jnp.where(cmp_end <= qpos, 0.0, NEG)
    kc = kc_ref[...].astype(BF)
    vc = vc_ref[...].astype(BF)
    for h in range(N_HEADS):
        cols = slice(h * HEAD_DIM, (h + 1) * HEAD_DIM)
        slot = h % 2
        s_ref[slot] = _dot_nt(q_ref[:, cols] * scale, kc)
        for r in range(tq // rb):
            rows = slice(r * rb, (r + 1) * rb)
            s = s_ref[slot, rows, :] + bias_ref[rows, :]
            e = jnp.exp(s - jnp.max(s, axis=-1, keepdims=True))
            seen = i * tq + r * rb + lax.broadcasted_iota(jnp.int32, (rb, 1), 0) >= CMP_LEN - 1
            p = e * jnp.where(seen, 1.0 / jnp.sum(e, axis=-1, keepdims=True), 0.0)
            p_ref[slot, rows, :] = p.astype(BF)
            imp_ref[rows, :] = p if h == 0 else imp_ref[rows, :] + p
        o_ref[:, cols] = jnp.dot(p_ref[slot], vc, preferred_element_type=F32)
    imp = imp_ref[...]
    pool_t = pool_ref[...]
    imp_hi, imp_mid, _ = _split3(imp)
    imp_slc = _dot_nt(pool_t, imp_hi) + _dot_nt(pool_t, imp_mid)
    blk = lax.broadcasted_iota(jnp.int32, (n_slc, 1), 0)
    cur = (i * tq + lax.broadcasted_iota(jnp.int32, (1, tq), 1)) // SLC_BLOCK
    valid = blk <= cur
    forced = (blk == 0) | (blk > cur - N_LOCAL_BLOCKS)
    n_forced = 1 + N_LOCAL_BLOCKS
    score = jnp.where(valid, jnp.where(forced, -2.0, imp_slc), -1.0)
    blk_f = blk.astype(F32)
    for _ in range(min(SLC_TOPK, n_slc) - n_forced):
        m = jnp.max(score, axis=0, keepdims=True)
        first = jnp.min(jnp.where(score == m, blk_f, float(n_slc)), axis=0, keepdims=True)
        score = jnp.where(blk_f == first, -2.0, score)
    sel_ref[...] = jnp.where(score == -2.0, 1.0, 0.0).T.astype(sel_ref.dtype)


def nsa_compressed_and_select(proj, kc, vc, tq=256, rb=16):
    B, S, _ = proj.shape
    ncp = kc.shape[1]
    n_slc = S // SLC_BLOCK
    n = jnp.arange(ncp)[:, None]
    jj = jnp.arange(n_slc)[None, :]
    w = n - (SLC_BLOCK // CMP_STRIDE) * jj + 1
    pool = jnp.where((w >= 0) & (w <= 4), jnp.where((w == 0) | (w == 4), 1.0, 2.0), 0.0).astype(BF).T
    return pl.pallas_call(
        functools.partial(_cmp_select_kernel, tq, rb),
        grid=(B, S // tq),
        in_specs=[pl.BlockSpec((None, tq, MIX_W), lambda b, i: (b, i, BLK_NQ // 4)),
                  pl.BlockSpec((None, ncp, HEAD_DIM), lambda b, i: (b, 0, 0)),
                  pl.BlockSpec((None, ncp, HEAD_DIM), lambda b, i: (b, 0, 0)),
                  pl.BlockSpec((n_slc, ncp), lambda b, i: (0, 0))],
        out_specs=[pl.BlockSpec((None, tq, MIX_W), lambda b, i: (b, i, 0)),
                   pl.BlockSpec((None, tq, n_slc), lambda b, i: (b, i, 0))],
        out_shape=[jax.ShapeDtypeStruct((B, S, MIX_W), F32),
                   jax.ShapeDtypeStruct((B, S, n_slc), BF)],
        scratch_shapes=[pltpu.VMEM((2, tq, ncp), F32), pltpu.VMEM((2, tq, ncp), BF),
                        pltpu.VMEM((tq, ncp), F32), pltpu.VMEM((tq, ncp), F32)],
        compiler_params=_cparams(("parallel", "parallel")),
        name="nsa_cmp_select",
    )(proj, kc, vc, pool)


def _slc_attn_kernel(tq, tk, rb, q_ref, k_ref, v_ref, sel_ref, exp_ref, o_ref,
                     qs_ref, m_ref, acc_ref, vx_ref, s_ref, p_ref, bias_ref):
    i, j = pl.program_id(1), pl.program_id(2)

    @pl.when(j == 0)
    def _():
        for h in range(N_HEADS):
            q = q_ref[:, h * HEAD_DIM:(h + 1) * HEAD_DIM] * (HEAD_DIM ** -0.5 * LOG2E)
            qs_ref[h * tq:(h + 1) * tq, :] = q.astype(BF)
        _flash_init(m_ref, acc_ref, vx_ref)

    @pl.when(j * tk <= i * tq + tq - 1)
    def _():
        picked = jnp.dot(sel_ref[...], exp_ref[...], preferred_element_type=F32)
        qpos = i * tq + lax.broadcasted_iota(jnp.int32, (tq, tk), 0)
        kpos = j * tk + lax.broadcasted_iota(jnp.int32, (tq, tk), 1)
        bias_ref[...] = jnp.where(kpos <= qpos, (picked - 1.0) * (-NEG), NEG)
        _flash_step(qs_ref, k_ref, v_ref, vx_ref, s_ref, p_ref, m_ref, acc_ref, rb,
                    lambda r0: bias_ref[(r0 % tq):(r0 % tq) + rb, :])

    @pl.when(j == pl.num_programs(2) - 1)
    def _():
        for h in range(N_HEADS):
            rows = slice(h * tq, (h + 1) * tq)
            o_ref[:, h * HEAD_DIM:(h + 1) * HEAD_DIM] = acc_ref[rows, 0:LANE] / acc_ref[rows, LANE:2 * LANE]


def nsa_selected_attention(proj, sel, tq=512, tk=1024, rb=16):
    B, S, _ = proj.shape
    tq, tk = min(tq, S), min(tk, S)
    n_slc = S // SLC_BLOCK
    expand = (jnp.arange(n_slc)[:, None] == (jnp.arange(S)[None, :] // SLC_BLOCK)).astype(BF)
    last = lambda i: (i * tq + tq - 1) // tk
    kv = lambda off: pl.BlockSpec((None, tk, LANE),
                                  lambda b, i, j: (b, jnp.minimum(j, last(i)), BLK_NKV + off))
    rows = N_HEADS * tq
    return pl.pallas_call(
        functools.partial(_slc_attn_kernel, tq, tk, rb),
        grid=(B, S // tq, S // tk),
        in_specs=[pl.BlockSpec((None, tq, MIX_W), lambda b, i, j: (b, i, BLK_NQ // 4)),
                  kv(2), kv(3),
                  pl.BlockSpec((None, tq, n_slc), lambda b, i, j: (b, i, 0)),
                  pl.BlockSpec((n_slc, tk), lambda b, i, j: (0, jnp.minimum(j, last(i))))],
        out_specs=pl.BlockSpec((None, tq, MIX_W), lambda b, i, j: (b, i, 0)),
        out_shape=jax.ShapeDtypeStruct((B, S, MIX_W), F32),
        scratch_shapes=[pltpu.VMEM((rows, LANE), BF),
                        pltpu.VMEM((rows, LANE), F32),
                        pltpu.VMEM((rows, 2 * LANE), F32),
                        pltpu.VMEM((tk, 2 * LANE), BF),
                        pltpu.VMEM((rows, tk), F32),
                        pltpu.VMEM((rows, tk), BF),
                        pltpu.VMEM((tq, tk), F32)],
        compiler_params=_cparams(("parallel", "parallel", "arbitrary")),
        name="nsa_selected",
    )(proj, proj, proj, sel, expand)


def _swa_combine_kernel(tq, rb, q_ref, kp_ref, kc_ref, vp_ref, vc_ref, g_ref, ocmp_ref, oslc_ref, bias_ref, o_ref,
                        s_ref, p_ref):
    scale = HEAD_DIM ** -0.5
    k = jnp.concatenate([kp_ref[...], kc_ref[...]], axis=0).astype(BF)
    v = jnp.concatenate([vp_ref[...], vc_ref[...]], axis=0).astype(BF)
    gates = _sigmoid(g_ref[...])
    for h in range(N_HEADS):
        cols = slice(h * HEAD_DIM, (h + 1) * HEAD_DIM)
        slot = h % 2
        s_ref[slot] = _dot_nt(q_ref[:, cols] * scale, k)
        for r in range(tq // rb):
            rows = slice(r * rb, (r + 1) * rb)
            s = s_ref[slot, rows, :] + bias_ref[rows, :]
            e = jnp.exp(s - jnp.max(s, axis=-1, keepdims=True))
            p_ref[slot, rows, :] = (e * (1.0 / jnp.sum(e, axis=-1, keepdims=True))).astype(BF)
        o_swa = jnp.dot(p_ref[slot], v, preferred_element_type=F32)
        o = (gates[:, 3 * h:3 * h + 1] * ocmp_ref[:, cols]
             + gates[:, 3 * h + 1:3 * h + 2] * oslc_ref[:, cols]
             + gates[:, 3 * h + 2:3 * h + 3] * o_swa)
        o_ref[:, cols] = o.astype(o_ref.dtype)


def nsa_window_and_combine(proj, o_cmp, o_slc):
    B, S, _ = proj.shape
    tq = SWA_WINDOW
    kvp = lambda off: pl.BlockSpec((None, tq, LANE), lambda b, i: (b, jnp.maximum(i - 1, 0), BLK_NKV + off))
    kvc = lambda off: pl.BlockSpec((None, tq, LANE), lambda b, i: (b, i, BLK_NKV + off))
    full = lambda: pl.BlockSpec((None, tq, MIX_W), lambda b, i: (b, i, 0))
    dist = jnp.arange(tq)[:, None] - (jnp.arange(SWA_WINDOW + tq)[None, :] - SWA_WINDOW)
    inside = (dist >= 0) & (dist < SWA_WINDOW)
    first = inside & (jnp.arange(SWA_WINDOW + tq)[None, :] >= SWA_WINDOW)
    bias = jnp.where(jnp.stack([first, inside]), 0.0, NEG).astype(F32)
    return pl.pallas_call(
        functools.partial(_swa_combine_kernel, tq, 16),
        grid=(B, S // tq),
        in_specs=[pl.BlockSpec((None, tq, MIX_W), lambda b, i: (b, i, BLK_NQ // 4)),
                  kvp(4), kvc(4), kvp(5), kvc(5),
                  pl.BlockSpec((None, tq, LANE), lambda b, i: (b, i, BLK_NG)),
                  full(), full(),
                  pl.BlockSpec((None, tq, SWA_WINDOW + tq), lambda b, i: (jnp.minimum(i, 1), 0, 0))],
        out_specs=full(),
        out_shape=jax.ShapeDtypeStruct((B, S, MIX_W), BF),
        scratch_shapes=[pltpu.VMEM((2, tq, SWA_WINDOW + tq), F32), pltpu.VMEM((2, tq, SWA_WINDOW + tq), BF)],
        compiler_params=_cparams(("parallel", "parallel")),
        name="nsa_window_combine",
    )(proj, proj, proj, proj, proj, proj, o_cmp, o_slc, bias)


def _gelu_tanh(x):
    return 0.5 * x * (1.0 + jnp.tanh(math.sqrt(2.0 / math.pi) * (x + 0.044715 * (x * x * x))))


def _gmlp_kernel(u_ref, v_ref, lnw_ref, lnb_ref, ws_ref, bs_ref, o_ref):
    C = ws_ref.shape[1]
    row = lax.broadcasted_iota(jnp.int32, (C, C), 0)
    col = lax.broadcasted_iota(jnp.int32, (C, C), 1)
    causal = col <= row
    for c in range(u_ref.shape[0] // C):
        rows = slice(c * C, (c + 1) * C)
        u = _gelu_tanh(u_ref[rows, :])
        v = _gelu_tanh(v_ref[rows, :])
        mu = jnp.mean(v, axis=-1, keepdims=True)
        var = jnp.mean(jnp.square(v - mu), axis=-1, keepdims=True)
        v = (v - mu) * lax.rsqrt(var + 1e-5) * lnw_ref[...] + lnb_ref[...]
        for g in range(N_HEADS):
            cols = slice(g * HEAD_DIM, (g + 1) * HEAD_DIM)
            w = jnp.where(causal, ws_ref[g], 0.0)
            vm = _dot(w, v[:, cols]) + bs_ref[:, cols]
            o_ref[rows, cols] = (u[:, cols] * vm).astype(o_ref.dtype)


def chunk_gmlp_mixer(proj, ln_w, ln_b, w_s, b_s, tt=512):
    B, S, _ = proj.shape
    G, C, _ = w_s.shape
    tt = min(tt, S)
    bias = jnp.repeat(b_s.T, HEAD_DIM, axis=1)
    return pl.pallas_call(
        _gmlp_kernel,
        grid=(B, S // tt),
        in_specs=[pl.BlockSpec((None, tt, MIX_W), lambda b, t: (b, t, BLK_GM // 4)),
                  pl.BlockSpec((None, tt, MIX_W), lambda b, t: (b, t, BLK_GM // 4 + 1)),
                  pl.BlockSpec((1, MIX_W), lambda b, t: (0, 0)),
                  pl.BlockSpec((1, MIX_W), lambda b, t: (0, 0)),
                  pl.BlockSpec((G, C, C), lambda b, t: (0, 0, 0)),
                  pl.BlockSpec((C, MIX_W), lambda b, t: (0, 0))],
        out_specs=pl.BlockSpec((None, tt, MIX_W), lambda b, t: (b, t, 0)),
        out_shape=jax.ShapeDtypeStruct((B, S, MIX_W), BF),
        compiler_params=_cparams(("parallel", "parallel")),
        name="chunk_gmlp",
    )(proj, proj, ln_w.reshape(1, MIX_W), ln_b.reshape(1, MIX_W), w_s, bias)


def _merge_kernel(oa_ref, ob_ref, oc_ref, od_ref, ga_ref, gb_ref, gc_ref, gd_ref, wb_ref, wo_ref, x_ref,
                  o_ref, acc_ref):
    j = pl.program_id(1)

    @pl.when(j == 0)
    def _():
        acc_ref[...] = jnp.zeros_like(acc_ref)

    merged = None
    for b, (o_b, g_b) in enumerate(((oa_ref, ga_ref), (ob_ref, gb_ref), (oc_ref, gc_ref), (od_ref, gd_ref))):
        term = _sigmoid(g_b[...].astype(F32)) * jnp.dot(o_b[...], wb_ref[b], preferred_element_type=F32)
        merged = term if merged is None else merged + term
    acc_ref[...] += jnp.dot(merged.astype(BF), wo_ref[...], preferred_element_type=F32)

    @pl.when(j == pl.num_programs(1) - 1)
    def _():
        o_ref[...] = x_ref[...] + acc_ref[...]


def merge_project(x, gate_logits, branches, w_branch, w_out, layer, tm=512, tn=512):
    T, D = x.shape
    tm = min(tm, T)
    per = D // tn
    o_spec = pl.BlockSpec((tm, MIX_W), lambda i, j: (i, 0))
    g_spec = lambda b: pl.BlockSpec((tm, tn), lambda i, j: (i, b * per + j))
    return pl.pallas_call(
        _merge_kernel,
        grid=(T // tm, per),
        in_specs=[o_spec, o_spec, o_spec, o_spec, g_spec(0), g_spec(1), g_spec(2), g_spec(3),
                  pl.BlockSpec((None, N_BRANCH, MIX_W, tn), lambda i, j: (layer, 0, 0, j)),
                  pl.BlockSpec((None, tn, D), lambda i, j: (layer, j, 0)),
                  pl.BlockSpec((tm, D), lambda i, j: (i, 0))],
        out_specs=pl.BlockSpec((tm, D), lambda i, j: (i, 0)),
        out_shape=jax.ShapeDtypeStruct((T, D), F32),
        scratch_shapes=[pltpu.VMEM((tm, D), F32)],
        compiler_params=_cparams(("parallel", "arbitrary")),
        name="merge_project",
    )(*branches, gate_logits, gate_logits, gate_logits, gate_logits, w_branch, w_out, x)


def _cross_attn_kernel(q_ref, k_ref, v_ref, o_ref):
    dh = q_ref.shape[1] // XA_HEADS
    for h in range(XA_HEADS):
        cols = slice(h * dh, (h + 1) * dh)
        s = _dot_nt(q_ref[:, cols], k_ref[:, cols]) * (dh ** -0.5)
        m = jnp.max(s, axis=-1, keepdims=True)
        p = jnp.exp(s - m)
        p = p / jnp.sum(p, axis=-1, keepdims=True)
        o_ref[:, cols] = _dot(p, v_ref[:, cols]).astype(o_ref.dtype)


def cross_attention_core(q, kv, tq=512):
    B, S, D = q.shape
    M = kv.shape[1]
    tq = min(tq, S)
    return pl.pallas_call(
        _cross_attn_kernel,
        grid=(B, S // tq),
        in_specs=[pl.BlockSpec((None, tq, D), lambda b, i: (b, i, 0)),
                  pl.BlockSpec((None, M, D), lambda b, i: (b, 0, 0)),
                  pl.BlockSpec((None, M, D), lambda b, i: (b, 0, 1))],
        out_specs=pl.BlockSpec((None, tq, D), lambda b, i: (b, i, 0)),
        out_shape=jax.ShapeDtypeStruct((B, S, D), BF),
        compiler_params=_cparams(("parallel", "parallel")),
        name="cross_attention",
    )(q, kv, kv)


def _ffn_kernel(x_ref, nw_ref, wg_ref, wu_ref, wo_ref, o_ref, h_ref, acc_ref):
    j = pl.program_id(1)

    @pl.when(j == 0)
    def _():
        h_ref[...] = _rms(x_ref[...], nw_ref[...], 1e-6).astype(BF)
        acc_ref[...] = jnp.zeros_like(acc_ref)

    h = h_ref[...]
    gate = jnp.dot(h, wg_ref[...], preferred_element_type=F32)
    up = jnp.dot(h, wu_ref[...], preferred_element_type=F32)
    acc_ref[...] += jnp.dot((_silu(gate) * up).astype(BF), wo_ref[...], preferred_element_type=F32)

    @pl.when(j == pl.num_programs(1) - 1)
    def _():
        o_ref[...] = x_ref[...] + acc_ref[...]


def swiglu_ffn(x, norm_w, w_in, w_out, layer, tm=512, tf=512):
    T, D = x.shape
    F = w_out.shape[1]
    tm = min(tm, T)
    nf = F // tf
    return pl.pallas_call(
        _ffn_kernel,
        grid=(T // tm, nf),
        in_specs=[pl.BlockSpec((tm, D), lambda i, j: (i, 0)),
                  pl.BlockSpec((1, D), lambda i, j: (0, 0)),
                  pl.BlockSpec((None, D, tf), lambda i, j: (layer, 0, j)),
                  pl.BlockSpec((None, D, tf), lambda i, j: (layer, 0, nf + j)),
                  pl.BlockSpec((None, tf, D), lambda i, j: (layer, j, 0))],
        out_specs=pl.BlockSpec((tm, D), lambda i, j: (i, 0)),
        out_shape=jax.ShapeDtypeStruct((T, D), F32),
        scratch_shapes=[pltpu.VMEM((tm, D), BF), pltpu.VMEM((tm, D), F32)],
        compiler_params=_cparams(("parallel", "arbitrary")),
        name="swiglu_ffn",
    )(x, norm_w.reshape(1, D), w_in, w_in, w_out)


def _pack_w_in(w):
    n_gate = 3 * N_HEADS
    gate_end = BLK_NG * LANE + n_gate
    wt = jnp.swapaxes(w, 1, 2)
    pad = jnp.zeros((w.shape[0], BLK_GM * LANE - gate_end, w.shape[1]), w.dtype)
    packed = jnp.concatenate([wt[:, :gate_end], pad, wt[:, gate_end:]], axis=1)
    assert packed.shape[1] == N_PROJ, packed.shape
    return packed.astype(BF)


def kernel(x, mem, mem_norm_w, mix_norm_w, w_in, hg_lower_bounds, hg_norm_w, da_lambda_q1, da_lambda_k1, da_lambda_q2, da_lambda_k2, da_norm_w, nsa_pos_k, nsa_cmp_w1_k, nsa_cmp_w2_k, nsa_pos_v, nsa_cmp_w1_v, nsa_cmp_w2_v, gm_ln_w, gm_ln_b, gm_w_s, gm_b_s, w_branch, w_out, xa_norm_w, xa_w_q, xa_w_kv, xa_w_o, ffn_norm_w, ffn_w_in, ffn_w_out, final_norm_w):
    B, S, D = x.shape
    M = mem.shape[1]
    depth = w_in.shape[0]
    assert D == BLK_MG * LANE // 3 and S % SWA_WINDOW == 0, (S, D)
    T = B * S
    x = x.reshape(T, D)
    mem2 = mem.reshape(B * M, D)
    w_in_p = _pack_w_in(w_in)
    w_branch_b, w_out_b = w_branch.astype(BF), w_out.astype(BF)
    xa_w_q_b, xa_w_kv_b, xa_w_o_b = xa_w_q.astype(BF), xa_w_kv.astype(BF), xa_w_o.astype(BF)
    ffn_w_in_b, ffn_w_out_b = ffn_w_in.astype(BF), ffn_w_out.astype(BF)
    for l in range(depth):
        lambda_init = 0.8 - 0.6 * math.exp(-0.3 * l)
        proj, gate_logits = norm_matmul(x, mix_norm_w[l], w_in_p, l, F32, tm=1024, w_is_transposed=True,
                                        tail=(N_PROJ - N_MIXER_IN, BF))
        proj = proj.reshape(B, S, N_MIXER_IN)
        o_a = hgrn2_mixer(proj, hg_lower_bounds, hg_norm_w[l], l)
        lam_params = jnp.stack([da_lambda_q1[l], da_lambda_k1[l], da_lambda_q2[l], da_lambda_k2[l]])
        o_b = diff_attention_mixer(proj, lam_params, da_norm_w[l], lambda_init)
        nkv = lambda off: proj[:, :, (BLK_NKV + off) * LANE:(BLK_NKV + off + 1) * LANE]
        kc = compress_tokens(nkv(0), nsa_pos_k[l], nsa_cmp_w1_k[l].astype(BF), nsa_cmp_w2_k[l].astype(BF))
        vc = compress_tokens(nkv(1), nsa_pos_v[l], nsa_cmp_w1_v[l].astype(BF), nsa_cmp_w2_v[l].astype(BF))
        o_cmp, sel = nsa_compressed_and_select(proj, kc, vc)
        o_slc = nsa_selected_attention(proj, sel)
        o_c = nsa_window_and_combine(proj, o_cmp, o_slc)
        o_d = chunk_gmlp_mixer(proj, gm_ln_w[l], gm_ln_b[l], gm_w_s[l], gm_b_s[l])
        branches = [o.reshape(T, MIX_W) for o in (o_a, o_b, o_c, o_d)]
        x = merge_project(x, gate_logits, branches, w_branch_b, w_out_b, l)
        q = norm_matmul(x, xa_norm_w[l], xa_w_q_b, l, BF, tm=1024).reshape(B, S, D)
        kv = norm_matmul(mem2, mem_norm_w, xa_w_kv_b, l, F32).reshape(B, M, 2 * D)
        att = cross_attention_core(q, kv).reshape(T, D)
        x = matmul_residual(att, xa_w_o_b, l, x, tm=1024)
        x = swiglu_ffn(x, ffn_norm_w[l], ffn_w_in_b, ffn_w_out_b, l)
    return rmsnorm(x, final_norm_w).reshape(B, S, D)
```

```python
import functools
import math

import jax
import jax.numpy as jnp
from jax import lax
from jax.experimental import pallas as pl
from jax.experimental.pallas import tpu as pltpu

BF = jnp.bfloat16
F32 = jnp.float32

LANE = 128
D_MODEL = 2048
HEAD_DIM = 128
N_HEADS = 4
MIX_W = N_HEADS * HEAD_DIM
N_BRANCH = 4
HG_CHUNK = 64
HG_SUB = 16
HG_EXP_CLAMP = 80.0
CMP_LEN = 32
CMP_STRIDE = 16
SLC_BLOCK = 64
SLC_TOPK = 16
N_LOCAL_BLOCKS = 2
SWA_WINDOW = 512
XA_HEADS = 4
NEG = -1e30
LOG2E = math.log2(math.e)
FLASH_MM_ROWS = 256
VMEM_LIMIT = 56 * 1024 * 1024

BLK_HG = 0
BLK_DA = 16
BLK_NQ = 28
BLK_NKV = 32
BLK_NG = 38
BLK_GM = 40
BLK_MG = 48
N_PROJ = 112 * LANE


def _cparams(sem):
    return pltpu.CompilerParams(dimension_semantics=sem, vmem_limit_bytes=VMEM_LIMIT)


def _dot(a, b):
    return jnp.dot(a.astype(BF), b.astype(BF), preferred_element_type=F32)


def _dot_nt(a, b):
    return lax.dot_general(a.astype(BF), b.astype(BF), (((1,), (1,)), ((), ())),
                           preferred_element_type=F32)


def _dot_tn(a, b):
    return lax.dot_general(a.astype(BF), b.astype(BF), (((0,), (0,)), ((), ())),
                           preferred_element_type=F32)


def _split3(a):
    hi = a.astype(BF)
    r = a - hi.astype(F32)
    mid = r.astype(BF)
    lo = (r - mid.astype(F32)).astype(BF)
    return hi, mid, lo


def _rms(x, w, eps):
    return x * lax.rsqrt(jnp.mean(x * x, axis=-1, keepdims=True) + eps) * w


def _sigmoid(x):
    return 1.0 / (1.0 + jnp.exp(-x))


def _silu(x):
    return x * _sigmoid(x)


def _norm_matmul_kernel(w_is_transposed, x_ref, nw_ref, w_ref, o_ref, h_ref):
    @pl.when(pl.program_id(1) == 0)
    def _():
        h_ref[...] = _rms(x_ref[...], nw_ref[...], 1e-6).astype(BF)

    if w_is_transposed:
        o = lax.dot_general(h_ref[...], w_ref[...], (((1,), (1,)), ((), ())), preferred_element_type=F32)
    else:
        o = jnp.dot(h_ref[...], w_ref[...], preferred_element_type=F32)
    o_ref[...] = o.astype(o_ref.dtype)


def norm_matmul(x, norm_w, w, layer, out_dtype, tm=512, tn=1024, w_is_transposed=False):
    T, D = x.shape
    N = w.shape[1] if w_is_transposed else w.shape[2]
    tm, tn = min(tm, T), min(tn, N)
    w_spec = (pl.BlockSpec((None, tn, D), lambda i, j: (layer, j, 0)) if w_is_transposed
              else pl.BlockSpec((None, D, tn), lambda i, j: (layer, 0, j)))
    return pl.pallas_call(
        functools.partial(_norm_matmul_kernel, w_is_transposed),
        grid=(T // tm, N // tn),
        in_specs=[pl.BlockSpec((tm, D), lambda i, j: (i, 0)),
                  pl.BlockSpec((1, D), lambda i, j: (0, 0)),
                  w_spec],
        out_specs=pl.BlockSpec((tm, tn), lambda i, j: (i, j)),
        out_shape=jax.ShapeDtypeStruct((T, N), out_dtype),
        scratch_shapes=[pltpu.VMEM((tm, D), BF)],
        compiler_params=_cparams(("parallel", "arbitrary")),
        name="norm_matmul",
    )(x, norm_w.reshape(1, D), w)


def _matmul_residual_kernel(a_ref, w_ref, r_ref, o_ref):
    o_ref[...] = r_ref[...] + jnp.dot(a_ref[...], w_ref[...], preferred_element_type=F32)


def matmul_residual(a, w, layer, r, tm=512, tn=1024):
    T, K = a.shape
    N = w.shape[2]
    tm, tn = min(tm, T), min(tn, N)
    return pl.pallas_call(
        _matmul_residual_kernel,
        grid=(T // tm, N // tn),
        in_specs=[pl.BlockSpec((tm, K), lambda i, j: (i, 0)),
                  pl.BlockSpec((None, K, tn), lambda i, j: (layer, 0, j)),
                  pl.BlockSpec((tm, tn), lambda i, j: (i, j))],
        out_specs=pl.BlockSpec((tm, tn), lambda i, j: (i, j)),
        out_shape=jax.ShapeDtypeStruct((T, N), F32),
        compiler_params=_cparams(("parallel", "arbitrary")),
        name="matmul_residual",
    )(a, w, r)


def _rmsnorm_kernel(x_ref, w_ref, o_ref):
    o_ref[...] = _rms(x_ref[...], w_ref[...], 1e-6)


def rmsnorm(x, w, tm=512):
    T, D = x.shape
    tm = min(tm, T)
    return pl.pallas_call(
        _rmsnorm_kernel,
        grid=(T // tm,),
        in_specs=[pl.BlockSpec((tm, D), lambda i: (i, 0)),
                  pl.BlockSpec((1, D), lambda i: (0, 0))],
        out_specs=pl.BlockSpec((tm, D), lambda i: (i, 0)),
        out_shape=jax.ShapeDtypeStruct((T, D), F32),
        compiler_params=_cparams(("parallel",)),
        name="final_rmsnorm",
    )(x, w.reshape(1, D))


def _hgrn2_kernel(layer, q_ref, f_ref, i_ref, g_ref, lb_ref, nw_ref, o_ref, state_ref):
    @pl.when(pl.program_id(1) == 0)
    def _():
        state_ref[...] = jnp.zeros_like(state_ref)

    lb_all = lb_ref[...]
    e = jnp.exp(lb_all - jnp.max(lb_all, axis=0, keepdims=True))
    soft = e / jnp.sum(e, axis=0, keepdims=True)
    lrow = lax.broadcasted_iota(jnp.int32, soft.shape, 0)
    lower_all = jnp.sum(jnp.where((lrow >= 1) & (lrow <= layer), soft, 0.0), axis=0, keepdims=True)

    C, SUB = HG_CHUNK, HG_SUB
    tt = q_ref.shape[0]
    row = lax.broadcasted_iota(jnp.int32, (C, C), 0)
    col = lax.broadcasted_iota(jnp.int32, (C, C), 1)
    tri = jnp.where(col <= row, 1.0, 0.0).astype(BF)
    tri_sub = jnp.where(col < (row // SUB) * SUB, 1.0, 0.0).astype(BF)
    tri2 = jnp.concatenate([tri, tri_sub], axis=0)
    nsub = C // SUB
    mrow = lax.broadcasted_iota(jnp.int32, (C, nsub * C), 0)
    mcol = lax.broadcasted_iota(jnp.int32, (C, nsub * C), 1)
    keep = jnp.where(mcol // C == mrow // SUB, mcol % C, C) <= mrow
    w = MIX_W

    def chunk(c, states):
        rows = slice(c * C, (c + 1) * C)
        f_all = lower_all + (1.0 - lower_all) * _sigmoid(f_ref[rows, :])
        log_f = jnp.log(f_all)
        hi, mid, lo = _split3(log_f)
        cum = jnp.dot(tri2, jnp.concatenate([hi, mid, lo], axis=1), preferred_element_type=F32)
        cum = cum[:, 0:w] + cum[:, w:2 * w] + cum[:, 2 * w:3 * w]
        b_all, b_sub_all = cum[0:C], cum[C:2 * C]
        b_end_all = jnp.sum(log_f, axis=0, keepdims=True)
        new_states = []
        for h in range(N_HEADS):
            cols = slice(h * HEAD_DIM, (h + 1) * HEAD_DIM)
            b, b_sub, b_end = b_all[:, cols], b_sub_all[:, cols], b_end_all[:, cols]
            q = _silu(q_ref[rows, cols])
            k = 1.0 - f_all[:, cols]
            v = i_ref[rows, cols]
            o_inter = _dot_nt(q * jnp.exp(b), states[h])
            qt = q * jnp.exp(b - b_sub)
            kt = jnp.concatenate(
                [k * jnp.exp(jnp.minimum(b_sub[s * SUB:s * SUB + 1] - b, HG_EXP_CLAMP)) for s in range(nsub)],
                axis=0)
            sc = jnp.where(keep, _dot_nt(qt, kt), 0.0)
            o = o_inter + _dot(sc, jnp.concatenate([v] * nsub, axis=0))
            o = _rms(o, nw_ref[...], 1e-6) * _silu(g_ref[rows, cols])
            o_ref[rows, cols] = o.astype(o_ref.dtype)
            new_states.append(states[h] * jnp.exp(b_end) + _dot_tn(v, k * jnp.exp(b_end - b)))
        return new_states

    states = [state_ref[h] for h in range(N_HEADS)]
    for c in range(tt // C):
        states = chunk(c, states)
    for h in range(N_HEADS):
        state_ref[h] = states[h]


def hgrn2_mixer(proj, lower_bounds, norm_w, layer, tt=256):
    B, S, _ = proj.shape
    L = lower_bounds.shape[0]
    tt = min(tt, S)
    seg = lambda off: pl.BlockSpec((None, tt, MIX_W), lambda b, t: (b, t, (BLK_HG + off) // 4))
    return pl.pallas_call(
        functools.partial(_hgrn2_kernel, layer),
        grid=(B, S // tt),
        in_specs=[seg(0), seg(4), seg(8), seg(12),
                  pl.BlockSpec((L, MIX_W), lambda b, t: (0, 0)),
                  pl.BlockSpec((1, LANE), lambda b, t: (0, 0))],
        out_specs=pl.BlockSpec((None, tt, MIX_W), lambda b, t: (b, t, 0)),
        out_shape=jax.ShapeDtypeStruct((B, S, MIX_W), BF),
        scratch_shapes=[pltpu.VMEM((N_HEADS, HEAD_DIM, HEAD_DIM), F32)],
        compiler_params=_cparams(("parallel", "arbitrary")),
        name="hgrn2",
    )(proj, proj, proj, proj, lower_bounds, norm_w.reshape(1, LANE))


def _flash_init(m_ref, acc_ref, vx_ref):
    m_ref[...] = jnp.full_like(m_ref, NEG)
    acc_ref[...] = jnp.zeros_like(acc_ref)
    vx_ref[:, LANE:2 * LANE] = jnp.ones((vx_ref.shape[0], LANE), BF)


def _flash_step(qs_ref, k_ref, v_ref, vx_ref, s_ref, p_ref, m_ref, acc_ref, rb, bias_fn, live_cols=None):
    nrows, tk = s_ref.shape
    k = k_ref[...].astype(BF)
    vx_ref[:, 0:LANE] = v_ref[...].astype(BF)
    chunk_cols = lambda c: tk if live_cols is None else live_cols((c + 1) * FLASH_MM_ROWS - rb)
    for c in range(nrows // FLASH_MM_ROWS):
        rows = slice(c * FLASH_MM_ROWS, (c + 1) * FLASH_MM_ROWS)
        s_ref[rows, 0:chunk_cols(c)] = _dot_nt(qs_ref[rows, :], k[0:chunk_cols(c)])
    for r in range(nrows // rb):
        rows = slice(r * rb, (r + 1) * rb)
        n = tk if live_cols is None else live_cols(r * rb)
        s = s_ref[rows, 0:n]
        if bias_fn is not None:
            s = s + bias_fn(r * rb, n)
        m_prev = m_ref[rows, :]
        m_new = jnp.maximum(m_prev, jnp.max(s, axis=-1, keepdims=True))
        p_ref[rows, 0:n] = jnp.exp2(s - jnp.concatenate([m_new] * (n // LANE), axis=1)).astype(BF)
        n_mm = chunk_cols(r * rb // FLASH_MM_ROWS)
        if n < n_mm:
            p_ref[rows, n:n_mm] = jnp.zeros((rb, n_mm - n), BF)
        alpha = jnp.exp2(m_prev - m_new)
        acc_ref[rows, :] = acc_ref[rows, :] * jnp.concatenate([alpha, alpha], axis=1)
        m_ref[rows, :] = m_new
    vx = vx_ref[...]
    for c in range(nrows // FLASH_MM_ROWS):
        rows = slice(c * FLASH_MM_ROWS, (c + 1) * FLASH_MM_ROWS)
        acc_ref[rows, :] += jnp.dot(p_ref[rows, 0:chunk_cols(c)], vx[0:chunk_cols(c)], preferred_element_type=F32)


def _diff_attn_kernel(lambda_init, tq, tk, rb, q_ref, k_ref, v_ref, lam_ref, nw_ref, o_ref,
                      qs_ref, m_ref, acc_ref, vx_ref, s_ref, p_ref):
    i, j = pl.program_id(2), pl.program_id(3)
    dc = HEAD_DIM // 2

    @pl.when(j == 0)
    def _():
        q = q_ref[...] * (dc ** -0.5 * LOG2E)
        first = lax.broadcasted_iota(jnp.int32, q.shape, 1) < dc
        qs_ref[0:tq, :] = jnp.where(first, q, 0.0).astype(BF)
        qs_ref[tq:2 * tq, :] = jnp.where(first, 0.0, q).astype(BF)
        _flash_init(m_ref, acc_ref, vx_ref)

    def causal_bias(r0, n):
        qloc = r0 % tq + lax.broadcasted_iota(jnp.int32, (rb, n), 0)
        col = lax.broadcasted_iota(jnp.int32, (rb, n), 1)
        return jnp.where(col - qloc <= i * tq - j * tk, 0.0, NEG)

    live_cols = (lambda r0: min(tk, -(-(r0 % tq + rb) // LANE) * LANE)) if tq == tk else None

    needed = j * tk <= i * tq + tq - 1
    unmasked = (j + 1) * tk - 1 <= i * tq

    @pl.when(needed & unmasked)
    def _():
        _flash_step(qs_ref, k_ref, v_ref, vx_ref, s_ref, p_ref, m_ref, acc_ref, rb, None)

    @pl.when(needed & jnp.logical_not(unmasked))
    def _():
        _flash_step(qs_ref, k_ref, v_ref, vx_ref, s_ref, p_ref, m_ref, acc_ref, rb, causal_bias, live_cols)

    @pl.when(j == pl.num_programs(3) - 1)
    def _():
        lam = (jnp.exp(jnp.sum(lam_ref[0:1] * lam_ref[1:2], axis=-1, keepdims=True))
               - jnp.exp(jnp.sum(lam_ref[2:3] * lam_ref[3:4], axis=-1, keepdims=True)) + lambda_init)
        o = acc_ref[:, 0:LANE] / acc_ref[:, LANE:2 * LANE]
        o = o[0:tq] - lam * o[tq:2 * tq]
        o = _rms(o, nw_ref[...], 1e-6) * (1.0 - lambda_init)
        o_ref[...] = o.astype(o_ref.dtype)


def diff_attention_mixer(proj, lam_params, norm_w, lambda_init, tq=1024, tk=1024, rb=16):
    B, S, _ = proj.shape
    tq, tk = min(tq, S), min(tk, S)
    last = lambda i: (i * tq + tq - 1) // tk
    return pl.pallas_call(
        functools.partial(_diff_attn_kernel, lambda_init, tq, tk, rb),
        grid=(B, N_HEADS, S // tq, S // tk),
        in_specs=[pl.BlockSpec((None, tq, LANE), lambda b, h, i, j: (b, i, BLK_DA + h)),
                  pl.BlockSpec((None, tk, LANE),
                               lambda b, h, i, j: (b, jnp.minimum(j, last(i)), BLK_DA + 4 + h)),
                  pl.BlockSpec((None, tk, LANE),
                               lambda b, h, i, j: (b, jnp.minimum(j, last(i)), BLK_DA + 8 + h)),
                  pl.BlockSpec((4, HEAD_DIM // 2), lambda b, h, i, j: (0, 0)),
                  pl.BlockSpec((1, LANE), lambda b, h, i, j: (0, 0))],
        out_specs=pl.BlockSpec((None, tq, LANE), lambda b, h, i, j: (b, i, h)),
        out_shape=jax.ShapeDtypeStruct((B, S, MIX_W), BF),
        scratch_shapes=[pltpu.VMEM((2 * tq, LANE), BF),
                        pltpu.VMEM((2 * tq, LANE), F32),
                        pltpu.VMEM((2 * tq, 2 * LANE), F32),
                        pltpu.VMEM((tk, 2 * LANE), BF),
                        pltpu.VMEM((2 * tq, tk), F32),
                        pltpu.VMEM((2 * tq, tk), BF)],
        compiler_params=_cparams(("parallel", "parallel", "parallel", "arbitrary")),
        name="diff_attention",
    )(proj, proj, proj, lam_params, norm_w.reshape(1, LANE))


def _compress_kernel(t_ref, pos_ref, w1_ref, w2_ref, o_ref):
    half = t_ref.shape[1]
    t = t_ref[...]
    a = _dot(t, w1_ref[0:half, :])
    bnext = _dot(t, w1_ref[half:2 * half, :])
    c0 = _dot(pos_ref[...], w1_ref[...])
    n = a.shape[0]
    hidden = a + pltpu.roll(bnext, n - 1, 0) + c0
    o_ref[...] = _dot(_silu(hidden), w2_ref[...])


def compress_tokens(t, pos, w1, w2):
    B, S, d = t.shape
    n_chunks = S // CMP_STRIDE
    hid = w1.shape[1]
    t = t.reshape(B, n_chunks, CMP_STRIDE * d)
    return pl.pallas_call(
        _compress_kernel,
        grid=(B,),
        in_specs=[pl.BlockSpec((None, n_chunks, CMP_STRIDE * d), lambda b: (b, 0, 0)),
                  pl.BlockSpec((1, CMP_LEN * d), lambda b: (0, 0)),
                  pl.BlockSpec((CMP_LEN * d, hid), lambda b: (0, 0)),
                  pl.BlockSpec((hid, d), lambda b: (0, 0))],
        out_specs=pl.BlockSpec((None, n_chunks, d), lambda b: (b, 0, 0)),
        out_shape=jax.ShapeDtypeStruct((B, n_chunks, d), F32),
        compiler_params=_cparams(("parallel",)),
        name="nsa_compress",
    )(t, pos.reshape(1, CMP_LEN * d), w1, w2)


def _cmp_select_kernel(tq, rb, q_ref, kc_ref, vc_ref, pool_ref, o_ref, sel_ref, s_ref, p_ref, imp_ref, bias_ref):
    i = pl.program_id(1)
    ncp = kc_ref.shape[0]
    n_slc = sel_ref.shape[1]
    scale = HEAD_DIM ** -0.5
    qpos = i * tq + lax.broadcasted_iota(jnp.int32, (tq, 1), 0)
    cmp_end = lax.broadcasted_iota(jnp.int32, (1, ncp), 1) * CMP_STRIDE + (CMP_LEN - 1)
    bias_ref[...] = jnp.where(cmp_end <= qpos, 0.0, NEG)
    kc = kc_ref[...].astype(BF)
    vc = vc_ref[...].astype(BF)
    for h in range(N_HEADS):
        cols = slice(h * HEAD_DIM, (h + 1) * HEAD_DIM)
        slot = h % 2
        s_ref[slot] = _dot_nt(q_ref[:, cols] * scale, kc)
        for r in range(tq // rb):
            rows = slice(r * rb, (r + 1) * rb)
            s = s_ref[slot, rows, :] + bias_ref[rows, :]
            e = jnp.exp(s - jnp.max(s, axis=-1, keepdims=True))
            seen = i * tq + r * rb + lax.broadcasted_iota(jnp.int32, (rb, 1), 0) >= CMP_LEN - 1
            p = e * jnp.where(seen, 1.0 / jnp.sum(e, axis=-1, keepdims=True), 0.0)
            p_ref[slot, rows, :] = p.astype(BF)
            imp_ref[rows, :] = p if h == 0 else imp_ref[rows, :] + p
        o_ref[:, cols] = jnp.dot(p_ref[slot], vc, preferred_element_type=F32)
    imp = imp_ref[...]
    pool_t = pool_ref[...]
    imp_hi, imp_mid, _ = _split3(imp)
    imp_slc = _dot_nt(pool_t, imp_hi) + _dot_nt(pool_t, imp_mid)
    blk = lax.broadcasted_iota(jnp.int32, (n_slc, 1), 0)
    cur = (i * tq + lax.broadcasted_iota(jnp.int32, (1, tq), 1)) // SLC_BLOCK
    valid = blk <= cur
    forced = (blk == 0) | (blk > cur - N_LOCAL_BLOCKS)
    n_forced = 1 + N_LOCAL_BLOCKS
    score = jnp.where(valid, jnp.where(forced, -2.0, imp_slc), -1.0)
    blk_f = blk.astype(F32)
    for _ in range(min(SLC_TOPK, n_slc) - n_forced):
        m = jnp.max(score, axis=0, keepdims=True)
        first = jnp.min(jnp.where(score == m, blk_f, float(n_slc)), axis=0, keepdims=True)
        score = jnp.where(blk_f == first, -2.0, score)
    sel_ref[...] = jnp.where(score == -2.0, 1.0, 0.0).T.astype(sel_ref.dtype)


def nsa_compressed_and_select(proj, kc, vc, tq=256, rb=16):
    B, S, _ = proj.shape
    ncp = kc.shape[1]
    n_slc = S // SLC_BLOCK
    n = jnp.arange(ncp)[:, None]
    jj = jnp.arange(n_slc)[None, :]
    w = n - (SLC_BLOCK // CMP_STRIDE) * jj + 1
    pool = jnp.where((w >= 0) & (w <= 4), jnp.where((w == 0) | (w == 4), 1.0, 2.0), 0.0).astype(BF).T
    return pl.pallas_call(
        functools.partial(_cmp_select_kernel, tq, rb),
        grid=(B, S // tq),
        in_specs=[pl.BlockSpec((None, tq, MIX_W), lambda b, i: (b, i, BLK_NQ // 4)),
                  pl.BlockSpec((None, ncp, HEAD_DIM), lambda b, i: (b, 0, 0)),
                  pl.BlockSpec((None, ncp, HEAD_DIM), lambda b, i: (b, 0, 0)),
                  pl.BlockSpec((n_slc, ncp), lambda b, i: (0, 0))],
        out_specs=[pl.BlockSpec((None, tq, MIX_W), lambda b, i: (b, i, 0)),
                   pl.BlockSpec((None, tq, n_slc), lambda b, i: (b, i, 0))],
        out_shape=[jax.ShapeDtypeStruct((B, S, MIX_W), F32),
                   jax.ShapeDtypeStruct((B, S, n_slc), BF)],
        scratch_shapes=[pltpu.VMEM((2, tq, ncp), F32), pltpu.VMEM((2, tq, ncp), BF),
                        pltpu.VMEM((tq, ncp), F32), pltpu.VMEM((tq, ncp), F32)],
        compiler_params=_cparams(("parallel", "parallel")),
        name="nsa_cmp_select",
    )(proj, kc, vc, pool)


def _slc_attn_kernel(tq, tk, rb, q_ref, k_ref, v_ref, sel_ref, exp_ref, o_ref,
                     qs_ref, m_ref, acc_ref, vx_ref, s_ref, p_ref, bias_ref):
    i, j = pl.program_id(1), pl.program_id(2)

    @pl.when(j == 0)
    def _():
        for h in range(N_HEADS):
            q = q_ref[:, h * HEAD_DIM:(h + 1) * HEAD_DIM] * (HEAD_DIM ** -0.5 * LOG2E)
            qs_ref[h * tq:(h + 1) * tq, :] = q.astype(BF)
        _flash_init(m_ref, acc_ref, vx_ref)

    @pl.when(j * tk <= i * tq + tq - 1)
    def _():
        picked = jnp.dot(sel_ref[...], exp_ref[...], preferred_element_type=F32)
        qpos = i * tq + lax.broadcasted_iota(jnp.int32, (tq, tk), 0)
        kpos = j * tk + lax.broadcasted_iota(jnp.int32, (tq, tk), 1)
        bias_ref[...] = jnp.where(kpos <= qpos, (picked - 1.0) * (-NEG), NEG)
        _flash_step(qs_ref, k_ref, v_ref, vx_ref, s_ref, p_ref, m_ref, acc_ref, rb,
                    lambda r0, n: bias_ref[(r0 % tq):(r0 % tq) + rb, 0:n])

    @pl.when(j == pl.num_programs(2) - 1)
    def _():
        for h in range(N_HEADS):
            rows = slice(h * tq, (h + 1) * tq)
            o_ref[:, h * HEAD_DIM:(h + 1) * HEAD_DIM] = acc_ref[rows, 0:LANE] / acc_ref[rows, LANE:2 * LANE]


def nsa_selected_attention(proj, sel, tq=512, tk=1024, rb=16):
    B, S, _ = proj.shape
    tq, tk = min(tq, S), min(tk, S)
    n_slc = S // SLC_BLOCK
    expand = (jnp.arange(n_slc)[:, None] == (jnp.arange(S)[None, :] // SLC_BLOCK)).astype(BF)
    last = lambda i: (i * tq + tq - 1) // tk
    kv = lambda off: pl.BlockSpec((None, tk, LANE),
                                  lambda b, i, j: (b, jnp.minimum(j, last(i)), BLK_NKV + off))
    rows = N_HEADS * tq
    return pl.pallas_call(
        functools.partial(_slc_attn_kernel, tq, tk, rb),
        grid=(B, S // tq, S // tk),
        in_specs=[pl.BlockSpec((None, tq, MIX_W), lambda b, i, j: (b, i, BLK_NQ // 4)),
                  kv(2), kv(3),
                  pl.BlockSpec((None, tq, n_slc), lambda b, i, j: (b, i, 0)),
                  pl.BlockSpec((n_slc, tk), lambda b, i, j: (0, jnp.minimum(j, last(i))))],
        out_specs=pl.BlockSpec((None, tq, MIX_W), lambda b, i, j: (b, i, 0)),
        out_shape=jax.ShapeDtypeStruct((B, S, MIX_W), F32),
        scratch_shapes=[pltpu.VMEM((rows, LANE), BF),
                        pltpu.VMEM((rows, LANE), F32),
                        pltpu.VMEM((rows, 2 * LANE), F32),
                        pltpu.VMEM((tk, 2 * LANE), BF),
                        pltpu.VMEM((rows, tk), F32),
                        pltpu.VMEM((rows, tk), BF),
                        pltpu.VMEM((tq, tk), F32)],
        compiler_params=_cparams(("parallel", "parallel", "arbitrary")),
        name="nsa_selected",
    )(proj, proj, proj, sel, expand)


def _swa_combine_kernel(tq, rb, q_ref, kp_ref, kc_ref, vp_ref, vc_ref, g_ref, ocmp_ref, oslc_ref, bias_ref, o_ref,
                        s_ref, p_ref):
    scale = HEAD_DIM ** -0.5
    k = jnp.concatenate([kp_ref[...], kc_ref[...]], axis=0).astype(BF)
    v = jnp.concatenate([vp_ref[...], vc_ref[...]], axis=0).astype(BF)
    gates = _sigmoid(g_ref[...])
    for h in range(N_HEADS):
        cols = slice(h * HEAD_DIM, (h + 1) * HEAD_DIM)
        slot = h % 2
        s_ref[slot] = _dot_nt(q_ref[:, cols] * scale, k)
        for r in range(tq // rb):
            rows = slice(r * rb, (r + 1) * rb)
            s = s_ref[slot, rows, :] + bias_ref[rows, :]
            e = jnp.exp(s - jnp.max(s, axis=-1, keepdims=True))
            p_ref[slot, rows, :] = (e * (1.0 / jnp.sum(e, axis=-1, keepdims=True))).astype(BF)
        o_swa = jnp.dot(p_ref[slot], v, preferred_element_type=F32)
        o = (gates[:, 3 * h:3 * h + 1] * ocmp_ref[:, cols]
             + gates[:, 3 * h + 1:3 * h + 2] * oslc_ref[:, cols]
             + gates[:, 3 * h + 2:3 * h + 3] * o_swa)
        o_ref[:, cols] = o.astype(o_ref.dtype)


def nsa_window_and_combine(proj, o_cmp, o_slc):
    B, S, _ = proj.shape
    tq = SWA_WINDOW
    kvp = lambda off: pl.BlockSpec((None, tq, LANE), lambda b, i: (b, jnp.maximum(i - 1, 0), BLK_NKV + off))
    kvc = lambda off: pl.BlockSpec((None, tq, LANE), lambda b, i: (b, i, BLK_NKV + off))
    full = lambda: pl.BlockSpec((None, tq, MIX_W), lambda b, i: (b, i, 0))
    dist = jnp.arange(tq)[:, None] - (jnp.arange(SWA_WINDOW + tq)[None, :] - SWA_WINDOW)
    inside = (dist >= 0) & (dist < SWA_WINDOW)
    first = inside & (jnp.arange(SWA_WINDOW + tq)[None, :] >= SWA_WINDOW)
    bias = jnp.where(jnp.stack([first, inside]), 0.0, NEG).astype(F32)
    return pl.pallas_call(
        functools.partial(_swa_combine_kernel, tq, 16),
        grid=(B, S // tq),
        in_specs=[pl.BlockSpec((None, tq, MIX_W), lambda b, i: (b, i, BLK_NQ // 4)),
                  kvp(4), kvc(4), kvp(5), kvc(5),
                  pl.BlockSpec((None, tq, LANE), lambda b, i: (b, i, BLK_NG)),
                  full(), full(),
                  pl.BlockSpec((None, tq, SWA_WINDOW + tq), lambda b, i: (jnp.minimum(i, 1), 0, 0))],
        out_specs=full(),
        out_shape=jax.ShapeDtypeStruct((B, S, MIX_W), BF),
        scratch_shapes=[pltpu.VMEM((2, tq, SWA_WINDOW + tq), F32), pltpu.VMEM((2, tq, SWA_WINDOW + tq), BF)],
        compiler_params=_cparams(("parallel", "parallel")),
        name="nsa_window_combine",
    )(proj, proj, proj, proj, proj, proj, o_cmp, o_slc, bias)


def _gelu_tanh(x):
    return 0.5 * x * (1.0 + jnp.tanh(math.sqrt(2.0 / math.pi) * (x + 0.044715 * (x * x * x))))


def _gmlp_kernel(u_ref, v_ref, lnw_ref, lnb_ref, ws_ref, bs_ref, o_ref):
    C = ws_ref.shape[1]
    row = lax.broadcasted_iota(jnp.int32, (C, C), 0)
    col = lax.broadcasted_iota(jnp.int32, (C, C), 1)
    causal = col <= row
    for c in range(u_ref.shape[0] // C):
        rows = slice(c * C, (c + 1) * C)
        u = _gelu_tanh(u_ref[rows, :])
        v = _gelu_tanh(v_ref[rows, :])
        mu = jnp.mean(v, axis=-1, keepdims=True)
        var = jnp.mean(jnp.square(v - mu), axis=-1, keepdims=True)
        v = (v - mu) * lax.rsqrt(var + 1e-5) * lnw_ref[...] + lnb_ref[...]
        for g in range(N_HEADS):
            cols = slice(g * HEAD_DIM, (g + 1) * HEAD_DIM)
            w = jnp.where(causal, ws_ref[g], 0.0)
            vm = _dot(w, v[:, cols]) + bs_ref[:, cols]
            o_ref[rows, cols] = (u[:, cols] * vm).astype(o_ref.dtype)


def chunk_gmlp_mixer(proj, ln_w, ln_b, w_s, b_s, tt=512):
    B, S, _ = proj.shape
    G, C, _ = w_s.shape
    tt = min(tt, S)
    bias = jnp.repeat(b_s.T, HEAD_DIM, axis=1)
    return pl.pallas_call(
        _gmlp_kernel,
        grid=(B, S // tt),
        in_specs=[pl.BlockSpec((None, tt, MIX_W), lambda b, t: (b, t, BLK_GM // 4)),
                  pl.BlockSpec((None, tt, MIX_W), lambda b, t: (b, t, BLK_GM // 4 + 1)),
                  pl.BlockSpec((1, MIX_W), lambda b, t: (0, 0)),
                  pl.BlockSpec((1, MIX_W), lambda b, t: (0, 0)),
                  pl.BlockSpec((G, C, C), lambda b, t: (0, 0, 0)),
                  pl.BlockSpec((C, MIX_W), lambda b, t: (0, 0))],
        out_specs=pl.BlockSpec((None, tt, MIX_W), lambda b, t: (b, t, 0)),
        out_shape=jax.ShapeDtypeStruct((B, S, MIX_W), BF),
        compiler_params=_cparams(("parallel", "parallel")),
        name="chunk_gmlp",
    )(proj, proj, ln_w.reshape(1, MIX_W), ln_b.reshape(1, MIX_W), w_s, bias)


def _merge_kernel(oa_ref, ob_ref, oc_ref, od_ref, ga_ref, gb_ref, gc_ref, gd_ref, wb_ref, wo_ref, x_ref,
                  o_ref, acc_ref):
    j = pl.program_id(1)

    @pl.when(j == 0)
    def _():
        acc_ref[...] = jnp.zeros_like(acc_ref)

    merged = None
    for b, (o_b, g_b) in enumerate(((oa_ref, ga_ref), (ob_ref, gb_ref), (oc_ref, gc_ref), (od_ref, gd_ref))):
        term = _sigmoid(g_b[...]) * jnp.dot(o_b[...], wb_ref[b], preferred_element_type=F32)
        merged = term if merged is None else merged + term
    acc_ref[...] += jnp.dot(merged.astype(BF), wo_ref[...], preferred_element_type=F32)

    @pl.when(j == pl.num_programs(1) - 1)
    def _():
        o_ref[...] = x_ref[...] + acc_ref[...]


def merge_project(x, proj, branches, w_branch, w_out, layer, tm=512, tn=512):
    T, D = x.shape
    tm = min(tm, T)
    per = D // tn
    o_spec = pl.BlockSpec((tm, MIX_W), lambda i, j: (i, 0))
    g_spec = lambda b: pl.BlockSpec((tm, tn), lambda i, j: (i, (BLK_MG * LANE) // tn + b * per + j))
    return pl.pallas_call(
        _merge_kernel,
        grid=(T // tm, per),
        in_specs=[o_spec, o_spec, o_spec, o_spec, g_spec(0), g_spec(1), g_spec(2), g_spec(3),
                  pl.BlockSpec((None, N_BRANCH, MIX_W, tn), lambda i, j: (layer, 0, 0, j)),
                  pl.BlockSpec((None, tn, D), lambda i, j: (layer, j, 0)),
                  pl.BlockSpec((tm, D), lambda i, j: (i, 0))],
        out_specs=pl.BlockSpec((tm, D), lambda i, j: (i, 0)),
        out_shape=jax.ShapeDtypeStruct((T, D), F32),
        scratch_shapes=[pltpu.VMEM((tm, D), F32)],
        compiler_params=_cparams(("parallel", "arbitrary")),
        name="merge_project",
    )(*branches, proj, proj, proj, proj, w_branch, w_out, x)


def _cross_attn_kernel(q_ref, k_ref, v_ref, o_ref):
    dh = q_ref.shape[1] // XA_HEADS
    for h in range(XA_HEADS):
        cols = slice(h * dh, (h + 1) * dh)
        s = _dot_nt(q_ref[:, cols], k_ref[:, cols]) * (dh ** -0.5)
        m = jnp.max(s, axis=-1, keepdims=True)
        p = jnp.exp(s - m)
        p = p / jnp.sum(p, axis=-1, keepdims=True)
        o_ref[:, cols] = _dot(p, v_ref[:, cols]).astype(o_ref.dtype)


def cross_attention_core(q, kv, tq=512):
    B, S, D = q.shape
    M = kv.shape[1]
    tq = min(tq, S)
    return pl.pallas_call(
        _cross_attn_kernel,
        grid=(B, S // tq),
        in_specs=[pl.BlockSpec((None, tq, D), lambda b, i: (b, i, 0)),
                  pl.BlockSpec((None, M, D), lambda b, i: (b, 0, 0)),
                  pl.BlockSpec((None, M, D), lambda b, i: (b, 0, 1))],
        out_specs=pl.BlockSpec((None, tq, D), lambda b, i: (b, i, 0)),
        out_shape=jax.ShapeDtypeStruct((B, S, D), BF),
        compiler_params=_cparams(("parallel", "parallel")),
        name="cross_attention",
    )(q, kv, kv)


def _ffn_kernel(x_ref, nw_ref, wg_ref, wu_ref, wo_ref, o_ref, h_ref, acc_ref):
    j = pl.program_id(1)

    @pl.when(j == 0)
    def _():
        h_ref[...] = _rms(x_ref[...], nw_ref[...], 1e-6).astype(BF)
        acc_ref[...] = jnp.zeros_like(acc_ref)

    h = h_ref[...]
    gate = jnp.dot(h, wg_ref[...], preferred_element_type=F32)
    up = jnp.dot(h, wu_ref[...], preferred_element_type=F32)
    acc_ref[...] += jnp.dot((_silu(gate) * up).astype(BF), wo_ref[...], preferred_element_type=F32)

    @pl.when(j == pl.num_programs(1) - 1)
    def _():
        o_ref[...] = x_ref[...] + acc_ref[...]


def swiglu_ffn(x, norm_w, w_in, w_out, layer, tm=512, tf=512):
    T, D = x.shape
    F = w_out.shape[1]
    tm = min(tm, T)
    nf = F // tf
    return pl.pallas_call(
        _ffn_kernel,
        grid=(T // tm, nf),
        in_specs=[pl.BlockSpec((tm, D), lambda i, j: (i, 0)),
                  pl.BlockSpec((1, D), lambda i, j: (0, 0)),
                  pl.BlockSpec((None, D, tf), lambda i, j: (layer, 0, j)),
                  pl.BlockSpec((None, D, tf), lambda i, j: (layer, 0, nf + j)),
                  pl.BlockSpec((None, tf, D), lambda i, j: (layer, j, 0))],
        out_specs=pl.BlockSpec((tm, D), lambda i, j: (i, 0)),
        out_shape=jax.ShapeDtypeStruct((T, D), F32),
        scratch_shapes=[pltpu.VMEM((tm, D), BF), pltpu.VMEM((tm, D), F32)],
        compiler_params=_cparams(("parallel", "arbitrary")),
        name="swiglu_ffn",
    )(x, norm_w.reshape(1, D), w_in, w_in, w_out)


def _pack_w_in(w):
    n_gate = 3 * N_HEADS
    gate_end = BLK_NG * LANE + n_gate
    wt = jnp.swapaxes(w, 1, 2)
    pad = jnp.zeros((w.shape[0], BLK_GM * LANE - gate_end, w.shape[1]), w.dtype)
    packed = jnp.concatenate([wt[:, :gate_end], pad, wt[:, gate_end:]], axis=1)
    assert packed.shape[1] == N_PROJ, packed.shape
    return packed.astype(BF)


def kernel(x, mem, mem_norm_w, mix_norm_w, w_in, hg_lower_bounds, hg_norm_w, da_lambda_q1, da_lambda_k1, da_lambda_q2, da_lambda_k2, da_norm_w, nsa_pos_k, nsa_cmp_w1_k, nsa_cmp_w2_k, nsa_pos_v, nsa_cmp_w1_v, nsa_cmp_w2_v, gm_ln_w, gm_ln_b, gm_w_s, gm_b_s, w_branch, w_out, xa_norm_w, xa_w_q, xa_w_kv, xa_w_o, ffn_norm_w, ffn_w_in, ffn_w_out, final_norm_w):
    B, S, D = x.shape
    M = mem.shape[1]
    depth = w_in.shape[0]
    assert D == D_MODEL and S % (2 * SWA_WINDOW) == 0, (S, D)
    T = B * S
    x = x.reshape(T, D)
    mem2 = mem.reshape(B * M, D)
    w_in_p = _pack_w_in(w_in)
    w_branch_b, w_out_b = w_branch.astype(BF), w_out.astype(BF)
    xa_w_q_b, xa_w_kv_b, xa_w_o_b = xa_w_q.astype(BF), xa_w_kv.astype(BF), xa_w_o.astype(BF)
    ffn_w_in_b, ffn_w_out_b = ffn_w_in.astype(BF), ffn_w_out.astype(BF)
    for l in range(depth):
        lambda_init = 0.8 - 0.6 * math.exp(-0.3 * l)
        proj = norm_matmul(x, mix_norm_w[l], w_in_p, l, F32, tm=1024, w_is_transposed=True).reshape(B, S, N_PROJ)
        o_a = hgrn2_mixer(proj, hg_lower_bounds, hg_norm_w[l], l)
        lam_params = jnp.stack([da_lambda_q1[l], da_lambda_k1[l], da_lambda_q2[l], da_lambda_k2[l]])
        o_b = diff_attention_mixer(proj, lam_params, da_norm_w[l], lambda_init)
        nkv = lambda off: proj[:, :, (BLK_NKV + off) * LANE:(BLK_NKV + off + 1) * LANE]
        kc = compress_tokens(nkv(0), nsa_pos_k[l], nsa_cmp_w1_k[l].astype(BF), nsa_cmp_w2_k[l].astype(BF))
        vc = compress_tokens(nkv(1), nsa_pos_v[l], nsa_cmp_w1_v[l].astype(BF), nsa_cmp_w2_v[l].astype(BF))
        o_cmp, sel = nsa_compressed_and_select(proj, kc, vc)
        o_slc = nsa_selected_attention(proj, sel)
        o_c = nsa_window_and_combine(proj, o_cmp, o_slc)
        o_d = chunk_gmlp_mixer(proj, gm_ln_w[l], gm_ln_b[l], gm_w_s[l], gm_b_s[l])
        branches = [o.reshape(T, MIX_W) for o in (o_a, o_b, o_c, o_d)]
        x = merge_project(x, proj.reshape(T, N_PROJ), branches, w_branch_b, w_out_b, l)
        q = norm_matmul(x, xa_norm_w[l], xa_w_q_b, l, BF, tm=1024).reshape(B, S, D)
        kv = norm_matmul(mem2, mem_norm_w, xa_w_kv_b, l, F32).reshape(B, M, 2 * D)
        att = cross_attention_core(q, kv).reshape(T, D)
        x = matmul_residual(att, xa_w_o_b, l, x, tm=1024)
        x = swiglu_ffn(x, ffn_norm_w[l], ffn_w_in_b, ffn_w_out_b, l)
    return rmsnorm(x, final_norm_w).reshape(B, S, D)
```

```python
import functools
import math

import jax
import jax.numpy as jnp
from jax import lax
from jax.experimental import pallas as pl
from jax.experimental.pallas import tpu as pltpu

BF = jnp.bfloat16
F32 = jnp.float32

LANE = 128
D_MODEL = 2048
HEAD_DIM = 128
N_HEADS = 4
MIX_W = N_HEADS * HEAD_DIM
N_BRANCH = 4
HG_CHUNK = 64
HG_SUB = 16
HG_EXP_CLAMP = 80.0
CMP_LEN = 32
CMP_STRIDE = 16
SLC_BLOCK = 64
SLC_TOPK = 16
N_LOCAL_BLOCKS = 2
SWA_WINDOW = 512
XA_HEADS = 4
NEG = -1e30
LOG2E = math.log2(math.e)
FLASH_MM_ROWS = 256
VMEM_LIMIT = 56 * 1024 * 1024

BLK_HG = 0
BLK_DA = 16
BLK_NQ = 28
BLK_NKV = 32
BLK_NG = 38
BLK_GM = 40
BLK_MG = 48
N_PROJ = 112 * LANE


def _cparams(sem):
    return pltpu.CompilerParams(dimension_semantics=sem, vmem_limit_bytes=VMEM_LIMIT)


def _dot(a, b):
    return jnp.dot(a.astype(BF), b.astype(BF), preferred_element_type=F32)


def _dot_nt(a, b):
    return lax.dot_general(a.astype(BF), b.astype(BF), (((1,), (1,)), ((), ())),
                           preferred_element_type=F32)


def _dot_tn(a, b):
    return lax.dot_general(a.astype(BF), b.astype(BF), (((0,), (0,)), ((), ())),
                           preferred_element_type=F32)


def _split3(a):
    hi = a.astype(BF)
    r = a - hi.astype(F32)
    mid = r.astype(BF)
    lo = (r - mid.astype(F32)).astype(BF)
    return hi, mid, lo


def _rms(x, w, eps):
    return x * lax.rsqrt(jnp.mean(x * x, axis=-1, keepdims=True) + eps) * w


def _sigmoid(x):
    return 1.0 / (1.0 + jnp.exp(-x))


def _silu(x):
    return x * _sigmoid(x)


def _norm_matmul_kernel(w_is_transposed, x_ref, nw_ref, w_ref, o_ref, h_ref):
    @pl.when(pl.program_id(1) == 0)
    def _():
        h_ref[...] = _rms(x_ref[...], nw_ref[...], 1e-6).astype(BF)

    if w_is_transposed:
        o = lax.dot_general(h_ref[...], w_ref[...], (((1,), (1,)), ((), ())), preferred_element_type=F32)
    else:
        o = jnp.dot(h_ref[...], w_ref[...], preferred_element_type=F32)
    o_ref[...] = o.astype(o_ref.dtype)


def norm_matmul(x, norm_w, w, layer, out_dtype, tm=512, tn=1024, w_is_transposed=False):
    T, D = x.shape
    N = w.shape[1] if w_is_transposed else w.shape[2]
    tm, tn = min(tm, T), min(tn, N)
    w_spec = (pl.BlockSpec((None, tn, D), lambda i, j: (layer, j, 0)) if w_is_transposed
              else pl.BlockSpec((None, D, tn), lambda i, j: (layer, 0, j)))
    return pl.pallas_call(
        functools.partial(_norm_matmul_kernel, w_is_transposed),
        grid=(T // tm, N // tn),
        in_specs=[pl.BlockSpec((tm, D), lambda i, j: (i, 0)),
                  pl.BlockSpec((1, D), lambda i, j: (0, 0)),
                  w_spec],
        out_specs=pl.BlockSpec((tm, tn), lambda i, j: (i, j)),
        out_shape=jax.ShapeDtypeStruct((T, N), out_dtype),
        scratch_shapes=[pltpu.VMEM((tm, D), BF)],
        compiler_params=_cparams(("parallel", "arbitrary")),
        name="norm_matmul",
    )(x, norm_w.reshape(1, D), w)


def _matmul_residual_kernel(a_ref, w_ref, r_ref, o_ref):
    o_ref[...] = r_ref[...] + jnp.dot(a_ref[...], w_ref[...], preferred_element_type=F32)


def matmul_residual(a, w, layer, r, tm=512, tn=1024):
    T, K = a.shape
    N = w.shape[2]
    tm, tn = min(tm, T), min(tn, N)
    return pl.pallas_call(
        _matmul_residual_kernel,
        grid=(T // tm, N // tn),
        in_specs=[pl.BlockSpec((tm, K), lambda i, j: (i, 0)),
                  pl.BlockSpec((None, K, tn), lambda i, j: (layer, 0, j)),
                  pl.BlockSpec((tm, tn), lambda i, j: (i, j))],
        out_specs=pl.BlockSpec((tm, tn), lambda i, j: (i, j)),
        out_shape=jax.ShapeDtypeStruct((T, N), F32),
        compiler_params=_cparams(("parallel", "arbitrary")),
        name="matmul_residual",
    )(a, w, r)


def _rmsnorm_kernel(x_ref, w_ref, o_ref):
    o_ref[...] = _rms(x_ref[...], w_ref[...], 1e-6)


def rmsnorm(x, w, tm=512):
    T, D = x.shape
    tm = min(tm, T)
    return pl.pallas_call(
        _rmsnorm_kernel,
        grid=(T // tm,),
        in_specs=[pl.BlockSpec((tm, D), lambda i: (i, 0)),
                  pl.BlockSpec((1, D), lambda i: (0, 0))],
        out_specs=pl.BlockSpec((tm, D), lambda i: (i, 0)),
        out_shape=jax.ShapeDtypeStruct((T, D), F32),
        compiler_params=_cparams(("parallel",)),
        name="final_rmsnorm",
    )(x, w.reshape(1, D))


def _hgrn2_kernel(layer, q_ref, f_ref, i_ref, g_ref, lb_ref, nw_ref, o_ref, state_ref):
    @pl.when(pl.program_id(1) == 0)
    def _():
        state_ref[...] = jnp.zeros_like(state_ref)

    lb_all = lb_ref[...]
    e = jnp.exp(lb_all - jnp.max(lb_all, axis=0, keepdims=True))
    soft = e / jnp.sum(e, axis=0, keepdims=True)
    lrow = lax.broadcasted_iota(jnp.int32, soft.shape, 0)
    lower_all = jnp.sum(jnp.where((lrow >= 1) & (lrow <= layer), soft, 0.0), axis=0, keepdims=True)

    C, SUB = HG_CHUNK, HG_SUB
    tt = q_ref.shape[0]
    row = lax.broadcasted_iota(jnp.int32, (C, C), 0)
    col = lax.broadcasted_iota(jnp.int32, (C, C), 1)
    tri = jnp.where(col <= row, 1.0, 0.0).astype(BF)
    tri_sub = jnp.where(col < (row // SUB) * SUB, 1.0, 0.0).astype(BF)
    tri2 = jnp.concatenate([tri, tri_sub], axis=0)
    nsub = C // SUB
    mrow = lax.broadcasted_iota(jnp.int32, (C, nsub * C), 0)
    mcol = lax.broadcasted_iota(jnp.int32, (C, nsub * C), 1)
    keep = jnp.where(mcol // C == mrow // SUB, mcol % C, C) <= mrow
    w = MIX_W

    def chunk(c, states):
        rows = slice(c * C, (c + 1) * C)
        f_all = lower_all + (1.0 - lower_all) * _sigmoid(f_ref[rows, :])
        log_f = jnp.log(f_all)
        hi, mid, lo = _split3(log_f)
        cum = jnp.dot(tri2, jnp.concatenate([hi, mid, lo], axis=1), preferred_element_type=F32)
        cum = cum[:, 0:w] + cum[:, w:2 * w] + cum[:, 2 * w:3 * w]
        b_all, b_sub_all = cum[0:C], cum[C:2 * C]
        b_end_all = jnp.sum(log_f, axis=0, keepdims=True)
        new_states = []
        for h in range(N_HEADS):
            cols = slice(h * HEAD_DIM, (h + 1) * HEAD_DIM)
            b, b_sub, b_end = b_all[:, cols], b_sub_all[:, cols], b_end_all[:, cols]
            q = _silu(q_ref[rows, cols])
            k = 1.0 - f_all[:, cols]
            v = i_ref[rows, cols]
            o_inter = _dot_nt(q * jnp.exp(b), states[h])
            qt = q * jnp.exp(b - b_sub)
            kt = jnp.concatenate(
                [k * jnp.exp(jnp.minimum(b_sub[s * SUB:s * SUB + 1] - b, HG_EXP_CLAMP)) for s in range(nsub)],
                axis=0)
            sc = jnp.where(keep, _dot_nt(qt, kt), 0.0)
            o = o_inter + _dot(sc, jnp.concatenate([v] * nsub, axis=0))
            o = _rms(o, nw_ref[...], 1e-6) * _silu(g_ref[rows, cols])
            o_ref[rows, cols] = o.astype(o_ref.dtype)
            new_states.append(states[h] * jnp.exp(b_end) + _dot_tn(v, k * jnp.exp(b_end - b)))
        return new_states

    states = [state_ref[h] for h in range(N_HEADS)]
    for c in range(tt // C):
        states = chunk(c, states)
    for h in range(N_HEADS):
        state_ref[h] = states[h]


def hgrn2_mixer(proj, lower_bounds, norm_w, layer, tt=256):
    B, S, _ = proj.shape
    L = lower_bounds.shape[0]
    tt = min(tt, S)
    seg = lambda off: pl.BlockSpec((None, tt, MIX_W), lambda b, t: (b, t, (BLK_HG + off) // 4))
    return pl.pallas_call(
        functools.partial(_hgrn2_kernel, layer),
        grid=(B, S // tt),
        in_specs=[seg(0), seg(4), seg(8), seg(12),
                  pl.BlockSpec((L, MIX_W), lambda b, t: (0, 0)),
                  pl.BlockSpec((1, LANE), lambda b, t: (0, 0))],
        out_specs=pl.BlockSpec((None, tt, MIX_W), lambda b, t: (b, t, 0)),
        out_shape=jax.ShapeDtypeStruct((B, S, MIX_W), BF),
        scratch_shapes=[pltpu.VMEM((N_HEADS, HEAD_DIM, HEAD_DIM), F32)],
        compiler_params=_cparams(("parallel", "arbitrary")),
        name="hgrn2",
    )(proj, proj, proj, proj, lower_bounds, norm_w.reshape(1, LANE))


def _flash_init(m_ref, acc_ref, vx_ref):
    m_ref[...] = jnp.full_like(m_ref, NEG)
    acc_ref[...] = jnp.zeros_like(acc_ref)
    vx_ref[:, LANE:2 * LANE] = jnp.ones((vx_ref.shape[0], LANE), BF)


def _flash_step(qs_ref, k_ref, v_ref, vx_ref, s_ref, p_ref, m_ref, acc_ref, rb, bias_fn, live_cols=None):
    nrows, tk = s_ref.shape
    k = k_ref[...].astype(BF)
    vx_ref[:, 0:LANE] = v_ref[...].astype(BF)
    chunk_cols = lambda c: tk if live_cols is None else live_cols((c + 1) * FLASH_MM_ROWS - rb)
    for c in range(nrows // FLASH_MM_ROWS):
        rows = slice(c * FLASH_MM_ROWS, (c + 1) * FLASH_MM_ROWS)
        s_ref[rows, 0:chunk_cols(c)] = _dot_nt(qs_ref[rows, :], k[0:chunk_cols(c)])
    for r in range(nrows // rb):
        rows = slice(r * rb, (r + 1) * rb)
        n = tk if live_cols is None else live_cols(r * rb)
        s = s_ref[rows, 0:n]
        if bias_fn is not None:
            s = s + bias_fn(r * rb, n)
        m_prev = m_ref[rows, :]
        m_new = jnp.maximum(m_prev, jnp.max(s, axis=-1, keepdims=True))
        p_ref[rows, 0:n] = jnp.exp2(s - jnp.concatenate([m_new] * (n // LANE), axis=1)).astype(BF)
        n_mm = chunk_cols(r * rb // FLASH_MM_ROWS)
        if n < n_mm:
            p_ref[rows, n:n_mm] = jnp.zeros((rb, n_mm - n), BF)
        alpha = jnp.exp2(m_prev - m_new)
        acc_ref[rows, :] = acc_ref[rows, :] * jnp.concatenate([alpha, alpha], axis=1)
        m_ref[rows, :] = m_new
    vx = vx_ref[...]
    for c in range(nrows // FLASH_MM_ROWS):
        rows = slice(c * FLASH_MM_ROWS, (c + 1) * FLASH_MM_ROWS)
        acc_ref[rows, :] += jnp.dot(p_ref[rows, 0:chunk_cols(c)], vx[0:chunk_cols(c)], preferred_element_type=F32)


def _diagonal_live_cols(t, rb):
    return lambda r0: min(t, -(-(r0 % t + rb) // LANE) * LANE)


def _diff_attn_kernel(lambda_init, tq, tk, rb, q_ref, k_ref, v_ref, lam_ref, nw_ref, o_ref,
                      qs_ref, m_ref, acc_ref, vx_ref, s_ref, p_ref):
    i, j = pl.program_id(2), pl.program_id(3)
    dc = HEAD_DIM // 2

    @pl.when(j == 0)
    def _():
        q = q_ref[...] * (dc ** -0.5 * LOG2E)
        first = lax.broadcasted_iota(jnp.int32, q.shape, 1) < dc
        qs_ref[0:tq, :] = jnp.where(first, q, 0.0).astype(BF)
        qs_ref[tq:2 * tq, :] = jnp.where(first, 0.0, q).astype(BF)
        _flash_init(m_ref, acc_ref, vx_ref)

    def causal_bias(r0, n):
        qloc = r0 % tq + lax.broadcasted_iota(jnp.int32, (rb, n), 0)
        col = lax.broadcasted_iota(jnp.int32, (rb, n), 1)
        return jnp.where(col - qloc <= i * tq - j * tk, 0.0, NEG)

    live_cols = _diagonal_live_cols(tq, rb) if tq == tk else None

    needed = j * tk <= i * tq + tq - 1
    unmasked = (j + 1) * tk - 1 <= i * tq

    @pl.when(needed & unmasked)
    def _():
        _flash_step(qs_ref, k_ref, v_ref, vx_ref, s_ref, p_ref, m_ref, acc_ref, rb, None)

    @pl.when(needed & jnp.logical_not(unmasked))
    def _():
        _flash_step(qs_ref, k_ref, v_ref, vx_ref, s_ref, p_ref, m_ref, acc_ref, rb, causal_bias, live_cols)

    @pl.when(j == pl.num_programs(3) - 1)
    def _():
        lam = (jnp.exp(jnp.sum(lam_ref[0:1] * lam_ref[1:2], axis=-1, keepdims=True))
               - jnp.exp(jnp.sum(lam_ref[2:3] * lam_ref[3:4], axis=-1, keepdims=True)) + lambda_init)
        o = acc_ref[:, 0:LANE] / acc_ref[:, LANE:2 * LANE]
        o = o[0:tq] - lam * o[tq:2 * tq]
        o = _rms(o, nw_ref[...], 1e-6) * (1.0 - lambda_init)
        o_ref[...] = o.astype(o_ref.dtype)


def diff_attention_mixer(proj, lam_params, norm_w, lambda_init, tq=1024, tk=1024, rb=16):
    B, S, _ = proj.shape
    tq, tk = min(tq, S), min(tk, S)
    last = lambda i: (i * tq + tq - 1) // tk
    return pl.pallas_call(
        functools.partial(_diff_attn_kernel, lambda_init, tq, tk, rb),
        grid=(B, N_HEADS, S // tq, S // tk),
        in_specs=[pl.BlockSpec((None, tq, LANE), lambda b, h, i, j: (b, i, BLK_DA + h)),
                  pl.BlockSpec((None, tk, LANE),
                               lambda b, h, i, j: (b, jnp.minimum(j, last(i)), BLK_DA + 4 + h)),
                  pl.BlockSpec((None, tk, LANE),
                               lambda b, h, i, j: (b, jnp.minimum(j, last(i)), BLK_DA + 8 + h)),
                  pl.BlockSpec((4, HEAD_DIM // 2), lambda b, h, i, j: (0, 0)),
                  pl.BlockSpec((1, LANE), lambda b, h, i, j: (0, 0))],
        out_specs=pl.BlockSpec((None, tq, LANE), lambda b, h, i, j: (b, i, h)),
        out_shape=jax.ShapeDtypeStruct((B, S, MIX_W), BF),
        scratch_shapes=[pltpu.VMEM((2 * tq, LANE), BF),
                        pltpu.VMEM((2 * tq, LANE), F32),
                        pltpu.VMEM((2 * tq, 2 * LANE), F32),
                        pltpu.VMEM((tk, 2 * LANE), BF),
                        pltpu.VMEM((2 * tq, tk), F32),
                        pltpu.VMEM((2 * tq, tk), BF)],
        compiler_params=_cparams(("parallel", "parallel", "parallel", "arbitrary")),
        name="diff_attention",
    )(proj, proj, proj, lam_params, norm_w.reshape(1, LANE))


def _compress_kernel(t_ref, pos_ref, w1_ref, w2_ref, o_ref):
    half = t_ref.shape[1]
    t = t_ref[...]
    a = _dot(t, w1_ref[0:half, :])
    bnext = _dot(t, w1_ref[half:2 * half, :])
    c0 = _dot(pos_ref[...], w1_ref[...])
    n = a.shape[0]
    hidden = a + pltpu.roll(bnext, n - 1, 0) + c0
    o_ref[...] = _dot(_silu(hidden), w2_ref[...])


def compress_tokens(t, pos, w1, w2):
    B, S, d = t.shape
    n_chunks = S // CMP_STRIDE
    hid = w1.shape[1]
    t = t.reshape(B, n_chunks, CMP_STRIDE * d)
    return pl.pallas_call(
        _compress_kernel,
        grid=(B,),
        in_specs=[pl.BlockSpec((None, n_chunks, CMP_STRIDE * d), lambda b: (b, 0, 0)),
                  pl.BlockSpec((1, CMP_LEN * d), lambda b: (0, 0)),
                  pl.BlockSpec((CMP_LEN * d, hid), lambda b: (0, 0)),
                  pl.BlockSpec((hid, d), lambda b: (0, 0))],
        out_specs=pl.BlockSpec((None, n_chunks, d), lambda b: (b, 0, 0)),
        out_shape=jax.ShapeDtypeStruct((B, n_chunks, d), F32),
        compiler_params=_cparams(("parallel",)),
        name="nsa_compress",
    )(t, pos.reshape(1, CMP_LEN * d), w1, w2)


def _cmp_select_kernel(tq, rb, q_ref, kc_ref, vc_ref, pool_ref, o_ref, sel_ref, s_ref, p_ref, imp_ref, bias_ref):
    i = pl.program_id(1)
    ncp = kc_ref.shape[0]
    n_slc = sel_ref.shape[1]
    scale = HEAD_DIM ** -0.5
    qpos = i * tq + lax.broadcasted_iota(jnp.int32, (tq, 1), 0)
    cmp_end = lax.broadcasted_iota(jnp.int32, (1, ncp), 1) * CMP_STRIDE + (CMP_LEN - 1)
    bias_ref[...] = jnp.where(cmp_end <= qpos, 0.0, NEG)
    kc = kc_ref[...].astype(BF)
    vc = vc_ref[...].astype(BF)
    for h in range(N_HEADS):
        cols = slice(h * HEAD_DIM, (h + 1) * HEAD_DIM)
        slot = h % 2
        s_ref[slot] = _dot_nt(q_ref[:, cols] * scale, kc)
        for r in range(tq // rb):
            rows = slice(r * rb, (r + 1) * rb)
            s = s_ref[slot, rows, :] + bias_ref[rows, :]
            e = jnp.exp(s - jnp.max(s, axis=-1, keepdims=True))
            seen = i * tq + r * rb + lax.broadcasted_iota(jnp.int32, (rb, 1), 0) >= CMP_LEN - 1
            p = e * jnp.where(seen, 1.0 / jnp.sum(e, axis=-1, keepdims=True), 0.0)
            p_ref[slot, rows, :] = p.astype(BF)
            imp_ref[rows, :] = p if h == 0 else imp_ref[rows, :] + p
        o_ref[:, cols] = jnp.dot(p_ref[slot], vc, preferred_element_type=F32)
    imp = imp_ref[...]
    pool_t = pool_ref[...]
    imp_hi, imp_mid, _ = _split3(imp)
    imp_slc = _dot_nt(pool_t, imp_hi) + _dot_nt(pool_t, imp_mid)
    blk = lax.broadcasted_iota(jnp.int32, (n_slc, 1), 0)
    cur = (i * tq + lax.broadcasted_iota(jnp.int32, (1, tq), 1)) // SLC_BLOCK
    valid = blk <= cur
    forced = (blk == 0) | (blk > cur - N_LOCAL_BLOCKS)
    n_forced = 1 + N_LOCAL_BLOCKS
    score = jnp.where(valid, jnp.where(forced, -2.0, imp_slc), -1.0)
    blk_f = blk.astype(F32)
    for _ in range(min(SLC_TOPK, n_slc) - n_forced):
        m = jnp.max(score, axis=0, keepdims=True)
        first = jnp.min(jnp.where(score == m, blk_f, float(n_slc)), axis=0, keepdims=True)
        score = jnp.where(blk_f == first, -2.0, score)
    sel_ref[...] = jnp.where(score == -2.0, 1.0, 0.0).T.astype(sel_ref.dtype)


def nsa_compressed_and_select(proj, kc, vc, tq=256, rb=16):
    B, S, _ = proj.shape
    ncp = kc.shape[1]
    n_slc = S // SLC_BLOCK
    n = jnp.arange(ncp)[:, None]
    jj = jnp.arange(n_slc)[None, :]
    w = n - (SLC_BLOCK // CMP_STRIDE) * jj + 1
    pool = jnp.where((w >= 0) & (w <= 4), jnp.where((w == 0) | (w == 4), 1.0, 2.0), 0.0).astype(BF).T
    return pl.pallas_call(
        functools.partial(_cmp_select_kernel, tq, rb),
        grid=(B, S // tq),
        in_specs=[pl.BlockSpec((None, tq, MIX_W), lambda b, i: (b, i, BLK_NQ // 4)),
                  pl.BlockSpec((None, ncp, HEAD_DIM), lambda b, i: (b, 0, 0)),
                  pl.BlockSpec((None, ncp, HEAD_DIM), lambda b, i: (b, 0, 0)),
                  pl.BlockSpec((n_slc, ncp), lambda b, i: (0, 0))],
        out_specs=[pl.BlockSpec((None, tq, MIX_W), lambda b, i: (b, i, 0)),
                   pl.BlockSpec((None, tq, n_slc), lambda b, i: (b, i, 0))],
        out_shape=[jax.ShapeDtypeStruct((B, S, MIX_W), F32),
                   jax.ShapeDtypeStruct((B, S, n_slc), BF)],
        scratch_shapes=[pltpu.VMEM((2, tq, ncp), F32), pltpu.VMEM((2, tq, ncp), BF),
                        pltpu.VMEM((tq, ncp), F32), pltpu.VMEM((tq, ncp), F32)],
        compiler_params=_cparams(("parallel", "parallel")),
        name="nsa_cmp_select",
    )(proj, kc, vc, pool)


def _slc_attn_kernel(tq, tk, rb, q_ref, k_ref, v_ref, sel_ref, exp_ref, o_ref,
                     qs_ref, m_ref, acc_ref, vx_ref, s_ref, p_ref, bias_ref):
    i, j = pl.program_id(1), pl.program_id(2)

    @pl.when(j == 0)
    def _():
        for h in range(N_HEADS):
            q = q_ref[:, h * HEAD_DIM:(h + 1) * HEAD_DIM] * (HEAD_DIM ** -0.5 * LOG2E)
            qs_ref[h * tq:(h + 1) * tq, :] = q.astype(BF)
        _flash_init(m_ref, acc_ref, vx_ref)

    def step(causal, live_cols):
        picked = jnp.dot(sel_ref[...], exp_ref[...], preferred_element_type=F32)
        bias = (picked - 1.0) * (-NEG)
        if causal:
            qpos = i * tq + lax.broadcasted_iota(jnp.int32, (tq, tk), 0)
            kpos = j * tk + lax.broadcasted_iota(jnp.int32, (tq, tk), 1)
            bias = jnp.where(kpos <= qpos, bias, NEG)
        bias_ref[...] = bias
        part_rows = s_ref.shape[0]
        for part in range(qs_ref.shape[0] // part_rows):
            view = lambda ref: ref.at[pl.ds(part * part_rows, part_rows)]
            _flash_step(view(qs_ref), k_ref, v_ref, vx_ref, s_ref, p_ref, view(m_ref), view(acc_ref), rb,
                        lambda r0, n: bias_ref[(r0 % tq):(r0 % tq) + rb, 0:n], live_cols)

    if tq == tk:
        pl.when(j < i)(lambda: step(False, None))
        pl.when(j == i)(lambda: step(True, _diagonal_live_cols(tq, rb)))
    else:
        pl.when(j * tk <= i * tq + tq - 1)(lambda: step(True, None))

    @pl.when(j == pl.num_programs(2) - 1)
    def _():
        for h in range(N_HEADS):
            rows = slice(h * tq, (h + 1) * tq)
            o_ref[:, h * HEAD_DIM:(h + 1) * HEAD_DIM] = acc_ref[rows, 0:LANE] / acc_ref[rows, LANE:2 * LANE]


def nsa_selected_attention(proj, sel, tq=1024, tk=1024, rb=16):
    B, S, _ = proj.shape
    tq, tk = min(tq, S), min(tk, S)
    n_slc = S // SLC_BLOCK
    expand = (jnp.arange(n_slc)[:, None] == (jnp.arange(S)[None, :] // SLC_BLOCK)).astype(BF)
    last = lambda i: (i * tq + tq - 1) // tk
    kv = lambda off: pl.BlockSpec((None, tk, LANE),
                                  lambda b, i, j: (b, jnp.minimum(j, last(i)), BLK_NKV + off))
    rows = N_HEADS * tq
    return pl.pallas_call(
        functools.partial(_slc_attn_kernel, tq, tk, rb),
        grid=(B, S // tq, S // tk),
        in_specs=[pl.BlockSpec((None, tq, MIX_W), lambda b, i, j: (b, i, BLK_NQ // 4)),
                  kv(2), kv(3),
                  pl.BlockSpec((None, tq, n_slc), lambda b, i, j: (b, i, 0)),
                  pl.BlockSpec((n_slc, tk), lambda b, i, j: (0, jnp.minimum(j, last(i))))],
        out_specs=pl.BlockSpec((None, tq, MIX_W), lambda b, i, j: (b, i, 0)),
        out_shape=jax.ShapeDtypeStruct((B, S, MIX_W), F32),
        scratch_shapes=[pltpu.VMEM((rows, LANE), BF),
                        pltpu.VMEM((rows, LANE), F32),
                        pltpu.VMEM((rows, 2 * LANE), F32),
                        pltpu.VMEM((tk, 2 * LANE), BF),
                        pltpu.VMEM((rows // 2, tk), F32),
                        pltpu.VMEM((rows // 2, tk), BF),
                        pltpu.VMEM((tq, tk), F32)],
        compiler_params=_cparams(("parallel", "parallel", "arbitrary")),
        name="nsa_selected",
    )(proj, proj, proj, sel, expand)


def _swa_combine_kernel(tq, rb, q_ref, kp_ref, kc_ref, vp_ref, vc_ref, g_ref, ocmp_ref, oslc_ref, bias_ref, o_ref,
                        s_ref, p_ref):
    scale = HEAD_DIM ** -0.5
    k = jnp.concatenate([kp_ref[...], kc_ref[...]], axis=0).astype(BF)
    v = jnp.concatenate([vp_ref[...], vc_ref[...]], axis=0).astype(BF)
    gates = _sigmoid(g_ref[...])
    for h in range(N_HEADS):
        cols = slice(h * HEAD_DIM, (h + 1) * HEAD_DIM)
        slot = h % 2
        s_ref[slot] = _dot_nt(q_ref[:, cols] * scale, k)
        for r in range(tq // rb):
            rows = slice(r * rb, (r + 1) * rb)
            s = s_ref[slot, rows, :] + bias_ref[rows, :]
            e = jnp.exp(s - jnp.max(s, axis=-1, keepdims=True))
            p_ref[slot, rows, :] = (e * (1.0 / jnp.sum(e, axis=-1, keepdims=True))).astype(BF)
        o_swa = jnp.dot(p_ref[slot], v, preferred_element_type=F32)
        o = (gates[:, 3 * h:3 * h + 1] * ocmp_ref[:, cols]
             + gates[:, 3 * h + 1:3 * h + 2] * oslc_ref[:, cols]
             + gates[:, 3 * h + 2:3 * h + 3] * o_swa)
        o_ref[:, cols] = o.astype(o_ref.dtype)


def nsa_window_and_combine(proj, o_cmp, o_slc):
    B, S, _ = proj.shape
    tq = SWA_WINDOW
    kvp = lambda off: pl.BlockSpec((None, tq, LANE), lambda b, i: (b, jnp.maximum(i - 1, 0), BLK_NKV + off))
    kvc = lambda off: pl.BlockSpec((None, tq, LANE), lambda b, i: (b, i, BLK_NKV + off))
    full = lambda: pl.BlockSpec((None, tq, MIX_W), lambda b, i: (b, i, 0))
    dist = jnp.arange(tq)[:, None] - (jnp.arange(SWA_WINDOW + tq)[None, :] - SWA_WINDOW)
    inside = (dist >= 0) & (dist < SWA_WINDOW)
    first = inside & (jnp.arange(SWA_WINDOW + tq)[None, :] >= SWA_WINDOW)
    bias = jnp.where(jnp.stack([first, inside]), 0.0, NEG).astype(F32)
    return pl.pallas_call(
        functools.partial(_swa_combine_kernel, tq, 16),
        grid=(B, S // tq),
        in_specs=[pl.BlockSpec((None, tq, MIX_W), lambda b, i: (b, i, BLK_NQ // 4)),
                  kvp(4), kvc(4), kvp(5), kvc(5),
                  pl.BlockSpec((None, tq, LANE), lambda b, i: (b, i, BLK_NG)),
                  full(), full(),
                  pl.BlockSpec((None, tq, SWA_WINDOW + tq), lambda b, i: (jnp.minimum(i, 1), 0, 0))],
        out_specs=full(),
        out_shape=jax.ShapeDtypeStruct((B, S, MIX_W), BF),
        scratch_shapes=[pltpu.VMEM((2, tq, SWA_WINDOW + tq), F32), pltpu.VMEM((2, tq, SWA_WINDOW + tq), BF)],
        compiler_params=_cparams(("parallel", "parallel")),
        name="nsa_window_combine",
    )(proj, proj, proj, proj, proj, proj, o_cmp, o_slc, bias)


def _gelu_tanh(x):
    return 0.5 * x * (1.0 + jnp.tanh(math.sqrt(2.0 / math.pi) * (x + 0.044715 * (x * x * x))))


def _gmlp_kernel(u_ref, v_ref, lnw_ref, lnb_ref, ws_ref, bs_ref, o_ref):
    C = ws_ref.shape[1]
    row = lax.broadcasted_iota(jnp.int32, (C, C), 0)
    col = lax.broadcasted_iota(jnp.int32, (C, C), 1)
    causal = col <= row
    for c in range(u_ref.shape[0] // C):
        rows = slice(c * C, (c + 1) * C)
        u = _gelu_tanh(u_ref[rows, :])
        v = _gelu_tanh(v_ref[rows, :])
        mu = jnp.mean(v, axis=-1, keepdims=True)
        var = jnp.mean(jnp.square(v - mu), axis=-1, keepdims=True)
        v = (v - mu) * lax.rsqrt(var + 1e-5) * lnw_ref[...] + lnb_ref[...]
        for g in range(N_HEADS):
            cols = slice(g * HEAD_DIM, (g + 1) * HEAD_DIM)
            w = jnp.where(causal, ws_ref[g], 0.0)
            vm = _dot(w, v[:, cols]) + bs_ref[:, cols]
            o_ref[rows, cols] = (u[:, cols] * vm).astype(o_ref.dtype)


def chunk_gmlp_mixer(proj, ln_w, ln_b, w_s, b_s, tt=512):
    B, S, _ = proj.shape
    G, C, _ = w_s.shape
    tt = min(tt, S)
    bias = jnp.repeat(b_s.T, HEAD_DIM, axis=1)
    return pl.pallas_call(
        _gmlp_kernel,
        grid=(B, S // tt),
        in_specs=[pl.BlockSpec((None, tt, MIX_W), lambda b, t: (b, t, BLK_GM // 4)),
                  pl.BlockSpec((None, tt, MIX_W), lambda b, t: (b, t, BLK_GM // 4 + 1)),
                  pl.BlockSpec((1, MIX_W), lambda b, t: (0, 0)),
                  pl.BlockSpec((1, MIX_W), lambda b, t: (0, 0)),
                  pl.BlockSpec((G, C, C), lambda b, t: (0, 0, 0)),
                  pl.BlockSpec((C, MIX_W), lambda b, t: (0, 0))],
        out_specs=pl.BlockSpec((None, tt, MIX_W), lambda b, t: (b, t, 0)),
        out_shape=jax.ShapeDtypeStruct((B, S, MIX_W), BF),
        compiler_params=_cparams(("parallel", "parallel")),
        name="chunk_gmlp",
    )(proj, proj, ln_w.reshape(1, MIX_W), ln_b.reshape(1, MIX_W), w_s, bias)


def _merge_kernel(oa_ref, ob_ref, oc_ref, od_ref, ga_ref, gb_ref, gc_ref, gd_ref, wb_ref, wo_ref, x_ref,
                  o_ref, acc_ref):
    j = pl.program_id(1)

    @pl.when(j == 0)
    def _():
        acc_ref[...] = jnp.zeros_like(acc_ref)

    merged = None
    for b, (o_b, g_b) in enumerate(((oa_ref, ga_ref), (ob_ref, gb_ref), (oc_ref, gc_ref), (od_ref, gd_ref))):
        term = _sigmoid(g_b[...]) * jnp.dot(o_b[...], wb_ref[b], preferred_element_type=F32)
        merged = term if merged is None else merged + term
    acc_ref[...] += jnp.dot(merged.astype(BF), wo_ref[...], preferred_element_type=F32)

    @pl.when(j == pl.num_programs(1) - 1)
    def _():
        o_ref[...] = x_ref[...] + acc_ref[...]


def merge_project(x, proj, branches, w_branch, w_out, layer, tm=512, tn=512):
    T, D = x.shape
    tm = min(tm, T)
    per = D // tn
    o_spec = pl.BlockSpec((tm, MIX_W), lambda i, j: (i, 0))
    g_spec = lambda b: pl.BlockSpec((tm, tn), lambda i, j: (i, (BLK_MG * LANE) // tn + b * per + j))
    return pl.pallas_call(
        _merge_kernel,
        grid=(T // tm, per),
        in_specs=[o_spec, o_spec, o_spec, o_spec, g_spec(0), g_spec(1), g_spec(2), g_spec(3),
                  pl.BlockSpec((None, N_BRANCH, MIX_W, tn), lambda i, j: (layer, 0, 0, j)),
                  pl.BlockSpec((None, tn, D), lambda i, j: (layer, j, 0)),
                  pl.BlockSpec((tm, D), lambda i, j: (i, 0))],
        out_specs=pl.BlockSpec((tm, D), lambda i, j: (i, 0)),
        out_shape=jax.ShapeDtypeStruct((T, D), F32),
        scratch_shapes=[pltpu.VMEM((tm, D), F32)],
        compiler_params=_cparams(("parallel", "arbitrary")),
        name="merge_project",
    )(*branches, proj, proj, proj, proj, w_branch, w_out, x)


def _cross_attn_kernel(q_ref, k_ref, v_ref, o_ref):
    dh = q_ref.shape[1] // XA_HEADS
    for h in range(XA_HEADS):
        cols = slice(h * dh, (h + 1) * dh)
        s = _dot_nt(q_ref[:, cols], k_ref[:, cols]) * (dh ** -0.5)
        m = jnp.max(s, axis=-1, keepdims=True)
        p = jnp.exp(s - m)
        p = p / jnp.sum(p, axis=-1, keepdims=True)
        o_ref[:, cols] = _dot(p, v_ref[:, cols]).astype(o_ref.dtype)


def cross_attention_core(q, kv, tq=512):
    B, S, D = q.shape
    M = kv.shape[1]
    tq = min(tq, S)
    return pl.pallas_call(
        _cross_attn_kernel,
        grid=(B, S // tq),
        in_specs=[pl.BlockSpec((None, tq, D), lambda b, i: (b, i, 0)),
                  pl.BlockSpec((None, M, D), lambda b, i: (b, 0, 0)),
                  pl.BlockSpec((None, M, D), lambda b, i: (b, 0, 1))],
        out_specs=pl.BlockSpec((None, tq, D), lambda b, i: (b, i, 0)),
        out_shape=jax.ShapeDtypeStruct((B, S, D), BF),
        compiler_params=_cparams(("parallel", "parallel")),
        name="cross_attention",
    )(q, kv, kv)


def _ffn_kernel(x_ref, nw_ref, wg_ref, wu_ref, wo_ref, o_ref, h_ref, acc_ref):
    j = pl.program_id(1)

    @pl.when(j == 0)
    def _():
        h_ref[...] = _rms(x_ref[...], nw_ref[...], 1e-6).astype(BF)
        acc_ref[...] = jnp.zeros_like(acc_ref)

    h = h_ref[...]
    gate = jnp.dot(h, wg_ref[...], preferred_element_type=F32)
    up = jnp.dot(h, wu_ref[...], preferred_element_type=F32)
    acc_ref[...] += jnp.dot((_silu(gate) * up).astype(BF), wo_ref[...], preferred_element_type=F32)

    @pl.when(j == pl.num_programs(1) - 1)
    def _():
        o_ref[...] = x_ref[...] + acc_ref[...]


def swiglu_ffn(x, norm_w, w_in, w_out, layer, tm=512, tf=512):
    T, D = x.shape
    F = w_out.shape[1]
    tm = min(tm, T)
    nf = F // tf
    return pl.pallas_call(
        _ffn_kernel,
        grid=(T // tm, nf),
        in_specs=[pl.BlockSpec((tm, D), lambda i, j: (i, 0)),
                  pl.BlockSpec((1, D), lambda i, j: (0, 0)),
                  pl.BlockSpec((None, D, tf), lambda i, j: (layer, 0, j)),
                  pl.BlockSpec((None, D, tf), lambda i, j: (layer, 0, nf + j)),
                  pl.BlockSpec((None, tf, D), lambda i, j: (layer, j, 0))],
        out_specs=pl.BlockSpec((tm, D), lambda i, j: (i, 0)),
        out_shape=jax.ShapeDtypeStruct((T, D), F32),
        scratch_shapes=[pltpu.VMEM((tm, D), BF), pltpu.VMEM((tm, D), F32)],
        compiler_params=_cparams(("parallel", "arbitrary")),
        name="swiglu_ffn",
    )(x, norm_w.reshape(1, D), w_in, w_in, w_out)


def _pack_w_in(w):
    n_gate = 3 * N_HEADS
    gate_end = BLK_NG * LANE + n_gate
    wt = jnp.swapaxes(w, 1, 2)
    pad = jnp.zeros((w.shape[0], BLK_GM * LANE - gate_end, w.shape[1]), w.dtype)
    packed = jnp.concatenate([wt[:, :gate_end], pad, wt[:, gate_end:]], axis=1)
    assert packed.shape[1] == N_PROJ, packed.shape
    return packed.astype(BF)


def kernel(x, mem, mem_norm_w, mix_norm_w, w_in, hg_lower_bounds, hg_norm_w, da_lambda_q1, da_lambda_k1, da_lambda_q2, da_lambda_k2, da_norm_w, nsa_pos_k, nsa_cmp_w1_k, nsa_cmp_w2_k, nsa_pos_v, nsa_cmp_w1_v, nsa_cmp_w2_v, gm_ln_w, gm_ln_b, gm_w_s, gm_b_s, w_branch, w_out, xa_norm_w, xa_w_q, xa_w_kv, xa_w_o, ffn_norm_w, ffn_w_in, ffn_w_out, final_norm_w):
    B, S, D = x.shape
    M = mem.shape[1]
    depth = w_in.shape[0]
    assert D == D_MODEL and S % (2 * SWA_WINDOW) == 0, (S, D)
    T = B * S
    x = x.reshape(T, D)
    mem2 = mem.reshape(B * M, D)
    w_in_p = _pack_w_in(w_in)
    w_branch_b, w_out_b = w_branch.astype(BF), w_out.astype(BF)
    xa_w_q_b, xa_w_kv_b, xa_w_o_b = xa_w_q.astype(BF), xa_w_kv.astype(BF), xa_w_o.astype(BF)
    ffn_w_in_b, ffn_w_out_b = ffn_w_in.astype(BF), ffn_w_out.astype(BF)
    for l in range(depth):
        lambda_init = 0.8 - 0.6 * math.exp(-0.3 * l)
        proj = norm_matmul(x, mix_norm_w[l], w_in_p, l, F32, tm=1024, w_is_transposed=True).reshape(B, S, N_PROJ)
        o_a = hgrn2_mixer(proj, hg_lower_bounds, hg_norm_w[l], l)
        lam_params = jnp.stack([da_lambda_q1[l], da_lambda_k1[l], da_lambda_q2[l], da_lambda_k2[l]])
        o_b = diff_attention_mixer(proj, lam_params, da_norm_w[l], lambda_init)
        nkv = lambda off: proj[:, :, (BLK_NKV + off) * LANE:(BLK_NKV + off + 1) * LANE]
        kc = compress_tokens(nkv(0), nsa_pos_k[l], nsa_cmp_w1_k[l].astype(BF), nsa_cmp_w2_k[l].astype(BF))
        vc = compress_tokens(nkv(1), nsa_pos_v[l], nsa_cmp_w1_v[l].astype(BF), nsa_cmp_w2_v[l].astype(BF))
        o_cmp, sel = nsa_compressed_and_select(proj, kc, vc)
        o_slc = nsa_selected_attention(proj, sel)
        o_c = nsa_window_and_combine(proj, o_cmp, o_slc)
        o_d = chunk_gmlp_mixer(proj, gm_ln_w[l], gm_ln_b[l], gm_w_s[l], gm_b_s[l])
        branches = [o.reshape(T, MIX_W) for o in (o_a, o_b, o_c, o_d)]
        x = merge_project(x, proj.reshape(T, N_PROJ), branches, w_branch_b, w_out_b, l)
        q = norm_matmul(x, xa_norm_w[l], xa_w_q_b, l, BF, tm=1024).reshape(B, S, D)
        kv = norm_matmul(mem2, mem_norm_w, xa_w_kv_b, l, F32).reshape(B, M, 2 * D)
        att = cross_attention_core(q, kv).reshape(T, D)
        x = matmul_residual(att, xa_w_o_b, l, x, tm=1024)
        x = swiglu_ffn(x, ffn_norm_w[l], ffn_w_in_b, ffn_w_out_b, l)
    return rmsnorm(x, final_norm_w).reshape(B, S, D)
```

```python
import functools
import math

import jax
import jax.numpy as jnp
from jax import lax
from jax.experimental import pallas as pl
from jax.experimental.pallas import tpu as pltpu

BF = jnp.bfloat16
F32 = jnp.float32

LANE = 128
D_MODEL = 2048
HEAD_DIM = 128
N_HEADS = 4
MIX_W = N_HEADS * HEAD_DIM
N_BRANCH = 4
HG_CHUNK = 64
HG_SUB = 16
HG_EXP_CLAMP = 80.0
CMP_LEN = 32
CMP_STRIDE = 16
SLC_BLOCK = 64
SLC_TOPK = 16
N_LOCAL_BLOCKS = 2
SWA_WINDOW = 512
XA_HEADS = 4
NEG = -1e30
LOG2E = math.log2(math.e)
FLASH_MM_ROWS = 256
VMEM_LIMIT = 56 * 1024 * 1024

BLK_HG = 0
BLK_DA = 16
BLK_NQ = 28
BLK_NKV = 32
BLK_NG = 38
BLK_GM = 40
BLK_MG = 48
N_PROJ = 112 * LANE


def _cparams(sem):
    return pltpu.CompilerParams(dimension_semantics=sem, vmem_limit_bytes=VMEM_LIMIT)


def _dot(a, b):
    return jnp.dot(a.astype(BF), b.astype(BF), preferred_element_type=F32)


def _dot_nt(a, b):
    return lax.dot_general(a.astype(BF), b.astype(BF), (((1,), (1,)), ((), ())),
                           preferred_element_type=F32)


def _dot_tn(a, b):
    return lax.dot_general(a.astype(BF), b.astype(BF), (((0,), (0,)), ((), ())),
                           preferred_element_type=F32)


def _split3(a):
    hi = a.astype(BF)
    r = a - hi.astype(F32)
    mid = r.astype(BF)
    lo = (r - mid.astype(F32)).astype(BF)
    return hi, mid, lo


def _rms(x, w, eps):
    return x * lax.rsqrt(jnp.mean(x * x, axis=-1, keepdims=True) + eps) * w


def _sigmoid(x):
    return 1.0 / (1.0 + jnp.exp(-x))


def _silu(x):
    return x * _sigmoid(x)


def _norm_matmul_kernel(w_is_transposed, x_ref, nw_ref, w_ref, o_ref, h_ref):
    @pl.when(pl.program_id(1) == 0)
    def _():
        h_ref[...] = _rms(x_ref[...], nw_ref[...], 1e-6).astype(BF)

    if w_is_transposed:
        o = lax.dot_general(h_ref[...], w_ref[...], (((1,), (1,)), ((), ())), preferred_element_type=F32)
    else:
        o = jnp.dot(h_ref[...], w_ref[...], preferred_element_type=F32)
    o_ref[...] = o.astype(o_ref.dtype)


def norm_matmul(x, norm_w, w, layer, out_dtype, tm=512, tn=1024, w_is_transposed=False):
    T, D = x.shape
    N = w.shape[1] if w_is_transposed else w.shape[2]
    tm, tn = min(tm, T), min(tn, N)
    w_spec = (pl.BlockSpec((None, tn, D), lambda i, j: (layer, j, 0)) if w_is_transposed
              else pl.BlockSpec((None, D, tn), lambda i, j: (layer, 0, j)))
    return pl.pallas_call(
        functools.partial(_norm_matmul_kernel, w_is_transposed),
        grid=(T // tm, N // tn),
        in_specs=[pl.BlockSpec((tm, D), lambda i, j: (i, 0)),
                  pl.BlockSpec((1, D), lambda i, j: (0, 0)),
                  w_spec],
        out_specs=pl.BlockSpec((tm, tn), lambda i, j: (i, j)),
        out_shape=jax.ShapeDtypeStruct((T, N), out_dtype),
        scratch_shapes=[pltpu.VMEM((tm, D), BF)],
        compiler_params=_cparams(("parallel", "arbitrary")),
        name="norm_matmul",
    )(x, norm_w.reshape(1, D), w)


def _matmul_residual_kernel(a_ref, w_ref, r_ref, o_ref):
    o_ref[...] = r_ref[...] + jnp.dot(a_ref[...], w_ref[...], preferred_element_type=F32)


def matmul_residual(a, w, layer, r, tm=512, tn=1024):
    T, K = a.shape
    N = w.shape[2]
    tm, tn = min(tm, T), min(tn, N)
    return pl.pallas_call(
        _matmul_residual_kernel,
        grid=(T // tm, N // tn),
        in_specs=[pl.BlockSpec((tm, K), lambda i, j: (i, 0)),
                  pl.BlockSpec((None, K, tn), lambda i, j: (layer, 0, j)),
                  pl.BlockSpec((tm, tn), lambda i, j: (i, j))],
        out_specs=pl.BlockSpec((tm, tn), lambda i, j: (i, j)),
        out_shape=jax.ShapeDtypeStruct((T, N), F32),
        compiler_params=_cparams(("parallel", "arbitrary")),
        name="matmul_residual",
    )(a, w, r)


def _rmsnorm_kernel(x_ref, w_ref, o_ref):
    o_ref[...] = _rms(x_ref[...], w_ref[...], 1e-6)


def rmsnorm(x, w, tm=512):
    T, D = x.shape
    tm = min(tm, T)
    return pl.pallas_call(
        _rmsnorm_kernel,
        grid=(T // tm,),
        in_specs=[pl.BlockSpec((tm, D), lambda i: (i, 0)),
                  pl.BlockSpec((1, D), lambda i: (0, 0))],
        out_specs=pl.BlockSpec((tm, D), lambda i: (i, 0)),
        out_shape=jax.ShapeDtypeStruct((T, D), F32),
        compiler_params=_cparams(("parallel",)),
        name="final_rmsnorm",
    )(x, w.reshape(1, D))


def _hgrn2_kernel(layer, q_ref, f_ref, i_ref, g_ref, lb_ref, nw_ref, o_ref, state_ref):
    @pl.when(pl.program_id(1) == 0)
    def _():
        state_ref[...] = jnp.zeros_like(state_ref)

    lb_all = lb_ref[...]
    e = jnp.exp(lb_all - jnp.max(lb_all, axis=0, keepdims=True))
    soft = e / jnp.sum(e, axis=0, keepdims=True)
    lrow = lax.broadcasted_iota(jnp.int32, soft.shape, 0)
    lower_all = jnp.sum(jnp.where((lrow >= 1) & (lrow <= layer), soft, 0.0), axis=0, keepdims=True)

    C, SUB = HG_CHUNK, HG_SUB
    tt = q_ref.shape[0]
    row = lax.broadcasted_iota(jnp.int32, (C, C), 0)
    col = lax.broadcasted_iota(jnp.int32, (C, C), 1)
    tri = jnp.where(col <= row, 1.0, 0.0).astype(BF)
    tri_sub = jnp.where(col < (row // SUB) * SUB, 1.0, 0.0).astype(BF)
    tri2 = jnp.concatenate([tri, tri_sub], axis=0)
    nsub = C // SUB
    mrow = lax.broadcasted_iota(jnp.int32, (C, nsub * C), 0)
    mcol = lax.broadcasted_iota(jnp.int32, (C, nsub * C), 1)
    keep = jnp.where(mcol // C == mrow // SUB, mcol % C, C) <= mrow
    w = MIX_W

    def chunk(c, states):
        rows = slice(c * C, (c + 1) * C)
        f_all = lower_all + (1.0 - lower_all) * _sigmoid(f_ref[rows, :])
        log_f = jnp.log(f_all)
        hi, mid, lo = _split3(log_f)
        cum = jnp.dot(tri2, jnp.concatenate([hi, mid, lo], axis=1), preferred_element_type=F32)
        cum = cum[:, 0:w] + cum[:, w:2 * w] + cum[:, 2 * w:3 * w]
        b_all, b_sub_all = cum[0:C], cum[C:2 * C]
        b_end_all = jnp.sum(log_f, axis=0, keepdims=True)
        new_states = []
        for h in range(N_HEADS):
            cols = slice(h * HEAD_DIM, (h + 1) * HEAD_DIM)
            b, b_sub, b_end = b_all[:, cols], b_sub_all[:, cols], b_end_all[:, cols]
            q = _silu(q_ref[rows, cols])
            k = 1.0 - f_all[:, cols]
            v = i_ref[rows, cols]
            o_inter = _dot_nt(q * jnp.exp(b), states[h])
            qt = q * jnp.exp(b - b_sub)
            kt = jnp.concatenate(
                [k * jnp.exp(jnp.minimum(b_sub[s * SUB:s * SUB + 1] - b, HG_EXP_CLAMP)) for s in range(nsub)],
                axis=0)
            sc = jnp.where(keep, _dot_nt(qt, kt), 0.0)
            o = o_inter + _dot(sc, jnp.concatenate([v] * nsub, axis=0))
            o = _rms(o, nw_ref[...], 1e-6) * _silu(g_ref[rows, cols])
            o_ref[rows, cols] = o.astype(o_ref.dtype)
            new_states.append(states[h] * jnp.exp(b_end) + _dot_tn(v, k * jnp.exp(b_end - b)))
        return new_states

    states = [state_ref[h] for h in range(N_HEADS)]
    for c in range(tt // C):
        states = chunk(c, states)
    for h in range(N_HEADS):
        state_ref[h] = states[h]


def hgrn2_mixer(proj, lower_bounds, norm_w, layer, tt=256):
    B, S, _ = proj.shape
    L = lower_bounds.shape[0]
    tt = min(tt, S)
    seg = lambda off: pl.BlockSpec((None, tt, MIX_W), lambda b, t: (b, t, (BLK_HG + off) // 4))
    return pl.pallas_call(
        functools.partial(_hgrn2_kernel, layer),
        grid=(B, S // tt),
        in_specs=[seg(0), seg(4), seg(8), seg(12),
                  pl.BlockSpec((L, MIX_W), lambda b, t: (0, 0)),
                  pl.BlockSpec((1, LANE), lambda b, t: (0, 0))],
        out_specs=pl.BlockSpec((None, tt, MIX_W), lambda b, t: (b, t, 0)),
        out_shape=jax.ShapeDtypeStruct((B, S, MIX_W), BF),
        scratch_shapes=[pltpu.VMEM((N_HEADS, HEAD_DIM, HEAD_DIM), F32)],
        compiler_params=_cparams(("parallel", "arbitrary")),
        name="hgrn2",
    )(proj, proj, proj, proj, lower_bounds, norm_w.reshape(1, LANE))


def _flash_init(m_ref, acc_ref, vx_ref):
    m_ref[...] = jnp.full_like(m_ref, NEG)
    acc_ref[...] = jnp.zeros_like(acc_ref)
    vx_ref[:, LANE:2 * LANE] = jnp.ones((vx_ref.shape[0], LANE), BF)


def _flash_step(qs_ref, k_ref, v_ref, vx_ref, s_ref, p_ref, m_ref, acc_ref, rb, bias_fn, live_cols=None):
    nrows, tk = s_ref.shape
    k = k_ref[...].astype(BF)
    vx_ref[:, 0:LANE] = v_ref[...].astype(BF)
    chunk_cols = lambda c: tk if live_cols is None else live_cols((c + 1) * FLASH_MM_ROWS - rb)
    for c in range(nrows // FLASH_MM_ROWS):
        rows = slice(c * FLASH_MM_ROWS, (c + 1) * FLASH_MM_ROWS)
        s_ref[rows, 0:chunk_cols(c)] = _dot_nt(qs_ref[rows, :], k[0:chunk_cols(c)])
    for r in range(nrows // rb):
        rows = slice(r * rb, (r + 1) * rb)
        n = tk if live_cols is None else live_cols(r * rb)
        s = s_ref[rows, 0:n]
        if bias_fn is not None:
            s = s + bias_fn(r * rb, n)
        m_prev = m_ref[rows, :]
        m_new = jnp.maximum(m_prev, jnp.max(s, axis=-1, keepdims=True))
        p_ref[rows, 0:n] = jnp.exp2(s - jnp.concatenate([m_new] * (n // LANE), axis=1)).astype(BF)
        n_mm = chunk_cols(r * rb // FLASH_MM_ROWS)
        if n < n_mm:
            p_ref[rows, n:n_mm] = jnp.zeros((rb, n_mm - n), BF)
        alpha = jnp.exp2(m_prev - m_new)
        acc_ref[rows, :] = acc_ref[rows, :] * jnp.concatenate([alpha, alpha], axis=1)
        m_ref[rows, :] = m_new
    vx = vx_ref[...]
    for c in range(nrows // FLASH_MM_ROWS):
        rows = slice(c * FLASH_MM_ROWS, (c + 1) * FLASH_MM_ROWS)
        acc_ref[rows, :] += jnp.dot(p_ref[rows, 0:chunk_cols(c)], vx[0:chunk_cols(c)], preferred_element_type=F32)


def _diagonal_live_cols(t, rb):
    return lambda r0: min(t, -(-(r0 % t + rb) // LANE) * LANE)


def _diff_attn_kernel(lambda_init, tq, tk, rb, q_ref, k_ref, v_ref, lam_ref, nw_ref, o_ref,
                      qs_ref, m_ref, acc_ref, vx_ref, s_ref, p_ref):
    i, j = pl.program_id(2), pl.program_id(3)
    dc = HEAD_DIM // 2

    @pl.when(j == 0)
    def _():
        q = q_ref[...] * (dc ** -0.5 * LOG2E)
        first = lax.broadcasted_iota(jnp.int32, q.shape, 1) < dc
        qs_ref[0:tq, :] = jnp.where(first, q, 0.0).astype(BF)
        qs_ref[tq:2 * tq, :] = jnp.where(first, 0.0, q).astype(BF)
        _flash_init(m_ref, acc_ref, vx_ref)

    def causal_bias(r0, n):
        qloc = r0 % tq + lax.broadcasted_iota(jnp.int32, (rb, n), 0)
        col = lax.broadcasted_iota(jnp.int32, (rb, n), 1)
        return jnp.where(col - qloc <= i * tq - j * tk, 0.0, NEG)

    live_cols = _diagonal_live_cols(tq, rb) if tq == tk else None

    needed = j * tk <= i * tq + tq - 1
    unmasked = (j + 1) * tk - 1 <= i * tq

    def step(bias_fn, cols_fn):
        for part in range(2):
            view = lambda ref: ref.at[pl.ds(part * tq, tq)]
            _flash_step(view(qs_ref), k_ref, v_ref, vx_ref, s_ref, p_ref, view(m_ref), view(acc_ref), rb,
                        bias_fn, cols_fn)

    pl.when(needed & unmasked)(lambda: step(None, None))
    pl.when(needed & jnp.logical_not(unmasked))(lambda: step(causal_bias, live_cols))

    @pl.when(j == pl.num_programs(3) - 1)
    def _():
        lam = (jnp.exp(jnp.sum(lam_ref[0:1] * lam_ref[1:2], axis=-1, keepdims=True))
               - jnp.exp(jnp.sum(lam_ref[2:3] * lam_ref[3:4], axis=-1, keepdims=True)) + lambda_init)
        o = acc_ref[:, 0:LANE] / acc_ref[:, LANE:2 * LANE]
        o = o[0:tq] - lam * o[tq:2 * tq]
        o = _rms(o, nw_ref[...], 1e-6) * (1.0 - lambda_init)
        o_ref[...] = o.astype(o_ref.dtype)


def diff_attention_mixer(proj, lam_params, norm_w, lambda_init, tq=2048, tk=2048, rb=16):
    B, S, _ = proj.shape
    tq, tk = min(tq, S), min(tk, S)
    last = lambda i: (i * tq + tq - 1) // tk
    return pl.pallas_call(
        functools.partial(_diff_attn_kernel, lambda_init, tq, tk, rb),
        grid=(B, N_HEADS, S // tq, S // tk),
        in_specs=[pl.BlockSpec((None, tq, LANE), lambda b, h, i, j: (b, i, BLK_DA + h)),
                  pl.BlockSpec((None, tk, LANE),
                               lambda b, h, i, j: (b, jnp.minimum(j, last(i)), BLK_DA + 4 + h)),
                  pl.BlockSpec((None, tk, LANE),
                               lambda b, h, i, j: (b, jnp.minimum(j, last(i)), BLK_DA + 8 + h)),
                  pl.BlockSpec((4, HEAD_DIM // 2), lambda b, h, i, j: (0, 0)),
                  pl.BlockSpec((1, LANE), lambda b, h, i, j: (0, 0))],
        out_specs=pl.BlockSpec((None, tq, LANE), lambda b, h, i, j: (b, i, h)),
        out_shape=jax.ShapeDtypeStruct((B, S, MIX_W), BF),
        scratch_shapes=[pltpu.VMEM((2 * tq, LANE), BF),
                        pltpu.VMEM((2 * tq, LANE), F32),
                        pltpu.VMEM((2 * tq, 2 * LANE), F32),
                        pltpu.VMEM((tk, 2 * LANE), BF),
                        pltpu.VMEM((tq, tk), F32),
                        pltpu.VMEM((tq, tk), BF)],
        compiler_params=_cparams(("parallel", "parallel", "parallel", "arbitrary")),
        name="diff_attention",
    )(proj, proj, proj, lam_params, norm_w.reshape(1, LANE))


def _compress_kernel(t_ref, pos_ref, w1_ref, w2_ref, o_ref):
    half = t_ref.shape[1]
    t = t_ref[...]
    a = _dot(t, w1_ref[0:half, :])
    bnext = _dot(t, w1_ref[half:2 * half, :])
    c0 = _dot(pos_ref[...], w1_ref[...])
    n = a.shape[0]
    hidden = a + pltpu.roll(bnext, n - 1, 0) + c0
    o_ref[...] = _dot(_silu(hidden), w2_ref[...])


def compress_tokens(t, pos, w1, w2):
    B, S, d = t.shape
    n_chunks = S // CMP_STRIDE
    hid = w1.shape[1]
    t = t.reshape(B, n_chunks, CMP_STRIDE * d)
    return pl.pallas_call(
        _compress_kernel,
        grid=(B,),
        in_specs=[pl.BlockSpec((None, n_chunks, CMP_STRIDE * d), lambda b: (b, 0, 0)),
                  pl.BlockSpec((1, CMP_LEN * d), lambda b: (0, 0)),
                  pl.BlockSpec((CMP_LEN * d, hid), lambda b: (0, 0)),
                  pl.BlockSpec((hid, d), lambda b: (0, 0))],
        out_specs=pl.BlockSpec((None, n_chunks, d), lambda b: (b, 0, 0)),
        out_shape=jax.ShapeDtypeStruct((B, n_chunks, d), F32),
        compiler_params=_cparams(("parallel",)),
        name="nsa_compress",
    )(t, pos.reshape(1, CMP_LEN * d), w1, w2)


def _cmp_select_kernel(tq, rb, q_ref, kc_ref, vc_ref, pool_ref, o_ref, sel_ref, s_ref, p_ref, imp_ref, bias_ref):
    i = pl.program_id(1)
    ncp = kc_ref.shape[0]
    n_slc = sel_ref.shape[1]
    scale = HEAD_DIM ** -0.5
    qpos = i * tq + lax.broadcasted_iota(jnp.int32, (tq, 1), 0)
    cmp_end = lax.broadcasted_iota(jnp.int32, (1, ncp), 1) * CMP_STRIDE + (CMP_LEN - 1)
    bias_ref[...] = jnp.where(cmp_end <= qpos, 0.0, NEG)
    kc = kc_ref[...].astype(BF)
    vc = vc_ref[...].astype(BF)
    for h in range(N_HEADS):
        cols = slice(h * HEAD_DIM, (h + 1) * HEAD_DIM)
        slot = h % 2
        s_ref[slot] = _dot_nt(q_ref[:, cols] * scale, kc)
        for r in range(tq // rb):
            rows = slice(r * rb, (r + 1) * rb)
            s = s_ref[slot, rows, :] + bias_ref[rows, :]
            e = jnp.exp(s - jnp.max(s, axis=-1, keepdims=True))
            seen = i * tq + r * rb + lax.broadcasted_iota(jnp.int32, (rb, 1), 0) >= CMP_LEN - 1
            p = e * jnp.where(seen, 1.0 / jnp.sum(e, axis=-1, keepdims=True), 0.0)
            p_ref[slot, rows, :] = p.astype(BF)
            imp_ref[rows, :] = p if h == 0 else imp_ref[rows, :] + p
        o_ref[:, cols] = jnp.dot(p_ref[slot], vc, preferred_element_type=F32)
    imp = imp_ref[...]
    pool_t = pool_ref[...]
    imp_hi, imp_mid, _ = _split3(imp)
    imp_slc = _dot_nt(pool_t, imp_hi) + _dot_nt(pool_t, imp_mid)
    blk = lax.broadcasted_iota(jnp.int32, (n_slc, 1), 0)
    cur = (i * tq + lax.broadcasted_iota(jnp.int32, (1, tq), 1)) // SLC_BLOCK
    valid = blk <= cur
    forced = (blk == 0) | (blk > cur - N_LOCAL_BLOCKS)
    n_forced = 1 + N_LOCAL_BLOCKS
    score = jnp.where(valid, jnp.where(forced, -2.0, imp_slc), -1.0)
    blk_f = blk.astype(F32)
    for _ in range(min(SLC_TOPK, n_slc) - n_forced):
        m = jnp.max(score, axis=0, keepdims=True)
        first = jnp.min(jnp.where(score == m, blk_f, float(n_slc)), axis=0, keepdims=True)
        score = jnp.where(blk_f == first, -2.0, score)
    sel_ref[...] = jnp.where(score == -2.0, 1.0, 0.0).T.astype(sel_ref.dtype)


def nsa_compressed_and_select(proj, kc, vc, tq=256, rb=16):
    B, S, _ = proj.shape
    ncp = kc.shape[1]
    n_slc = S // SLC_BLOCK
    n = jnp.arange(ncp)[:, None]
    jj = jnp.arange(n_slc)[None, :]
    w = n - (SLC_BLOCK // CMP_STRIDE) * jj + 1
    pool = jnp.where((w >= 0) & (w <= 4), jnp.where((w == 0) | (w == 4), 1.0, 2.0), 0.0).astype(BF).T
    return pl.pallas_call(
        functools.partial(_cmp_select_kernel, tq, rb),
        grid=(B, S // tq),
        in_specs=[pl.BlockSpec((None, tq, MIX_W), lambda b, i: (b, i, BLK_NQ // 4)),
                  pl.BlockSpec((None, ncp, HEAD_DIM), lambda b, i: (b, 0, 0)),
                  pl.BlockSpec((None, ncp, HEAD_DIM), lambda b, i: (b, 0, 0)),
                  pl.BlockSpec((n_slc, ncp), lambda b, i: (0, 0))],
        out_specs=[pl.BlockSpec((None, tq, MIX_W), lambda b, i: (b, i, 0)),
                   pl.BlockSpec((None, tq, n_slc), lambda b, i: (b, i, 0))],
        out_shape=[jax.ShapeDtypeStruct((B, S, MIX_W), F32),
                   jax.ShapeDtypeStruct((B, S, n_slc), BF)],
        scratch_shapes=[pltpu.VMEM((2, tq, ncp), F32), pltpu.VMEM((2, tq, ncp), BF),
                        pltpu.VMEM((tq, ncp), F32), pltpu.VMEM((tq, ncp), F32)],
        compiler_params=_cparams(("parallel", "parallel")),
        name="nsa_cmp_select",
    )(proj, kc, vc, pool)


def _slc_attn_kernel(tq, tk, rb, q_ref, k_ref, v_ref, sel_ref, exp_ref, o_ref,
                     qs_ref, m_ref, acc_ref, vx_ref, s_ref, p_ref, bias_ref):
    i, j = pl.program_id(1), pl.program_id(2)

    @pl.when(j == 0)
    def _():
        for h in range(N_HEADS):
            q = q_ref[:, h * HEAD_DIM:(h + 1) * HEAD_DIM] * (HEAD_DIM ** -0.5 * LOG2E)
            qs_ref[h * tq:(h + 1) * tq, :] = q.astype(BF)
        _flash_init(m_ref, acc_ref, vx_ref)

    def step(causal, live_cols):
        picked = jnp.dot(sel_ref[...], exp_ref[...], preferred_element_type=F32)
        bias = (picked - 1.0) * (-NEG)
        if causal:
            qpos = i * tq + lax.broadcasted_iota(jnp.int32, (tq, tk), 0)
            kpos = j * tk + lax.broadcasted_iota(jnp.int32, (tq, tk), 1)
            bias = jnp.where(kpos <= qpos, bias, NEG)
        bias_ref[...] = bias
        part_rows = s_ref.shape[0]
        for part in range(qs_ref.shape[0] // part_rows):
            view = lambda ref: ref.at[pl.ds(part * part_rows, part_rows)]
            _flash_step(view(qs_ref), k_ref, v_ref, vx_ref, s_ref, p_ref, view(m_ref), view(acc_ref), rb,
                        lambda r0, n: bias_ref[(r0 % tq):(r0 % tq) + rb, 0:n], live_cols)

    if tq == tk:
        pl.when(j < i)(lambda: step(False, None))
        pl.when(j == i)(lambda: step(True, _diagonal_live_cols(tq, rb)))
    else:
        pl.when(j * tk <= i * tq + tq - 1)(lambda: step(True, None))

    @pl.when(j == pl.num_programs(2) - 1)
    def _():
        for h in range(N_HEADS):
            rows = slice(h * tq, (h + 1) * tq)
            o_ref[:, h * HEAD_DIM:(h + 1) * HEAD_DIM] = acc_ref[rows, 0:LANE] / acc_ref[rows, LANE:2 * LANE]


def nsa_selected_attention(proj, sel, tq=1024, tk=1024, rb=16):
    B, S, _ = proj.shape
    tq, tk = min(tq, S), min(tk, S)
    n_slc = S // SLC_BLOCK
    expand = (jnp.arange(n_slc)[:, None] == (jnp.arange(S)[None, :] // SLC_BLOCK)).astype(BF)
    last = lambda i: (i * tq + tq - 1) // tk
    kv = lambda off: pl.BlockSpec((None, tk, LANE),
                                  lambda b, i, j: (b, jnp.minimum(j, last(i)), BLK_NKV + off))
    rows = N_HEADS * tq
    return pl.pallas_call(
        functools.partial(_slc_attn_kernel, tq, tk, rb),
        grid=(B, S // tq, S // tk),
        in_specs=[pl.BlockSpec((None, tq, MIX_W), lambda b, i, j: (b, i, BLK_NQ // 4)),
                  kv(2), kv(3),
                  pl.BlockSpec((None, tq, n_slc), lambda b, i, j: (b, i, 0)),
                  pl.BlockSpec((n_slc, tk), lambda b, i, j: (0, jnp.minimum(j, last(i))))],
        out_specs=pl.BlockSpec((None, tq, MIX_W), lambda b, i, j: (b, i, 0)),
        out_shape=jax.ShapeDtypeStruct((B, S, MIX_W), F32),
        scratch_shapes=[pltpu.VMEM((rows, LANE), BF),
                        pltpu.VMEM((rows, LANE), F32),
                        pltpu.VMEM((rows, 2 * LANE), F32),
                        pltpu.VMEM((tk, 2 * LANE), BF),
                        pltpu.VMEM((rows // 2, tk), F32),
                        pltpu.VMEM((rows // 2, tk), BF),
                        pltpu.VMEM((tq, tk), F32)],
        compiler_params=_cparams(("parallel", "parallel", "arbitrary")),
        name="nsa_selected",
    )(proj, proj, proj, sel, expand)


def _swa_combine_kernel(tq, rb, q_ref, kp_ref, kc_ref, vp_ref, vc_ref, g_ref, ocmp_ref, oslc_ref, bias_ref, o_ref,
                        s_ref, p_ref):
    scale = HEAD_DIM ** -0.5
    k = jnp.concatenate([kp_ref[...], kc_ref[...]], axis=0).astype(BF)
    v = jnp.concatenate([vp_ref[...], vc_ref[...]], axis=0).astype(BF)
    gates = _sigmoid(g_ref[...])
    for h in range(N_HEADS):
        cols = slice(h * HEAD_DIM, (h + 1) * HEAD_DIM)
        slot = h % 2
        s_ref[slot] = _dot_nt(q_ref[:, cols] * scale, k)
        for r in range(tq // rb):
            rows = slice(r * rb, (r + 1) * rb)
            s = s_ref[slot, rows, :] + bias_ref[rows, :]
            e = jnp.exp(s - jnp.max(s, axis=-1, keepdims=True))
            p_ref[slot, rows, :] = (e * (1.0 / jnp.sum(e, axis=-1, keepdims=True))).astype(BF)
        o_swa = jnp.dot(p_ref[slot], v, preferred_element_type=F32)
        o = (gates[:, 3 * h:3 * h + 1] * ocmp_ref[:, cols]
             + gates[:, 3 * h + 1:3 * h + 2] * oslc_ref[:, cols]
             + gates[:, 3 * h + 2:3 * h + 3] * o_swa)
        o_ref[:, cols] = o.astype(o_ref.dtype)


def nsa_window_and_combine(proj, o_cmp, o_slc):
    B, S, _ = proj.shape
    tq = SWA_WINDOW
    kvp = lambda off: pl.BlockSpec((None, tq, LANE), lambda b, i: (b, jnp.maximum(i - 1, 0), BLK_NKV + off))
    kvc = lambda off: pl.BlockSpec((None, tq, LANE), lambda b, i: (b, i, BLK_NKV + off))
    full = lambda: pl.BlockSpec((None, tq, MIX_W), lambda b, i: (b, i, 0))
    dist = jnp.arange(tq)[:, None] - (jnp.arange(SWA_WINDOW + tq)[None, :] - SWA_WINDOW)
    inside = (dist >= 0) & (dist < SWA_WINDOW)
    first = inside & (jnp.arange(SWA_WINDOW + tq)[None, :] >= SWA_WINDOW)
    bias = jnp.where(jnp.stack([first, inside]), 0.0, NEG).astype(F32)
    return pl.pallas_call(
        functools.partial(_swa_combine_kernel, tq, 16),
        grid=(B, S // tq),
        in_specs=[pl.BlockSpec((None, tq, MIX_W), lambda b, i: (b, i, BLK_NQ // 4)),
                  kvp(4), kvc(4), kvp(5), kvc(5),
                  pl.BlockSpec((None, tq, LANE), lambda b, i: (b, i, BLK_NG)),
                  full(), full(),
                  pl.BlockSpec((None, tq, SWA_WINDOW + tq), lambda b, i: (jnp.minimum(i, 1), 0, 0))],
        out_specs=full(),
        out_shape=jax.ShapeDtypeStruct((B, S, MIX_W), BF),
        scratch_shapes=[pltpu.VMEM((2, tq, SWA_WINDOW + tq), F32), pltpu.VMEM((2, tq, SWA_WINDOW + tq), BF)],
        compiler_params=_cparams(("parallel", "parallel")),
        name="nsa_window_combine",
    )(proj, proj, proj, proj, proj, proj, o_cmp, o_slc, bias)


def _gelu_tanh(x):
    return 0.5 * x * (1.0 + jnp.tanh(math.sqrt(2.0 / math.pi) * (x + 0.044715 * (x * x * x))))


def _gmlp_kernel(u_ref, v_ref, lnw_ref, lnb_ref, ws_ref, bs_ref, o_ref):
    C = ws_ref.shape[1]
    row = lax.broadcasted_iota(jnp.int32, (C, C), 0)
    col = lax.broadcasted_iota(jnp.int32, (C, C), 1)
    causal = col <= row
    for c in range(u_ref.shape[0] // C):
        rows = slice(c * C, (c + 1) * C)
        u = _gelu_tanh(u_ref[rows, :])
        v = _gelu_tanh(v_ref[rows, :])
        mu = jnp.mean(v, axis=-1, keepdims=True)
        var = jnp.mean(jnp.square(v - mu), axis=-1, keepdims=True)
        v = (v - mu) * lax.rsqrt(var + 1e-5) * lnw_ref[...] + lnb_ref[...]
        for g in range(N_HEADS):
            cols = slice(g * HEAD_DIM, (g + 1) * HEAD_DIM)
            w = jnp.where(causal, ws_ref[g], 0.0)
            vm = _dot(w, v[:, cols]) + bs_ref[:, cols]
            o_ref[rows, cols] = (u[:, cols] * vm).astype(o_ref.dtype)


def chunk_gmlp_mixer(proj, ln_w, ln_b, w_s, b_s, tt=512):
    B, S, _ = proj.shape
    G, C, _ = w_s.shape
    tt = min(tt, S)
    bias = jnp.repeat(b_s.T, HEAD_DIM, axis=1)
    return pl.pallas_call(
        _gmlp_kernel,
        grid=(B, S // tt),
        in_specs=[pl.BlockSpec((None, tt, MIX_W), lambda b, t: (b, t, BLK_GM // 4)),
                  pl.BlockSpec((None, tt, MIX_W), lambda b, t: (b, t, BLK_GM // 4 + 1)),
                  pl.BlockSpec((1, MIX_W), lambda b, t: (0, 0)),
                  pl.BlockSpec((1, MIX_W), lambda b, t: (0, 0)),
                  pl.BlockSpec((G, C, C), lambda b, t: (0, 0, 0)),
                  pl.BlockSpec((C, MIX_W), lambda b, t: (0, 0))],
        out_specs=pl.BlockSpec((None, tt, MIX_W), lambda b, t: (b, t, 0)),
        out_shape=jax.ShapeDtypeStruct((B, S, MIX_W), BF),
        compiler_params=_cparams(("parallel", "parallel")),
        name="chunk_gmlp",
    )(proj, proj, ln_w.reshape(1, MIX_W), ln_b.reshape(1, MIX_W), w_s, bias)


def _merge_kernel(oa_ref, ob_ref, oc_ref, od_ref, ga_ref, gb_ref, gc_ref, gd_ref, wb_ref, wo_ref, x_ref,
                  o_ref, acc_ref):
    j = pl.program_id(1)

    @pl.when(j == 0)
    def _():
        acc_ref[...] = jnp.zeros_like(acc_ref)

    merged = None
    for b, (o_b, g_b) in enumerate(((oa_ref, ga_ref), (ob_ref, gb_ref), (oc_ref, gc_ref), (od_ref, gd_ref))):
        term = _sigmoid(g_b[...]) * jnp.dot(o_b[...], wb_ref[b], preferred_element_type=F32)
        merged = term if merged is None else merged + term
    acc_ref[...] += jnp.dot(merged.astype(BF), wo_ref[...], preferred_element_type=F32)

    @pl.when(j == pl.num_programs(1) - 1)
    def _():
        o_ref[...] = x_ref[...] + acc_ref[...]


def merge_project(x, proj, branches, w_branch, w_out, layer, tm=512, tn=512):
    T, D = x.shape
    tm = min(tm, T)
    per = D // tn
    o_spec = pl.BlockSpec((tm, MIX_W), lambda i, j: (i, 0))
    g_spec = lambda b: pl.BlockSpec((tm, tn), lambda i, j: (i, (BLK_MG * LANE) // tn + b * per + j))
    return pl.pallas_call(
        _merge_kernel,
        grid=(T // tm, per),
        in_specs=[o_spec, o_spec, o_spec, o_spec, g_spec(0), g_spec(1), g_spec(2), g_spec(3),
                  pl.BlockSpec((None, N_BRANCH, MIX_W, tn), lambda i, j: (layer, 0, 0, j)),
                  pl.BlockSpec((None, tn, D), lambda i, j: (layer, j, 0)),
                  pl.BlockSpec((tm, D), lambda i, j: (i, 0))],
        out_specs=pl.BlockSpec((tm, D), lambda i, j: (i, 0)),
        out_shape=jax.ShapeDtypeStruct((T, D), F32),
        scratch_shapes=[pltpu.VMEM((tm, D), F32)],
        compiler_params=_cparams(("parallel", "arbitrary")),
        name="merge_project",
    )(*branches, proj, proj, proj, proj, w_branch, w_out, x)


def _cross_attn_kernel(q_ref, k_ref, v_ref, o_ref):
    dh = q_ref.shape[1] // XA_HEADS
    for h in range(XA_HEADS):
        cols = slice(h * dh, (h + 1) * dh)
        s = _dot_nt(q_ref[:, cols], k_ref[:, cols]) * (dh ** -0.5)
        m = jnp.max(s, axis=-1, keepdims=True)
        p = jnp.exp(s - m)
        p = p / jnp.sum(p, axis=-1, keepdims=True)
        o_ref[:, cols] = _dot(p, v_ref[:, cols]).astype(o_ref.dtype)


def cross_attention_core(q, kv, tq=512):
    B, S, D = q.shape
    M = kv.shape[1]
    tq = min(tq, S)
    return pl.pallas_call(
        _cross_attn_kernel,
        grid=(B, S // tq),
        in_specs=[pl.BlockSpec((None, tq, D), lambda b, i: (b, i, 0)),
                  pl.BlockSpec((None, M, D), lambda b, i: (b, 0, 0)),
                  pl.BlockSpec((None, M, D), lambda b, i: (b, 0, 1))],
        out_specs=pl.BlockSpec((None, tq, D), lambda b, i: (b, i, 0)),
        out_shape=jax.ShapeDtypeStruct((B, S, D), BF),
        compiler_params=_cparams(("parallel", "parallel")),
        name="cross_attention",
    )(q, kv, kv)


def _ffn_kernel(x_ref, nw_ref, wg_ref, wu_ref, wo_ref, o_ref, h_ref, acc_ref):
    j = pl.program_id(1)

    @pl.when(j == 0)
    def _():
        h_ref[...] = _rms(x_ref[...], nw_ref[...], 1e-6).astype(BF)
        acc_ref[...] = jnp.zeros_like(acc_ref)

    h = h_ref[...]
    gate = jnp.dot(h, wg_ref[...], preferred_element_type=F32)
    up = jnp.dot(h, wu_ref[...], preferred_element_type=F32)
    acc_ref[...] += jnp.dot((_silu(gate) * up).astype(BF), wo_ref[...], preferred_element_type=F32)

    @pl.when(j == pl.num_programs(1) - 1)
    def _():
        o_ref[...] = x_ref[...] + acc_ref[...]


def swiglu_ffn(x, norm_w, w_in, w_out, layer, tm=512, tf=512):
    T, D = x.shape
    F = w_out.shape[1]
    tm = min(tm, T)
    nf = F // tf
    return pl.pallas_call(
        _ffn_kernel,
        grid=(T // tm, nf),
        in_specs=[pl.BlockSpec((tm, D), lambda i, j: (i, 0)),
                  pl.BlockSpec((1, D), lambda i, j: (0, 0)),
                  pl.BlockSpec((None, D, tf), lambda i, j: (layer, 0, j)),
                  pl.BlockSpec((None, D, tf), lambda i, j: (layer, 0, nf + j)),
                  pl.BlockSpec((None, tf, D), lambda i, j: (layer, j, 0))],
        out_specs=pl.BlockSpec((tm, D), lambda i, j: (i, 0)),
        out_shape=jax.ShapeDtypeStruct((T, D), F32),
        scratch_shapes=[pltpu.VMEM((tm, D), BF), pltpu.VMEM((tm, D), F32)],
        compiler_params=_cparams(("parallel", "arbitrary")),
        name="swiglu_ffn",
    )(x, norm_w.reshape(1, D), w_in, w_in, w_out)


def _pack_w_in(w):
    n_gate = 3 * N_HEADS
    gate_end = BLK_NG * LANE + n_gate
    wt = jnp.swapaxes(w, 1, 2)
    pad = jnp.zeros((w.shape[0], BLK_GM * LANE - gate_end, w.shape[1]), w.dtype)
    packed = jnp.concatenate([wt[:, :gate_end], pad, wt[:, gate_end:]], axis=1)
    assert packed.shape[1] == N_PROJ, packed.shape
    return packed.astype(BF)


def kernel(x, mem, mem_norm_w, mix_norm_w, w_in, hg_lower_bounds, hg_norm_w, da_lambda_q1, da_lambda_k1, da_lambda_q2, da_lambda_k2, da_norm_w, nsa_pos_k, nsa_cmp_w1_k, nsa_cmp_w2_k, nsa_pos_v, nsa_cmp_w1_v, nsa_cmp_w2_v, gm_ln_w, gm_ln_b, gm_w_s, gm_b_s, w_branch, w_out, xa_norm_w, xa_w_q, xa_w_kv, xa_w_o, ffn_norm_w, ffn_w_in, ffn_w_out, final_norm_w):
    B, S, D = x.shape
    M = mem.shape[1]
    depth = w_in.shape[0]
    assert D == D_MODEL and S % (4 * SWA_WINDOW) == 0, (S, D)
    T = B * S
    x = x.reshape(T, D)
    mem2 = mem.reshape(B * M, D)
    w_in_p = _pack_w_in(w_in)
    w_branch_b, w_out_b = w_branch.astype(BF), w_out.astype(BF)
    xa_w_q_b, xa_w_kv_b, xa_w_o_b = xa_w_q.astype(BF), xa_w_kv.astype(BF), xa_w_o.astype(BF)
    ffn_w_in_b, ffn_w_out_b = ffn_w_in.astype(BF), ffn_w_out.astype(BF)
    for l in range(depth):
        lambda_init = 0.8 - 0.6 * math.exp(-0.3 * l)
        proj = norm_matmul(x, mix_norm_w[l], w_in_p, l, F32, tm=1024, w_is_transposed=True).reshape(B, S, N_PROJ)
        o_a = hgrn2_mixer(proj, hg_lower_bounds, hg_norm_w[l], l)
        lam_params = jnp.stack([da_lambda_q1[l], da_lambda_k1[l], da_lambda_q2[l], da_lambda_k2[l]])
        o_b = diff_attention_mixer(proj, lam_params, da_norm_w[l], lambda_init)
        nkv = lambda off: proj[:, :, (BLK_NKV + off) * LANE:(BLK_NKV + off + 1) * LANE]
        kc = compress_tokens(nkv(0), nsa_pos_k[l], nsa_cmp_w1_k[l].astype(BF), nsa_cmp_w2_k[l].astype(BF))
        vc = compress_tokens(nkv(1), nsa_pos_v[l], nsa_cmp_w1_v[l].astype(BF), nsa_cmp_w2_v[l].astype(BF))
        o_cmp, sel = nsa_compressed_and_select(proj, kc, vc)
        o_slc = nsa_selected_attention(proj, sel)
        o_c = nsa_window_and_combine(proj, o_cmp, o_slc)
        o_d = chunk_gmlp_mixer(proj, gm_ln_w[l], gm_ln_b[l], gm_w_s[l], gm_b_s[l])
        branches = [o.reshape(T, MIX_W) for o in (o_a, o_b, o_c, o_d)]
        x = merge_project(x, proj.reshape(T, N_PROJ), branches, w_branch_b, w_out_b, l)
        q = norm_matmul(x, xa_norm_w[l], xa_w_q_b, l, BF, tm=1024).reshape(B, S, D)
        kv = norm_matmul(mem2, mem_norm_w, xa_w_kv_b, l, F32).reshape(B, M, 2 * D)
        att = cross_attention_core(q, kv).reshape(T, D)
        x = matmul_residual(att, xa_w_o_b, l, x, tm=1024)
        x = swiglu_ffn(x, ffn_norm_w[l], ffn_w_in_b, ffn_w_out_b, l)
    return rmsnorm(x, final_norm_w).reshape(B, S, D)
```
